```python
import functools
import jax, jax.numpy as jnp
from jax import lax
import numpy as np

D_MODEL = 2048
BATCH = 4
SEQ = 2048
DEPTH = 1
DEC_BATCH = 32
DEC_SEQ = 4
PAST_LEN = 16384
PAGE_SIZE = 128

N_META = 16
D_CONV = D_MODEL // 2
CONV_WIDTH = 3
HEAD_DIM = 64
N_HEADS = (D_MODEL // 2) // HEAD_DIM
N_KV_HEADS = N_HEADS // 4
GROUP = N_HEADS // N_KV_HEADS
Q_DIM = N_HEADS * HEAD_DIM
KV_DIM = N_KV_HEADS * HEAD_DIM
WINDOW = 128
BLOCK = 128
D_FF = ((8 * D_MODEL // 3 + 127) // 128) * 128
N_BRANCH = 2
IN_SPLITS = (D_CONV, 2 * D_CONV, 3 * D_CONV, 3 * D_CONV + Q_DIM,
             3 * D_CONV + Q_DIM + KV_DIM, 3 * D_CONV + Q_DIM + 2 * KV_DIM,
             3 * D_CONV + Q_DIM + 2 * KV_DIM + D_MODEL)
D_IN = 3 * D_CONV + Q_DIM + 2 * KV_DIM + N_BRANCH * D_MODEL
EPS = 1e-6
NEG_INF = -1e30

kernel_name = "hybrid_conv_swa_macaron_step"


def rms_norm(x, g):
    xf = x.astype(jnp.float32)
    y = xf * lax.rsqrt(jnp.mean(xf * xf, axis=-1, keepdims=True) + EPS)
    return (y * g.astype(jnp.float32)).astype(x.dtype)


def half_ffn(x, g, w_up, w_down):
    h = rms_norm(x, g)
    gate, up = jnp.split(h @ w_up, 2, axis=-1)
    return x + 0.5 * ((jax.nn.silu(gate) * up) @ w_down)


def alibi_slopes():
    h = jnp.arange(1, N_HEADS + 1, dtype=jnp.float32)
    return jnp.exp2(-8.0 * h / N_HEADS).reshape(N_KV_HEADS, GROUP)


def project(h, w_in, q_g, k_g):
    z = h @ w_in
    xc, bg, cg, q, k, v, gc, ga = jnp.split(z, IN_SPLITS, axis=-1)
    lead = h.shape[:-1]
    q = rms_norm(q.reshape(*lead, N_KV_HEADS, GROUP, HEAD_DIM), q_g) * (HEAD_DIM ** -0.5)
    k = rms_norm(k.reshape(*lead, N_KV_HEADS, HEAD_DIM), k_g)
    v = v.reshape(*lead, N_KV_HEADS, HEAD_DIM)
    return xc, bg, cg, q, k, v, jax.nn.sigmoid(gc), jax.nn.sigmoid(ga)


def short_conv(xc, bg, cg, prefix, conv_w, w_out):
    u = cg * xc
    L = u.shape[1]
    up = jnp.concatenate([prefix.astype(u.dtype), u], axis=1)
    y = conv_w[0] * up[:, 0:L]
    for j in range(1, CONV_WIDTH):
        y = y + conv_w[j] * up[:, j:j + L]
    return (bg * y) @ w_out, up[:, L:]


def window_attention(q, k, v, pos_q, pos_k, sinks):
    s = jnp.einsum('...qkgd,...skd->...kgqs', q, k, preferred_element_type=jnp.float32)
    dist = (pos_q[..., :, None] - pos_k[..., None, :])[..., None, None, :, :]
    mask = (dist >= 0) & (dist <= WINDOW) & (pos_k[..., None, None, None, :] >= 0)
    s = jnp.where(mask, s - alibi_slopes()[:, :, None, None] * dist.astype(jnp.float32), NEG_INF)
    sink = jnp.broadcast_to(sinks.astype(jnp.float32).reshape(N_KV_HEADS, GROUP, 1, 1),
                            s.shape[:-1] + (1,))
    p = jax.nn.softmax(jnp.concatenate([s, sink], axis=-1), axis=-1)[..., :-1]
    return jnp.einsum('...kgqs,...skd->...qkgd', p.astype(v.dtype), v)


def prompt_attention(q, k, v, sinks):
    B, L = q.shape[0], q.shape[1]
    pad = BLOCK - N_META
    Lp = L + pad
    nb = Lp // BLOCK
    padf = lambda a: jnp.pad(a, ((0, 0), (pad, 0)) + ((0, 0),) * (a.ndim - 2))
    qb = padf(q).reshape(B, nb, BLOCK, N_KV_HEADS, GROUP, HEAD_DIM)
    kb = padf(k).reshape(B, nb, BLOCK, N_KV_HEADS, HEAD_DIM)
    vb = padf(v).reshape(B, nb, BLOCK, N_KV_HEADS, HEAD_DIM)
    shift = lambda a: jnp.concatenate([jnp.zeros_like(a[:, :1]), a[:, :-1]], axis=1)
    kk = jnp.concatenate([shift(kb), kb], axis=2)
    vv = jnp.concatenate([shift(vb), vb], axis=2)
    pos_q = (jnp.arange(Lp, dtype=jnp.int32) - pad).reshape(nb, BLOCK)
    pos_k = jnp.concatenate([pos_q - BLOCK, pos_q], axis=1)
    o = window_attention(qb, kk, vv, pos_q, pos_k, sinks)
    n_win = min(WINDOW, L)
    return o.reshape(B, Lp, Q_DIM)[:, pad:], k[:, L - n_win:], v[:, L - n_win:]


def sample_attention(q, k, v, cache_k, cache_v, sinks):
    Bd, T = q.shape[0], q.shape[1]
    n_buf = cache_k.shape[1]
    kk = jnp.concatenate([cache_k.astype(k.dtype), k], axis=1)
    vv = jnp.concatenate([cache_v.astype(v.dtype), v], axis=1)
    pos_q = jnp.arange(T, dtype=jnp.int32) + PAST_LEN
    pos_k = jnp.arange(n_buf + T, dtype=jnp.int32) + (PAST_LEN - n_buf)
    o = window_attention(q, kk, vv, pos_q, pos_k, sinks)
    return o.reshape(Bd, T, Q_DIM), kk[:, T:], vv[:, T:]


def layer_forward(x, conv_prefix, attend, g_ffn1, w_up1, w_down1, g_mix, w_in, q_g, k_g,
                  conv_w, w_conv_out, w_attn_out, w_o, g_ffn2, w_up2, w_down2):
    x = half_ffn(x, g_ffn1, w_up1, w_down1)
    h = rms_norm(x, g_mix)
    xc, bg, cg, q, k, v, gate_c, gate_a = project(h, w_in, q_g, k_g)
    conv_out, new_conv = short_conv(xc, bg, cg, conv_prefix, conv_w, w_conv_out)
    attn, new_k, new_v = attend(q, k, v)
    attn_out = attn @ w_attn_out
    x = x + (gate_c * conv_out + gate_a * attn_out) @ w_o
    x = half_ffn(x, g_ffn2, w_up2, w_down2)
    return x, new_conv, new_k, new_v


def setup_inputs(seed: int = 0) -> dict:
    key = jax.random.key(seed)
    ks = jax.random.split(key, 24)
    n_win = min(WINDOW, PAST_LEN)
    nrm = lambda k, shape, scale: jax.random.normal(k, shape, jnp.float32) * scale
    gain = lambda k, shape: 1.0 + 0.02 * jax.random.normal(k, shape, jnp.float32)
    return {
        "x_prompt": nrm(ks[0], (BATCH, SEQ, D_MODEL), 1.0),
        "x_sample": nrm(ks[1], (DEC_BATCH, DEC_SEQ, D_MODEL), 1.0),
        "state_conv": nrm(ks[2], (DEPTH, DEC_BATCH, CONV_WIDTH - 1, D_CONV), 1.0),
        "cache_k_win": nrm(ks[3], (DEPTH, DEC_BATCH, n_win, N_KV_HEADS, HEAD_DIM), 1.0),
        "cache_v_win": nrm(ks[4], (DEPTH, DEC_BATCH, n_win, N_KV_HEADS, HEAD_DIM), 1.0),
        "meta_tokens": nrm(ks[5], (N_META, D_MODEL), 1.0),
        "ffn1_norm": gain(ks[6], (DEPTH, D_MODEL)),
        "ffn1_w_up": nrm(ks[7], (DEPTH, D_MODEL, 2 * D_FF), D_MODEL ** -0.5),
        "ffn1_w_down": nrm(ks[8], (DEPTH, D_FF, D_MODEL), D_FF ** -0.5),
        "mix_norm": gain(ks[9], (DEPTH, D_MODEL)),
        "w_in": nrm(ks[10], (DEPTH, D_MODEL, D_IN), D_MODEL ** -0.5),
        "q_norm": gain(ks[11], (DEPTH, HEAD_DIM)),
        "k_norm": gain(ks[12], (DEPTH, HEAD_DIM)),
        "conv_w": nrm(ks[13], (DEPTH, CONV_WIDTH, D_CONV), CONV_WIDTH ** -0.5),
        "w_conv_out": nrm(ks[14], (DEPTH, D_CONV, D_MODEL), D_CONV ** -0.5),
        "attn_sinks": nrm(ks[15], (DEPTH, N_HEADS), 0.5),
        "w_attn_out": nrm(ks[16], (DEPTH, Q_DIM, D_MODEL), Q_DIM ** -0.5),
        "w_o": nrm(ks[17], (DEPTH, D_MODEL, D_MODEL), D_MODEL ** -0.5),
        "ffn2_norm": gain(ks[18], (DEPTH, D_MODEL)),
        "ffn2_w_up": nrm(ks[19], (DEPTH, D_MODEL, 2 * D_FF), D_MODEL ** -0.5),
        "ffn2_w_down": nrm(ks[20], (DEPTH, D_FF, D_MODEL), D_FF ** -0.5),
    }


def reference(x_prompt, x_sample, state_conv, cache_k_win, cache_v_win, meta_tokens,
              ffn1_norm, ffn1_w_up, ffn1_w_down, mix_norm, w_in, q_norm, k_norm,
              conv_w, w_conv_out, attn_sinks, w_attn_out, w_o,
              ffn2_norm, ffn2_w_up, ffn2_w_down):
    B = x_prompt.shape[0]
    meta = jnp.broadcast_to(meta_tokens[None].astype(x_prompt.dtype), (B, N_META, D_MODEL))
    xp = jnp.concatenate([meta, x_prompt], axis=1)
    xs = x_sample
    conv_p, k_p, v_p, conv_s, k_s, v_s = [], [], [], [], [], []
    for l in range(DEPTH):
        w = (ffn1_norm[l], ffn1_w_up[l], ffn1_w_down[l], mix_norm[l], w_in[l], q_norm[l], k_norm[l],
             conv_w[l], w_conv_out[l], w_attn_out[l], w_o[l], ffn2_norm[l], ffn2_w_up[l], ffn2_w_down[l])
        zero_prefix = jnp.zeros((B, CONV_WIDTH - 1, D_CONV), xp.dtype)
        xp, cp, kp, vp = layer_forward(
            xp, zero_prefix, functools.partial(prompt_attention, sinks=attn_sinks[l]), *w)
        xs, cs, ksn, vsn = layer_forward(
            xs, state_conv[l],
            functools.partial(sample_attention, cache_k=cache_k_win[l], cache_v=cache_v_win[l],
                              sinks=attn_sinks[l]), *w)
        conv_p.append(cp); k_p.append(kp); v_p.append(vp)
        conv_s.append(cs); k_s.append(ksn); v_s.append(vsn)
    y_prompt = xp[:, N_META:]
    y_sample = xs
    return (y_prompt, y_sample, jnp.stack(conv_p), jnp.stack(k_p), jnp.stack(v_p),
            jnp.stack(conv_s), jnp.stack(k_s), jnp.stack(v_s))
```

```python
import functools

import jax
import jax.numpy as jnp
from jax import lax
from jax.experimental import pallas as pl
from jax.experimental.pallas import tpu as pltpu

D_MODEL = 2048
BATCH = 4
SEQ = 2048
DEC_BATCH = 32
DEC_SEQ = 4
PAST_LEN = 16384
N_META = 16
D_CONV = D_MODEL // 2
CONV_WIDTH = 3
HEAD_DIM = 64
N_HEADS = (D_MODEL // 2) // HEAD_DIM
N_KV_HEADS = N_HEADS // 4
GROUP = N_HEADS // N_KV_HEADS
Q_DIM = N_HEADS * HEAD_DIM
KV_DIM = N_KV_HEADS * HEAD_DIM
WINDOW = 128
D_FF = ((8 * D_MODEL // 3 + 127) // 128) * 128
D_IN = 3 * D_CONV + Q_DIM + 2 * KV_DIM + 2 * D_MODEL
EPS = 1e-6
NEG_INF = -1e30

OFF_XC = 0
OFF_BG = D_CONV
OFF_CG = 2 * D_CONV
OFF_Q = 3 * D_CONV
OFF_K = OFF_Q + Q_DIM
OFF_V = OFF_K + KV_DIM
OFF_GC = OFF_V + KV_DIM
OFF_GA = OFF_GC + D_MODEL

N_MAIN = BATCH * SEQ
N_SAMPLE = DEC_BATCH * DEC_SEQ
N_TILES = 16
TM = N_MAIN // N_TILES
TS = 16
N_SMALL = N_TILES * TS
SAMPLE_ROW0 = N_META

TF = 512
D_FF_PAD = ((D_FF + TF - 1) // TF) * TF
N_FF_CHUNKS = D_FF_PAD // TF
TN_IN = 512
N_IN_CHUNKS = D_IN // TN_IN
TN_OUT = 512
N_OUT_CHUNKS = D_MODEL // TN_OUT

QBLK = 128
KPAD = QBLK - N_META
N_QBLK = SEQ // QBLK
KBUF_ROWS = QBLK + SEQ

VMEM_LIMIT = 48 * 1024 * 1024

_bf16 = jnp.bfloat16
_f32 = jnp.float32


def _params(sem):
    return pltpu.CompilerParams(dimension_semantics=sem, vmem_limit_bytes=VMEM_LIMIT)


def _rms_rows(x, g):
    ms = jnp.mean(x * x, axis=-1, keepdims=True)
    return x * lax.rsqrt(ms + EPS) * g


def _seg_rms(x, g_tiled, nseg):
    lane = lax.broadcasted_iota(jnp.int32, x.shape, x.ndim - 1)
    seg = lane // HEAD_DIM
    x2 = x * x
    ms = jnp.zeros_like(x)
    for s in range(nseg):
        tot = jnp.sum(jnp.where(seg == s, x2, 0.0), axis=-1, keepdims=True)
        ms = jnp.where(seg == s, tot, ms)
    ms = ms * (1.0 / HEAD_DIM)
    return x * lax.rsqrt(ms + EPS) * g_tiled


def _ffn_kernel(xm_ref, xs_ref, g_ref, wg_ref, wu_ref, wd_ref, om_ref, os_ref, h_ref):
    j = pl.program_id(1)

    @pl.when(j == 0)
    def _():
        g = g_ref[...]
        xm = xm_ref[...]
        h_ref[0:TM, :] = _rms_rows(xm, g).astype(_bf16)
        om_ref[...] = xm
        xs = xs_ref[...]
        h_ref[TM:TM + TS, :] = _rms_rows(xs, g).astype(_bf16)
        os_ref[...] = xs

    h = h_ref[...]
    gate = jnp.dot(h, wg_ref[...], preferred_element_type=_f32)
    up = jnp.dot(h, wu_ref[...], preferred_element_type=_f32)
    a = (gate * jax.nn.sigmoid(gate) * up * 0.5).astype(_bf16)
    r = jnp.dot(a, wd_ref[...], preferred_element_type=_f32)
    om_ref[...] += r[0:TM, :]
    os_ref[...] += r[TM:TM + TS, :]


def _ffn(xm, xs, g, wg, wu, wd):
    return pl.pallas_call(
        _ffn_kernel,
        grid=(N_TILES, N_FF_CHUNKS),
        in_specs=[
            pl.BlockSpec((TM, D_MODEL), lambda i, j: (i, 0)),
            pl.BlockSpec((TS, D_MODEL), lambda i, j: (i, 0)),
            pl.BlockSpec((1, D_MODEL), lambda i, j: (0, 0)),
            pl.BlockSpec((D_MODEL, TF), lambda i, j: (0, j)),
            pl.BlockSpec((D_MODEL, TF), lambda i, j: (0, j)),
            pl.BlockSpec((TF, D_MODEL), lambda i, j: (j, 0)),
        ],
        out_specs=[
            pl.BlockSpec((TM, D_MODEL), lambda i, j: (i, 0)),
            pl.BlockSpec((TS, D_MODEL), lambda i, j: (i, 0)),
        ],
        out_shape=[
            jax.ShapeDtypeStruct((N_MAIN, D_MODEL), _f32),
            jax.ShapeDtypeStruct((N_SMALL, D_MODEL), _f32),
        ],
        scratch_shapes=[pltpu.VMEM((TM + TS, D_MODEL), _bf16)],
        compiler_params=_params(("parallel", "arbitrary")),
        name="ffn",
    )(xm, xs, g, wg, wu, wd)


def _inproj_kernel(xm_ref, xs_ref, g_ref, w_ref, zm_ref, zs_ref, h_ref):
    j = pl.program_id(1)

    @pl.when(j == 0)
    def _():
        g = g_ref[...]
        h_ref[0:TM, :] = _rms_rows(xm_ref[...], g).astype(_bf16)
        h_ref[TM:TM + TS, :] = _rms_rows(xs_ref[...], g).astype(_bf16)

    z = jnp.dot(h_ref[...], w_ref[...], preferred_element_type=_f32)
    zm_ref[...] = z[0:TM, :]
    zs_ref[...] = z[TM:TM + TS, :]


def _inproj(xm, xs, g, w):
    return pl.pallas_call(
        _inproj_kernel,
        grid=(N_TILES, N_IN_CHUNKS),
        in_specs=[
            pl.BlockSpec((TM, D_MODEL), lambda i, j: (i, 0)),
            pl.BlockSpec((TS, D_MODEL), lambda i, j: (i, 0)),
            pl.BlockSpec((1, D_MODEL), lambda i, j: (0, 0)),
            pl.BlockSpec((D_MODEL, TN_IN), lambda i, j: (0, j)),
        ],
        out_specs=[
            pl.BlockSpec((TM, TN_IN), lambda i, j: (i, j)),
            pl.BlockSpec((TS, TN_IN), lambda i, j: (i, j)),
        ],
        out_shape=[
            jax.ShapeDtypeStruct((N_MAIN, D_IN), _f32),
            jax.ShapeDtypeStruct((N_SMALL, D_IN), _f32),
        ],
        scratch_shapes=[pltpu.VMEM((TM + TS, D_MODEL), _bf16)],
        compiler_params=_params(("parallel", "arbitrary")),
        name="inproj",
    )(xm, xs, g, w)


CONV_TILE = 512
CONV_TILES_PER_SEQ = SEQ // CONV_TILE
CONV_HALO = 8


def _conv_prompt_kernel(xc_ref, bg_ref, cg_ref, xcm_ref, cgm_ref, w_ref, cb_ref, nc_ref, ubuf):
    r = pl.program_id(1)

    @pl.when(r == 0)
    def _():
        um = cgm_ref[...] * xcm_ref[...]
        ubuf[0:CONV_HALO, :] = um[N_META - CONV_HALO:N_META, :]

    u = cg_ref[...] * xc_ref[...]
    ubuf[CONV_HALO:CONV_HALO + CONV_TILE, :] = u
    u1 = ubuf[CONV_HALO - 1:CONV_HALO - 1 + CONV_TILE, :]
    u2 = ubuf[CONV_HALO - 2:CONV_HALO - 2 + CONV_TILE, :]
    w = w_ref[...]
    y = w[0:1, :] * u2 + w[1:2, :] * u1 + w[2:3, :] * u
    cb_ref[...] = (bg_ref[...] * y).astype(_bf16)
    tail = ubuf[CONV_TILE:CONV_TILE + CONV_HALO, :]
    ubuf[0:CONV_HALO, :] = tail

    @pl.when(r == CONV_TILES_PER_SEQ - 1)
    def _():
        nc_ref[0] = tail[CONV_HALO - (CONV_WIDTH - 1):CONV_HALO, :]


def _conv_prompt(zm, zs, conv_w):
    nb = D_CONV
    return pl.pallas_call(
        _conv_prompt_kernel,
        grid=(BATCH, CONV_TILES_PER_SEQ),
        in_specs=[
            pl.BlockSpec((CONV_TILE, nb), lambda b, r: (b * CONV_TILES_PER_SEQ + r, OFF_XC // nb)),
            pl.BlockSpec((CONV_TILE, nb), lambda b, r: (b * CONV_TILES_PER_SEQ + r, OFF_BG // nb)),
            pl.BlockSpec((CONV_TILE, nb), lambda b, r: (b * CONV_TILES_PER_SEQ + r, OFF_CG // nb)),
            pl.BlockSpec((N_META, nb), lambda b, r: (0, OFF_XC // nb)),
            pl.BlockSpec((N_META, nb), lambda b, r: (0, OFF_CG // nb)),
            pl.BlockSpec((CONV_WIDTH, nb), lambda b, r: (0, 0)),
        ],
        out_specs=[
            pl.BlockSpec((CONV_TILE, nb), lambda b, r: (b * CONV_TILES_PER_SEQ + r, 0)),
            pl.BlockSpec((1, CONV_WIDTH - 1, nb), lambda b, r: (b, 0, 0)),
        ],
        out_shape=[
            jax.ShapeDtypeStruct((N_MAIN, nb), _bf16),
            jax.ShapeDtypeStruct((BATCH, CONV_WIDTH - 1, nb), _f32),
        ],
        scratch_shapes=[pltpu.VMEM((CONV_HALO + CONV_TILE, nb), _f32)],
        compiler_params=_params(("arbitrary", "arbitrary")),
        name="conv_prompt",
    )(zm, zm, zm, zs, zs, conv_w)


def _conv_sample_kernel(xc_ref, bg_ref, cg_ref, st_ref, w_ref, cb_ref, nc_ref):
    w = w_ref[...]
    w0, w1, w2 = w[0:1, :], w[1:2, :], w[2:3, :]
    rows = lambda t: slice(SAMPLE_ROW0 + t * DEC_BATCH, SAMPLE_ROW0 + (t + 1) * DEC_BATCH)
    up = [st_ref[0], st_ref[1]]
    for t in range(DEC_SEQ):
        up.append(cg_ref[rows(t), :] * xc_ref[rows(t), :])
    cb_ref[...] = jnp.zeros(cb_ref.shape, cb_ref.dtype)
    for t in range(DEC_SEQ):
        y = w0 * up[t] + w1 * up[t + 1] + w2 * up[t + 2]
        cb_ref[rows(t), :] = (bg_ref[rows(t), :] * y).astype(_bf16)
    nc_ref[0] = up[DEC_SEQ]
    nc_ref[1] = up[DEC_SEQ + 1]


def _conv_sample(zs, state_t, conv_w):
    nb = D_CONV
    return pl.pallas_call(
        _conv_sample_kernel,
        grid=(1,),
        in_specs=[
            pl.BlockSpec((N_SMALL, nb), lambda i: (0, OFF_XC // nb)),
            pl.BlockSpec((N_SMALL, nb), lambda i: (0, OFF_BG // nb)),
            pl.BlockSpec((N_SMALL, nb), lambda i: (0, OFF_CG // nb)),
            pl.BlockSpec((CONV_WIDTH - 1, DEC_BATCH, nb), lambda i: (0, 0, 0)),
            pl.BlockSpec((CONV_WIDTH, nb), lambda i: (0, 0)),
        ],
        out_specs=[
            pl.BlockSpec((N_SMALL, nb), lambda i: (0, 0)),
            pl.BlockSpec((CONV_WIDTH - 1, DEC_BATCH, nb), lambda i: (0, 0, 0)),
        ],
        out_shape=[
            jax.ShapeDtypeStruct((N_SMALL, nb), _bf16),
            jax.ShapeDtypeStruct((CONV_WIDTH - 1, DEC_BATCH, nb), _f32),
        ],
        compiler_params=_params(("arbitrary",)),
        name="conv_sample",
    )(zs, zs, zs, state_t, conv_w)


PAIR = 2 * HEAD_DIM
QPAIR = 2 * GROUP * HEAD_DIM
HEADS_PER_STEP = 2 * GROUP


def _attn_prompt_kernel(sink_ref, q_ref, k_ref, v_ref, km_ref, vm_ref, qg_ref, kg_ref,
                        o_ref, nk_ref, nv_ref, kbuf, vbuf, bias_ref):
    gp = pl.program_id(1)
    qg = qg_ref[...]
    kg = kg_ref[...]

    kn = _seg_rms(k_ref[...], kg, 2)
    kbuf[0:KPAD, :] = jnp.zeros((KPAD, PAIR), _bf16)
    vbuf[0:KPAD, :] = jnp.zeros((KPAD, PAIR), _bf16)
    kbuf[KPAD:QBLK, :] = _seg_rms(km_ref[...], kg, 2).astype(_bf16)
    vbuf[KPAD:QBLK, :] = vm_ref[...].astype(_bf16)
    kbuf[QBLK:KBUF_ROWS, :] = kn.astype(_bf16)
    vbuf[QBLK:KBUF_ROWS, :] = v_ref[...].astype(_bf16)
    nk_ref[0] = kn[SEQ - WINDOW:SEQ, :]
    nv_ref[0] = v_ref[SEQ - WINDOW:SEQ, :]

    row = lax.broadcasted_iota(jnp.int32, (QBLK, 2 * QBLK), 0)
    col = lax.broadcasted_iota(jnp.int32, (QBLK, 2 * QBLK), 1)
    dist = QBLK + row - col
    in_window = (dist >= 0) & (dist <= WINDOW)
    distf = dist.astype(_f32)
    for hh in range(HEADS_PER_STEP):
        head = (gp * HEADS_PER_STEP + hh + 1).astype(_f32)
        slope = jnp.exp2(jnp.full((QBLK, 2 * QBLK), -8.0 / N_HEADS, _f32) * head)
        bias_ref[hh] = jnp.where(in_window, -slope * distf, NEG_INF)

    lane = lax.broadcasted_iota(jnp.int32, (QBLK, PAIR), 1)
    low_half = lane < HEAD_DIM

    def block(i, carry):
        r0 = pl.multiple_of(i * QBLK, QBLK)
        q = q_ref[pl.ds(r0 - QBLK, QBLK), :]
        qn = _seg_rms(q, qg, HEADS_PER_STEP) * (HEAD_DIM ** -0.5)
        kk = kbuf[pl.ds(r0 - QBLK, 2 * QBLK), :]
        vv = vbuf[pl.ds(r0 - QBLK, 2 * QBLK), :]
        pad_key = col < jnp.where(i == 1, KPAD, 0)
        for g in range(2):
            keep = low_half if g == 0 else jnp.logical_not(low_half)
            outs = []
            for hp in range(GROUP):
                hh = g * GROUP + hp
                tile = qn[:, (hh // 2) * PAIR:(hh // 2 + 1) * PAIR]
                if hp % 2 != g:
                    tile = pltpu.roll(tile, HEAD_DIM, axis=1)
                qp = jnp.where(keep, tile, 0.0).astype(_bf16)
                s = lax.dot_general(qp, kk, (((1,), (1,)), ((), ())),
                                    preferred_element_type=_f32)
                s = jnp.where(pad_key, NEG_INF, s + bias_ref[hh])
                sink = sink_ref[gp * HEADS_PER_STEP + hh]
                m = jnp.maximum(jnp.max(s, axis=-1, keepdims=True), sink)
                e = jnp.exp(s - m)
                denom = jnp.sum(e, axis=-1, keepdims=True) + jnp.exp(sink - m)
                o = jnp.dot(e.astype(_bf16), vv, preferred_element_type=_f32)
                outs.append(o / denom)
            for pr in range(GROUP // 2):
                even, odd = outs[2 * pr], outs[2 * pr + 1]
                if g == 0:
                    odd = pltpu.roll(odd, HEAD_DIM, axis=1)
                else:
                    even = pltpu.roll(even, HEAD_DIM, axis=1)
                t = g * (GROUP // 2) + pr
                o_ref[pl.ds(r0 - QBLK, QBLK), t * PAIR:(t + 1) * PAIR] = (
                    jnp.where(low_half, even, odd))
        return carry

    lax.fori_loop(1, N_QBLK + 1, block, 0)


def _attn_prompt(zm, zs, q_gain, k_gain, sinks):
    qg = jnp.tile(q_gain.reshape(1, HEAD_DIM), (1, HEADS_PER_STEP))
    kg = jnp.tile(k_gain.reshape(1, HEAD_DIM), (1, 2))
    grid_spec = pltpu.PrefetchScalarGridSpec(
        num_scalar_prefetch=1,
        grid=(BATCH, N_KV_HEADS // 2),
        in_specs=[
            pl.BlockSpec((SEQ, QPAIR), lambda b, p, s: (b, OFF_Q // QPAIR + p)),
            pl.BlockSpec((SEQ, PAIR), lambda b, p, s: (b, OFF_K // PAIR + p)),
            pl.BlockSpec((SEQ, PAIR), lambda b, p, s: (b, OFF_V // PAIR + p)),
            pl.BlockSpec((N_META, PAIR), lambda b, p, s: (0, OFF_K // PAIR + p)),
            pl.BlockSpec((N_META, PAIR), lambda b, p, s: (0, OFF_V // PAIR + p)),
            pl.BlockSpec((1, QPAIR), lambda b, p, s: (0, 0)),
            pl.BlockSpec((1, PAIR), lambda b, p, s: (0, 0)),
        ],
        out_specs=[
            pl.BlockSpec((SEQ, QPAIR), lambda b, p, s: (b, p)),
            pl.BlockSpec((1, WINDOW, PAIR), lambda b, p, s: (b, 0, p)),
            pl.BlockSpec((1, WINDOW, PAIR), lambda b, p, s: (b, 0, p)),
        ],
        scratch_shapes=[
            pltpu.VMEM((KBUF_ROWS, PAIR), _bf16),
            pltpu.VMEM((KBUF_ROWS, PAIR), _bf16),
            pltpu.VMEM((HEADS_PER_STEP, QBLK, 2 * QBLK), _f32),
        ],
    )
    return pl.pallas_call(
        _attn_prompt_kernel,
        grid_spec=grid_spec,
        out_shape=[
            jax.ShapeDtypeStruct((N_MAIN, Q_DIM), _f32),
            jax.ShapeDtypeStruct((BATCH, WINDOW, KV_DIM), _f32),
            jax.ShapeDtypeStruct((BATCH, WINDOW, KV_DIM), _f32),
        ],
        compiler_params=_params(("arbitrary", "arbitrary")),
        name="attn_prompt",
    )(sinks, zm, zm, zm, zs, zs, qg, kg)


SEQ_CHUNK = 8
N_QROWS = N_HEADS * DEC_SEQ
N_KEYS = WINDOW + DEC_SEQ
N_KEYS_PAD = ((N_KEYS + 7) // 8) * 8


def _attn_sample_kernel(q_ref, kn_ref, vn_ref, ck_ref, cv_ref, qg_ref, kg_ref, sink_ref,
                        o_ref, nk_ref, nv_ref, kk_buf, vv_buf):
    q4 = _seg_rms(q_ref[...], qg_ref[...], N_KV_HEADS) * (HEAD_DIM ** -0.5)
    shp = (SEQ_CHUNK, N_QROWS, KV_DIM)
    row_kv = lax.broadcasted_iota(jnp.int32, shp, 1) // (GROUP * DEC_SEQ)
    lane_kv = lax.broadcasted_iota(jnp.int32, shp, 2) // HEAD_DIM
    own = row_kv == lane_kv
    qp = jnp.where(own, q4, 0.0)

    k_new = _seg_rms(kn_ref[...], kg_ref[...], N_KV_HEADS)
    kk_buf[:, 0:WINDOW, :] = ck_ref[...]
    kk_buf[:, WINDOW:N_KEYS, :] = k_new
    kk_buf[:, N_KEYS:N_KEYS_PAD, :] = jnp.zeros((SEQ_CHUNK, N_KEYS_PAD - N_KEYS, KV_DIM), _f32)
    vv_buf[:, 0:WINDOW, :] = cv_ref[...]
    vv_buf[:, WINDOW:N_KEYS, :] = vn_ref[...]
    vv_buf[:, N_KEYS:N_KEYS_PAD, :] = jnp.zeros((SEQ_CHUNK, N_KEYS_PAD - N_KEYS, KV_DIM), _f32)
    nk_ref[...] = kk_buf[:, DEC_SEQ:N_KEYS, :]
    nv_ref[...] = vv_buf[:, DEC_SEQ:N_KEYS, :]

    kk = kk_buf[...]
    vv = vv_buf[...]
    s = jnp.einsum('bqd,bkd->bqk', qp, kk, preferred_element_type=_f32)

    row = lax.broadcasted_iota(jnp.int32, (N_QROWS, N_KEYS_PAD), 0)
    col = lax.broadcasted_iota(jnp.int32, (N_QROWS, N_KEYS_PAD), 1)
    step = row % DEC_SEQ
    head = (row // DEC_SEQ + 1).astype(_f32)
    dist = step + WINDOW - col
    valid = (dist >= 0) & (dist <= WINDOW) & (col < N_KEYS)
    slope = jnp.exp2(head * (-8.0 / N_HEADS))
    bias = jnp.where(valid, -slope * dist.astype(_f32), NEG_INF)
    s = jnp.where(valid[None], s + bias[None], NEG_INF)
    sink = sink_ref[...][None]
    m = jnp.maximum(jnp.max(s, axis=-1, keepdims=True), sink)
    e = jnp.exp(s - m)
    denom = jnp.sum(e, axis=-1, keepdims=True) + jnp.exp(sink - m)
    o = jnp.einsum('bqk,bkd->bqd', e, vv, preferred_element_type=_f32) / denom
    o = jnp.where(own, o, 0.0)
    acc = o
    for g in range(1, N_KV_HEADS):
        acc = acc + pltpu.roll(o, g * HEAD_DIM, axis=2)
    o_ref[...] = acc[:, :, 0:HEAD_DIM]


def _attn_sample(q_s, k_new, v_new, cache_k, cache_v, q_gain, k_gain, sink_rows):
    qg = jnp.tile(q_gain.reshape(1, 1, HEAD_DIM), (1, 1, N_KV_HEADS))
    kg = jnp.tile(k_gain.reshape(1, 1, HEAD_DIM), (1, 1, N_KV_HEADS))
    c = SEQ_CHUNK
    return pl.pallas_call(
        _attn_sample_kernel,
        grid=(DEC_BATCH // c,),
        in_specs=[
            pl.BlockSpec((c, N_QROWS, KV_DIM), lambda i: (i, 0, 0)),
            pl.BlockSpec((c, DEC_SEQ, KV_DIM), lambda i: (i, 0, 0)),
            pl.BlockSpec((c, DEC_SEQ, KV_DIM), lambda i: (i, 0, 0)),
            pl.BlockSpec((c, WINDOW, KV_DIM), lambda i: (i, 0, 0)),
            pl.BlockSpec((c, WINDOW, KV_DIM), lambda i: (i, 0, 0)),
            pl.BlockSpec((1, 1, KV_DIM), lambda i: (0, 0, 0)),
            pl.BlockSpec((1, 1, KV_DIM), lambda i: (0, 0, 0)),
            pl.BlockSpec((N_QROWS, 1), lambda i: (0, 0)),
        ],
        out_specs=[
            pl.BlockSpec((c, N_QROWS, HEAD_DIM), lambda i: (i, 0, 0)),
            pl.BlockSpec((c, WINDOW, KV_DIM), lambda i: (i, 0, 0)),
            pl.BlockSpec((c, WINDOW, KV_DIM), lambda i: (i, 0, 0)),
        ],
        out_shape=[
            jax.ShapeDtypeStruct((DEC_BATCH, N_QROWS, HEAD_DIM), _f32),
            jax.ShapeDtypeStruct((DEC_BATCH, WINDOW, KV_DIM), _f32),
            jax.ShapeDtypeStruct((DEC_BATCH, WINDOW, KV_DIM), _f32),
        ],
        scratch_shapes=[
            pltpu.VMEM((c, N_KEYS_PAD, KV_DIM), _f32),
            pltpu.VMEM((c, N_KEYS_PAD, KV_DIM), _f32),
        ],
        compiler_params=_params(("arbitrary",)),
        name="attn_sample",
    )(q_s, k_new, v_new, cache_k, cache_v, qg, kg, sink_rows)


def _mix_kernel(x1m_ref, x1s_ref, cbm_ref, cbs_ref, atm_ref, ats_ref, gcm_ref, gcs_ref,
                gam_ref, gas_ref, wc_ref, wa_ref, wo_ref, om_ref, os_ref):
    n = pl.program_id(1)

    @pl.when(n == 0)
    def _():
        om_ref[...] = x1m_ref[...]
        os_ref[...] = x1s_ref[...]

    def branch(cb, at, gc, ga):
        yc = jnp.dot(cb, wc_ref[...], preferred_element_type=_f32)
        ya = jnp.dot(at.astype(_bf16), wa_ref[...], preferred_element_type=_f32)
        t = jax.nn.sigmoid(gc) * yc + jax.nn.sigmoid(ga) * ya
        return jnp.dot(t.astype(_bf16), wo_ref[...], preferred_element_type=_f32)

    om_ref[...] += branch(cbm_ref[...], atm_ref[...], gcm_ref[...], gam_ref[...])
    os_ref[...] += branch(cbs_ref[...], ats_ref[...], gcs_ref[...], gas_ref[...])


def _mix(x1m, x1s, cbm, cbs, atm, ats, zm, zs, wc, wa, wo):
    gc0 = OFF_GC // TN_OUT
    ga0 = OFF_GA // TN_OUT
    row = lambda i, n: (i, 0)
    return pl.pallas_call(
        _mix_kernel,
        grid=(N_TILES, N_OUT_CHUNKS),
        in_specs=[
            pl.BlockSpec((TM, D_MODEL), row),
            pl.BlockSpec((TS, D_MODEL), row),
            pl.BlockSpec((TM, D_CONV), row),
            pl.BlockSpec((TS, D_CONV), row),
            pl.BlockSpec((TM, Q_DIM), row),
            pl.BlockSpec((TS, Q_DIM), row),
            pl.BlockSpec((TM, TN_OUT), lambda i, n: (i, gc0 + n)),
            pl.BlockSpec((TS, TN_OUT), lambda i, n: (i, gc0 + n)),
            pl.BlockSpec((TM, TN_OUT), lambda i, n: (i, ga0 + n)),
            pl.BlockSpec((TS, TN_OUT), lambda i, n: (i, ga0 + n)),
            pl.BlockSpec((D_CONV, TN_OUT), lambda i, n: (0, n)),
            pl.BlockSpec((Q_DIM, TN_OUT), lambda i, n: (0, n)),
            pl.BlockSpec((TN_OUT, D_MODEL), lambda i, n: (n, 0)),
        ],
        out_specs=[
            pl.BlockSpec((TM, D_MODEL), row),
            pl.BlockSpec((TS, D_MODEL), row),
        ],
        out_shape=[
            jax.ShapeDtypeStruct((N_MAIN, D_MODEL), _f32),
            jax.ShapeDtypeStruct((N_SMALL, D_MODEL), _f32),
        ],
        compiler_params=_params(("parallel", "arbitrary")),
        name="mix",
    )(x1m, x1s, cbm, cbs, atm, ats, zm, zs, zm, zs, wc, wa, wo)


def _prep_ffn_weights(w_up, w_down):
    pad = D_FF_PAD - D_FF
    wg = jnp.pad(w_up[:, :D_FF], ((0, 0), (0, pad))).astype(_bf16)
    wu = jnp.pad(w_up[:, D_FF:], ((0, 0), (0, pad))).astype(_bf16)
    wd = jnp.pad(w_down, ((0, pad), (0, 0))).astype(_bf16)
    return wg, wu, wd


def _sample_rows(a):
    return a.transpose(1, 0, 2).reshape(N_SAMPLE, a.shape[-1])


def _to_small(sample_rows, dtype):
    w = sample_rows.shape[-1]
    return jnp.concatenate([
        jnp.zeros((SAMPLE_ROW0, w), dtype), sample_rows.astype(dtype),
        jnp.zeros((N_SMALL - SAMPLE_ROW0 - N_SAMPLE, w), dtype)], axis=0)


def kernel(x_prompt, x_sample, state_conv, cache_k_win, cache_v_win, meta_tokens, ffn1_norm, ffn1_w_up, ffn1_w_down, mix_norm, w_in, q_norm, k_norm, conv_w, w_conv_out, attn_sinks, w_attn_out, w_o, ffn2_norm, ffn2_w_up, ffn2_w_down):
    l = 0
    xm = x_prompt.reshape(N_MAIN, D_MODEL)
    xs = jnp.concatenate([
        meta_tokens.astype(_f32), _sample_rows(x_sample),
        jnp.zeros((N_SMALL - N_META - N_SAMPLE, D_MODEL), _f32)], axis=0)

    wg1, wu1, wd1 = _prep_ffn_weights(ffn1_w_up[l], ffn1_w_down[l])
    wg2, wu2, wd2 = _prep_ffn_weights(ffn2_w_up[l], ffn2_w_down[l])
    w_in_b = w_in[l].astype(_bf16)
    wc_b = w_conv_out[l].astype(_bf16)
    wa_b = w_attn_out[l].astype(_bf16)
    wo_b = w_o[l].astype(_bf16)

    x1m, x1s = _ffn(xm, xs, ffn1_norm[l].reshape(1, D_MODEL), wg1, wu1, wd1)
    zm, zs = _inproj(x1m, x1s, mix_norm[l].reshape(1, D_MODEL), w_in_b)

    cbm, new_conv_p = _conv_prompt(zm, zs, conv_w[l])
    cbs, new_conv_s = _conv_sample(zs, state_conv[l].transpose(1, 0, 2), conv_w[l])

    atm, new_k_p, new_v_p = _attn_prompt(zm, zs, q_norm[l], k_norm[l], attn_sinks[l])
    zsamp = zs[SAMPLE_ROW0:SAMPLE_ROW0 + N_SAMPLE]
    q_s = (zsamp[:, OFF_Q:OFF_Q + Q_DIM].reshape(DEC_SEQ, DEC_BATCH, N_HEADS, HEAD_DIM)
           .transpose(1, 2, 0, 3).reshape(DEC_BATCH, N_QROWS, HEAD_DIM))
    q_s = jnp.tile(q_s, (1, 1, N_KV_HEADS))
    k_new = zsamp[:, OFF_K:OFF_K + KV_DIM].reshape(DEC_SEQ, DEC_BATCH, KV_DIM).transpose(1, 0, 2)
    v_new = zsamp[:, OFF_V:OFF_V + KV_DIM].reshape(DEC_SEQ, DEC_BATCH, KV_DIM).transpose(1, 0, 2)
    sink_rows = jnp.repeat(attn_sinks[l].astype(_f32), DEC_SEQ).reshape(N_QROWS, 1)
    o_s, new_k_s, new_v_s = _attn_sample(
        q_s, k_new, v_new,
        cache_k_win[l].reshape(DEC_BATCH, WINDOW, KV_DIM),
        cache_v_win[l].reshape(DEC_BATCH, WINDOW, KV_DIM),
        q_norm[l], k_norm[l], sink_rows)
    at_rows = (o_s.reshape(DEC_BATCH, N_HEADS, DEC_SEQ, HEAD_DIM)
               .transpose(2, 0, 1, 3).reshape(N_SAMPLE, Q_DIM))
    ats = _to_small(at_rows, _f32)

    x2m, x2s = _mix(x1m, x1s, cbm, cbs, atm, ats, zm, zs, wc_b, wa_b, wo_b)
    ym, ys = _ffn(x2m, x2s, ffn2_norm[l].reshape(1, D_MODEL), wg2, wu2, wd2)

    y_prompt = ym.reshape(BATCH, SEQ, D_MODEL)
    y_sample = (ys[SAMPLE_ROW0:SAMPLE_ROW0 + N_SAMPLE]
                .reshape(DEC_SEQ, DEC_BATCH, D_MODEL).transpose(1, 0, 2))
    kv_shape_p = (1, BATCH, WINDOW, N_KV_HEADS, HEAD_DIM)
    kv_shape_s = (1, DEC_BATCH, WINDOW, N_KV_HEADS, HEAD_DIM)
    return (y_prompt, y_sample,
            new_conv_p[None],
            new_k_p.reshape(kv_shape_p), new_v_p.reshape(kv_shape_p),
            new_conv_s.transpose(1, 0, 2)[None],
            new_k_s.reshape(kv_shape_s), new_v_s.reshape(kv_shape_s))
```

```python
import functools

import jax
import jax.numpy as jnp
from jax import lax
from jax.experimental import pallas as pl
from jax.experimental.pallas import tpu as pltpu

D_MODEL = 2048
BATCH = 4
SEQ = 2048
DEC_BATCH = 32
DEC_SEQ = 4
PAST_LEN = 16384
N_META = 16
D_CONV = D_MODEL // 2
CONV_WIDTH = 3
HEAD_DIM = 64
N_HEADS = (D_MODEL // 2) // HEAD_DIM
N_KV_HEADS = N_HEADS // 4
GROUP = N_HEADS // N_KV_HEADS
Q_DIM = N_HEADS * HEAD_DIM
KV_DIM = N_KV_HEADS * HEAD_DIM
WINDOW = 128
D_FF = ((8 * D_MODEL // 3 + 127) // 128) * 128
D_IN = 3 * D_CONV + Q_DIM + 2 * KV_DIM + 2 * D_MODEL
EPS = 1e-6
NEG_INF = -1e30

SRC_K = 3 * D_CONV + Q_DIM
SRC_GC = SRC_K + 2 * KV_DIM
OFF_XC = 0
OFF_BG = D_CONV
OFF_CG = 2 * D_CONV
OFF_Q = 3 * D_CONV
OFF_GC = OFF_Q + Q_DIM
OFF_GA = OFF_GC + D_MODEL
OFF_K = OFF_GA + D_MODEL
OFF_V = OFF_K + KV_DIM

N_MAIN = BATCH * SEQ
N_SAMPLE = DEC_BATCH * DEC_SEQ
N_SMALL = 256
SAMPLE_ROW0 = N_META

FFN_TM, FFN_TS = 1024, 32
PROJ_TM, PROJ_TS = 512, 16
assert N_MAIN // FFN_TM == N_SMALL // FFN_TS and N_MAIN // PROJ_TM == N_SMALL // PROJ_TS

TF = 512
D_FF_PAD = ((D_FF + TF - 1) // TF) * TF
N_FF_CHUNKS = D_FF_PAD // TF
TN_IN = 2176
N_IN_CHUNKS = D_IN // TN_IN
TN_OUT = 512
N_OUT_CHUNKS = D_MODEL // TN_OUT

QBLK = 128
KPAD = QBLK - N_META
N_QBLK = SEQ // QBLK
KBUF_ROWS = QBLK + SEQ

VMEM_LIMIT = 56 * 1024 * 1024

_bf16 = jnp.bfloat16
_f32 = jnp.float32


def _params(sem):
    return pltpu.CompilerParams(dimension_semantics=sem, vmem_limit_bytes=VMEM_LIMIT)


def _rms_rows(x, g):
    ms = jnp.mean(x * x, axis=-1, keepdims=True)
    return x * lax.rsqrt(ms + EPS) * g


def _seg_rms(x, g_tiled, nseg):
    lane = lax.broadcasted_iota(jnp.int32, x.shape, x.ndim - 1)
    seg = lane // HEAD_DIM
    x2 = x * x
    ms = jnp.zeros_like(x)
    for s in range(nseg):
        tot = jnp.sum(jnp.where(seg == s, x2, 0.0), axis=-1, keepdims=True)
        ms = jnp.where(seg == s, tot, ms)
    ms = ms * (1.0 / HEAD_DIM)
    return x * lax.rsqrt(ms + EPS) * g_tiled


PREP_ROWS = 256


def _cast_cols_kernel(segments, zero_ranges, x_ref, o_ref):
    for src, width, dst in segments:
        o_ref[:, dst:dst + width] = x_ref[:, src:src + width].astype(_bf16)
    for lo, hi in zero_ranges:
        o_ref[:, lo:hi] = jnp.zeros((o_ref.shape[0], hi - lo), _bf16)


def _cast_cols(w, segments, zero_ranges, n_out):
    rows, n_in = w.shape
    return pl.pallas_call(
        functools.partial(_cast_cols_kernel, segments, zero_ranges),
        grid=(rows // PREP_ROWS,),
        in_specs=[pl.BlockSpec((PREP_ROWS, n_in), lambda i: (i, 0))],
        out_specs=pl.BlockSpec((PREP_ROWS, n_out), lambda i: (i, 0)),
        out_shape=jax.ShapeDtypeStruct((rows, n_out), _bf16),
        compiler_params=_params(("parallel",)),
        name="cast_cols",
    )(w)


def _cast_pad_rows_kernel(n_valid, x_ref, o_ref):
    row = pl.program_id(0) * TF + lax.broadcasted_iota(jnp.int32, x_ref.shape, 0)
    o_ref[...] = jnp.where(row < n_valid, x_ref[...], 0.0).astype(_bf16)


def _cast_pad_rows(w, n_out):
    rows, cols = w.shape
    return pl.pallas_call(
        functools.partial(_cast_pad_rows_kernel, rows),
        grid=(n_out // TF,),
        in_specs=[pl.BlockSpec((TF, cols), lambda i: (i, 0))],
        out_specs=pl.BlockSpec((TF, cols), lambda i: (i, 0)),
        out_shape=jax.ShapeDtypeStruct((n_out, cols), _bf16),
        compiler_params=_params(("parallel",)),
        name="cast_pad_rows",
    )(w)


def _prep_ffn_weights(w_up, w_down):
    wgu = _cast_cols(
        w_up, ((0, D_FF, 0), (D_FF, D_FF, D_FF_PAD)),
        ((D_FF, D_FF_PAD), (D_FF_PAD + D_FF, 2 * D_FF_PAD)), 2 * D_FF_PAD)
    return wgu, _cast_pad_rows(w_down, D_FF_PAD)


def _prep_w_in(w_in):
    n_gates = 2 * D_MODEL
    return _cast_cols(
        w_in, ((0, SRC_K, 0), (SRC_GC, n_gates, OFF_GC), (SRC_K, 2 * KV_DIM, OFF_K)), (), D_IN)


def _ffn_kernel(xm_ref, xs_ref, g_ref, wg_ref, wu_ref, wd_ref, om_ref, os_ref, h_ref):
    tm = FFN_TM
    j = pl.program_id(1)

    @pl.when(j == 0)
    def _():
        g = g_ref[...]
        xm = xm_ref[...]
        h_ref[0:tm, :] = _rms_rows(xm, g).astype(_bf16)
        om_ref[...] = xm
        xs = xs_ref[...]
        h_ref[tm:, :] = _rms_rows(xs, g).astype(_bf16)
        os_ref[...] = xs

    h = h_ref[...]
    gate = jnp.dot(h, wg_ref[...], preferred_element_type=_f32)
    up = jnp.dot(h, wu_ref[...], preferred_element_type=_f32)
    a = (gate * jax.nn.sigmoid(gate) * up * 0.5).astype(_bf16)
    for n in range(D_MODEL // TF):
        cols = slice(n * TF, (n + 1) * TF)
        r = jnp.dot(a, wd_ref[:, cols], preferred_element_type=_f32)
        om_ref[:, cols] += r[0:tm, :]
        os_ref[:, cols] += r[tm:, :]


def _ffn(xm, xs, g, wgu, wd):
    tm, ts = FFN_TM, FFN_TS
    return pl.pallas_call(
        _ffn_kernel,
        grid=(N_MAIN // tm, N_FF_CHUNKS),
        in_specs=[
            pl.BlockSpec((tm, D_MODEL), lambda i, j: (i, 0), pipeline_mode=pl.Buffered(1)),
            pl.BlockSpec((ts, D_MODEL), lambda i, j: (i, 0)),
            pl.BlockSpec((1, D_MODEL), lambda i, j: (0, 0)),
            pl.BlockSpec((D_MODEL, TF), lambda i, j: (0, j)),
            pl.BlockSpec((D_MODEL, TF), lambda i, j: (0, N_FF_CHUNKS + j)),
            pl.BlockSpec((TF, D_MODEL), lambda i, j: (j, 0)),
        ],
        out_specs=[
            pl.BlockSpec((tm, D_MODEL), lambda i, j: (i, 0)),
            pl.BlockSpec((ts, D_MODEL), lambda i, j: (i, 0)),
        ],
        out_shape=[
            jax.ShapeDtypeStruct((N_MAIN, D_MODEL), _f32),
            jax.ShapeDtypeStruct((N_SMALL, D_MODEL), _f32),
        ],
        scratch_shapes=[pltpu.VMEM((tm + ts, D_MODEL), _bf16)],
        compiler_params=_params(("parallel", "arbitrary")),
        name="ffn",
    )(xm, xs, g, wgu, wgu, wd)


def _inproj_kernel(xm_ref, xs_ref, g_ref, w_ref, zm_ref, zs_ref, h_ref):
    tm = PROJ_TM
    j = pl.program_id(1)

    @pl.when(j == 0)
    def _():
        g = g_ref[...]
        h_ref[0:tm, :] = _rms_rows(xm_ref[...], g).astype(_bf16)
        h_ref[tm:, :] = _rms_rows(xs_ref[...], g).astype(_bf16)

    z = jnp.dot(h_ref[...], w_ref[...], preferred_element_type=_f32)
    zm_ref[...] = z[0:tm, :]
    zs_ref[...] = z[tm:, :]


def _inproj(xm, xs, g, w):
    tm, ts = PROJ_TM, PROJ_TS
    return pl.pallas_call(
        _inproj_kernel,
        grid=(N_MAIN // tm, N_IN_CHUNKS),
        in_specs=[
            pl.BlockSpec((tm, D_MODEL), lambda i, j: (i, 0)),
            pl.BlockSpec((ts, D_MODEL), lambda i, j: (i, 0)),
            pl.BlockSpec((1, D_MODEL), lambda i, j: (0, 0)),
            pl.BlockSpec((D_MODEL, TN_IN), lambda i, j: (0, j)),
        ],
        out_specs=[
            pl.BlockSpec((tm, TN_IN), lambda i, j: (i, j)),
            pl.BlockSpec((ts, TN_IN), lambda i, j: (i, j)),
        ],
        out_shape=[
            jax.ShapeDtypeStruct((N_MAIN, D_IN), _f32),
            jax.ShapeDtypeStruct((N_SMALL, D_IN), _f32),
        ],
        scratch_shapes=[pltpu.VMEM((tm + ts, D_MODEL), _bf16)],
        compiler_params=_params(("parallel", "arbitrary")),
        name="inproj",
    )(xm, xs, g, w)


CONV_TILE = 512
CONV_TILES_PER_SEQ = SEQ // CONV_TILE
CONV_HALO = 8


def _conv_prompt_kernel(xc_ref, bg_ref, cg_ref, xcm_ref, cgm_ref, w_ref, cb_ref, nc_ref, ubuf):
    r = pl.program_id(1)

    @pl.when(r == 0)
    def _():
        um = cgm_ref[...] * xcm_ref[...]
        ubuf[0:CONV_HALO, :] = um[N_META - CONV_HALO:N_META, :]

    u = cg_ref[...] * xc_ref[...]
    ubuf[CONV_HALO:CONV_HALO + CONV_TILE, :] = u
    u1 = ubuf[CONV_HALO - 1:CONV_HALO - 1 + CONV_TILE, :]
    u2 = ubuf[CONV_HALO - 2:CONV_HALO - 2 + CONV_TILE, :]
    w = w_ref[...]
    y = w[0:1, :] * u2 + w[1:2, :] * u1 + w[2:3, :] * u
    cb_ref[...] = (bg_ref[...] * y).astype(_bf16)
    tail = ubuf[CONV_TILE:CONV_TILE + CONV_HALO, :]
    ubuf[0:CONV_HALO, :] = tail

    @pl.when(r == CONV_TILES_PER_SEQ - 1)
    def _():
        nc_ref[0] = tail[CONV_HALO - (CONV_WIDTH - 1):CONV_HALO, :]


def _conv_prompt(zm, zs, conv_w):
    nb = D_CONV
    return pl.pallas_call(
        _conv_prompt_kernel,
        grid=(BATCH, CONV_TILES_PER_SEQ),
        in_specs=[
            pl.BlockSpec((CONV_TILE, nb), lambda b, r: (b * CONV_TILES_PER_SEQ + r, OFF_XC // nb)),
            pl.BlockSpec((CONV_TILE, nb), lambda b, r: (b * CONV_TILES_PER_SEQ + r, OFF_BG // nb)),
            pl.BlockSpec((CONV_TILE, nb), lambda b, r: (b * CONV_TILES_PER_SEQ + r, OFF_CG // nb)),
            pl.BlockSpec((N_META, nb), lambda b, r: (0, OFF_XC // nb)),
            pl.BlockSpec((N_META, nb), lambda b, r: (0, OFF_CG // nb)),
            pl.BlockSpec((CONV_WIDTH, nb), lambda b, r: (0, 0)),
        ],
        out_specs=[
            pl.BlockSpec((CONV_TILE, nb), lambda b, r: (b * CONV_TILES_PER_SEQ + r, 0)),
            pl.BlockSpec((1, CONV_WIDTH - 1, nb), lambda b, r: (b, 0, 0)),
        ],
        out_shape=[
            jax.ShapeDtypeStruct((N_MAIN, nb), _bf16),
            jax.ShapeDtypeStruct((BATCH, CONV_WIDTH - 1, nb), _f32),
        ],
        scratch_shapes=[pltpu.VMEM((CONV_HALO + CONV_TILE, nb), _f32)],
        compiler_params=_params(("arbitrary", "arbitrary")),
        name="conv_prompt",
    )(zm, zm, zm, zs, zs, conv_w)


def _conv_sample_kernel(xc_ref, bg_ref, cg_ref, st_ref, w_ref, cb_ref, nc_ref):
    w = w_ref[...]
    w0, w1, w2 = w[0:1, :], w[1:2, :], w[2:3, :]
    rows = lambda t: slice(SAMPLE_ROW0 + t * DEC_BATCH, SAMPLE_ROW0 + (t + 1) * DEC_BATCH)
    up = [st_ref[0], st_ref[1]]
    for t in range(DEC_SEQ):
        up.append(cg_ref[rows(t), :] * xc_ref[rows(t), :])
    cb_ref[...] = jnp.zeros(cb_ref.shape, cb_ref.dtype)
    for t in range(DEC_SEQ):
        y = w0 * up[t] + w1 * up[t + 1] + w2 * up[t + 2]
        cb_ref[rows(t), :] = (bg_ref[rows(t), :] * y).astype(_bf16)
    nc_ref[0] = up[DEC_SEQ]
    nc_ref[1] = up[DEC_SEQ + 1]


def _conv_sample(zs, state_t, conv_w):
    nb = D_CONV
    return pl.pallas_call(
        _conv_sample_kernel,
        grid=(1,),
        in_specs=[
            pl.BlockSpec((N_SMALL, nb), lambda i: (0, OFF_XC // nb)),
            pl.BlockSpec((N_SMALL, nb), lambda i: (0, OFF_BG // nb)),
            pl.BlockSpec((N_SMALL, nb), lambda i: (0, OFF_CG // nb)),
            pl.BlockSpec((CONV_WIDTH - 1, DEC_BATCH, nb), lambda i: (0, 0, 0)),
            pl.BlockSpec((CONV_WIDTH, nb), lambda i: (0, 0)),
        ],
        out_specs=[
            pl.BlockSpec((N_SMALL, nb), lambda i: (0, 0)),
            pl.BlockSpec((CONV_WIDTH - 1, DEC_BATCH, nb), lambda i: (0, 0, 0)),
        ],
        out_shape=[
            jax.ShapeDtypeStruct((N_SMALL, nb), _bf16),
            jax.ShapeDtypeStruct((CONV_WIDTH - 1, DEC_BATCH, nb), _f32),
        ],
        compiler_params=_params(("arbitrary",)),
        name="conv_sample",
    )(zs, zs, zs, state_t, conv_w)


PAIR = 2 * HEAD_DIM
QPAIR = 2 * GROUP * HEAD_DIM
HEADS_PER_STEP = 2 * GROUP


def _attn_prompt_kernel(sink_ref, q_ref, k_ref, v_ref, km_ref, vm_ref, qg_ref, kg_ref,
                        o_ref, nk_ref, nv_ref, kbuf, vbuf, bias_ref):
    gp = pl.program_id(1)
    qg = qg_ref[...]
    kg = kg_ref[...]

    kn = _seg_rms(k_ref[...], kg, 2)
    kbuf[0:KPAD, :] = jnp.zeros((KPAD, PAIR), _bf16)
    vbuf[0:KPAD, :] = jnp.zeros((KPAD, PAIR), _bf16)
    kbuf[KPAD:QBLK, :] = _seg_rms(km_ref[...], kg, 2).astype(_bf16)
    vbuf[KPAD:QBLK, :] = vm_ref[...].astype(_bf16)
    kbuf[QBLK:KBUF_ROWS, :] = kn.astype(_bf16)
    vbuf[QBLK:KBUF_ROWS, :] = v_ref[...].astype(_bf16)
    nk_ref[0] = kn[SEQ - WINDOW:SEQ, :]
    nv_ref[0] = v_ref[SEQ - WINDOW:SEQ, :]

    row = lax.broadcasted_iota(jnp.int32, (QBLK, 2 * QBLK), 0)
    col = lax.broadcasted_iota(jnp.int32, (QBLK, 2 * QBLK), 1)
    dist = QBLK + row - col
    in_window = (dist >= 0) & (dist <= WINDOW)
    distf = dist.astype(_f32)
    for hh in range(HEADS_PER_STEP):
        head = (gp * HEADS_PER_STEP + hh + 1).astype(_f32)
        slope = jnp.exp2(jnp.full((QBLK, 2 * QBLK), -8.0 / N_HEADS, _f32) * head)
        bias_ref[hh] = jnp.where(in_window, -slope * distf, NEG_INF)

    lane = lax.broadcasted_iota(jnp.int32, (QBLK, PAIR), 1)
    low_half = lane < HEAD_DIM

    def block(i, carry):
        r0 = pl.multiple_of(i * QBLK, QBLK)
        q = q_ref[pl.ds(r0 - QBLK, QBLK), :]
        qn = _seg_rms(q, qg, HEADS_PER_STEP) * (HEAD_DIM ** -0.5)
        kk = kbuf[pl.ds(r0 - QBLK, 2 * QBLK), :]
        vv = vbuf[pl.ds(r0 - QBLK, 2 * QBLK), :]
        pad_key = col < jnp.where(i == 1, KPAD, 0)
        for g in range(2):
            keep = low_half if g == 0 else jnp.logical_not(low_half)
            outs = []
            for hp in range(GROUP):
                hh = g * GROUP + hp
                tile = qn[:, (hh // 2) * PAIR:(hh // 2 + 1) * PAIR]
                if hp % 2 != g:
                    tile = pltpu.roll(tile, HEAD_DIM, axis=1)
                qp = jnp.where(keep, tile, 0.0).astype(_bf16)
                s = lax.dot_general(qp, kk, (((1,), (1,)), ((), ())),
                                    preferred_element_type=_f32)
                s = jnp.where(pad_key, NEG_INF, s + bias_ref[hh])
                sink = sink_ref[gp * HEADS_PER_STEP + hh]
                m = jnp.maximum(jnp.max(s, axis=-1, keepdims=True), sink)
                e = jnp.exp(s - m)
                denom = jnp.sum(e, axis=-1, keepdims=True) + jnp.exp(sink - m)
                o = jnp.dot(e.astype(_bf16), vv, preferred_element_type=_f32)
                outs.append(o / denom)
            for pr in range(GROUP // 2):
                even, odd = outs[2 * pr], outs[2 * pr + 1]
                if g == 0:
                    odd = pltpu.roll(odd, HEAD_DIM, axis=1)
                else:
                    even = pltpu.roll(even, HEAD_DIM, axis=1)
                t = g * (GROUP // 2) + pr
                o_ref[pl.ds(r0 - QBLK, QBLK), t * PAIR:(t + 1) * PAIR] = (
                    jnp.where(low_half, even, odd).astype(_bf16))
        return carry

    lax.fori_loop(1, N_QBLK + 1, block, 0)


def _attn_prompt(zm, zs, q_gain, k_gain, sinks):
    qg = jnp.tile(q_gain.reshape(1, HEAD_DIM), (1, HEADS_PER_STEP))
    kg = jnp.tile(k_gain.reshape(1, HEAD_DIM), (1, 2))
    grid_spec = pltpu.PrefetchScalarGridSpec(
        num_scalar_prefetch=1,
        grid=(BATCH, N_KV_HEADS // 2),
        in_specs=[
            pl.BlockSpec((SEQ, QPAIR), lambda b, p, s: (b, OFF_Q // QPAIR + p)),
            pl.BlockSpec((SEQ, PAIR), lambda b, p, s: (b, OFF_K // PAIR + p)),
            pl.BlockSpec((SEQ, PAIR), lambda b, p, s: (b, OFF_V // PAIR + p)),
            pl.BlockSpec((N_META, PAIR), lambda b, p, s: (0, OFF_K // PAIR + p)),
            pl.BlockSpec((N_META, PAIR), lambda b, p, s: (0, OFF_V // PAIR + p)),
            pl.BlockSpec((1, QPAIR), lambda b, p, s: (0, 0)),
            pl.BlockSpec((1, PAIR), lambda b, p, s: (0, 0)),
        ],
        out_specs=[
            pl.BlockSpec((SEQ, QPAIR), lambda b, p, s: (b, p)),
            pl.BlockSpec((1, WINDOW, PAIR), lambda b, p, s: (b, 0, p)),
            pl.BlockSpec((1, WINDOW, PAIR), lambda b, p, s: (b, 0, p)),
        ],
        scratch_shapes=[
            pltpu.VMEM((KBUF_ROWS, PAIR), _bf16),
            pltpu.VMEM((KBUF_ROWS, PAIR), _bf16),
            pltpu.VMEM((HEADS_PER_STEP, QBLK, 2 * QBLK), _f32),
        ],
    )
    return pl.pallas_call(
        _attn_prompt_kernel,
        grid_spec=grid_spec,
        out_shape=[
            jax.ShapeDtypeStruct((N_MAIN, Q_DIM), _bf16),
            jax.ShapeDtypeStruct((BATCH, WINDOW, KV_DIM), _f32),
            jax.ShapeDtypeStruct((BATCH, WINDOW, KV_DIM), _f32),
        ],
        compiler_params=_params(("arbitrary", "arbitrary")),
        name="attn_prompt",
    )(sinks, zm, zm, zm, zs, zs, qg, kg)


SEQ_CHUNK = 8
N_QROWS = N_HEADS * DEC_SEQ
N_KEYS = WINDOW + DEC_SEQ
N_KEYS_PAD = ((N_KEYS + 7) // 8) * 8


def _attn_sample_kernel(q_ref, kn_ref, vn_ref, ck_ref, cv_ref, qg_ref, kg_ref, sink_ref,
                        o_ref, nk_ref, nv_ref, kk_buf, vv_buf):
    q4 = _seg_rms(q_ref[...], qg_ref[...], N_KV_HEADS) * (HEAD_DIM ** -0.5)
    shp = (SEQ_CHUNK, N_QROWS, KV_DIM)
    row_kv = lax.broadcasted_iota(jnp.int32, shp, 1) // (GROUP * DEC_SEQ)
    lane_kv = lax.broadcasted_iota(jnp.int32, shp, 2) // HEAD_DIM
    own = row_kv == lane_kv
    qp = jnp.where(own, q4, 0.0)

    k_new = _seg_rms(kn_ref[...], kg_ref[...], N_KV_HEADS)
    kk_buf[:, 0:WINDOW, :] = ck_ref[...]
    kk_buf[:, WINDOW:N_KEYS, :] = k_new
    kk_buf[:, N_KEYS:N_KEYS_PAD, :] = jnp.zeros((SEQ_CHUNK, N_KEYS_PAD - N_KEYS, KV_DIM), _f32)
    vv_buf[:, 0:WINDOW, :] = cv_ref[...]
    vv_buf[:, WINDOW:N_KEYS, :] = vn_ref[...]
    vv_buf[:, N_KEYS:N_KEYS_PAD, :] = jnp.zeros((SEQ_CHUNK, N_KEYS_PAD - N_KEYS, KV_DIM), _f32)
    nk_ref[...] = kk_buf[:, DEC_SEQ:N_KEYS, :]
    nv_ref[...] = vv_buf[:, DEC_SEQ:N_KEYS, :]

    kk = kk_buf[...]
    vv = vv_buf[...]
    s = jnp.einsum('bqd,bkd->bqk', qp, kk, preferred_element_type=_f32)

    row = lax.broadcasted_iota(jnp.int32, (N_QROWS, N_KEYS_PAD), 0)
    col = lax.broadcasted_iota(jnp.int32, (N_QROWS, N_KEYS_PAD), 1)
    step = row % DEC_SEQ
    head = (row // DEC_SEQ + 1).astype(_f32)
    dist = step + WINDOW - col
    valid = (dist >= 0) & (dist <= WINDOW) & (col < N_KEYS)
    slope = jnp.exp2(head * (-8.0 / N_HEADS))
    bias = jnp.where(valid, -slope * dist.astype(_f32), NEG_INF)
    s = jnp.where(valid[None], s + bias[None], NEG_INF)
    sink = sink_ref[...][None]
    m = jnp.maximum(jnp.max(s, axis=-1, keepdims=True), sink)
    e = jnp.exp(s - m)
    denom = jnp.sum(e, axis=-1, keepdims=True) + jnp.exp(sink - m)
    o = jnp.einsum('bqk,bkd->bqd', e, vv, preferred_element_type=_f32) / denom
    o = jnp.where(own, o, 0.0)
    acc = o
    for g in range(1, N_KV_HEADS):
        acc = acc + pltpu.roll(o, g * HEAD_DIM, axis=2)
    o_ref[...] = acc[:, :, 0:HEAD_DIM]


def _attn_sample(q_s, k_new, v_new, cache_k, cache_v, q_gain, k_gain, sink_rows):
    qg = jnp.tile(q_gain.reshape(1, 1, HEAD_DIM), (1, 1, N_KV_HEADS))
    kg = jnp.tile(k_gain.reshape(1, 1, HEAD_DIM), (1, 1, N_KV_HEADS))
    c = SEQ_CHUNK
    return pl.pallas_call(
        _attn_sample_kernel,
        grid=(DEC_BATCH // c,),
        in_specs=[
            pl.BlockSpec((c, N_QROWS, KV_DIM), lambda i: (i, 0, 0)),
            pl.BlockSpec((c, DEC_SEQ, KV_DIM), lambda i: (i, 0, 0)),
            pl.BlockSpec((c, DEC_SEQ, KV_DIM), lambda i: (i, 0, 0)),
            pl.BlockSpec((c, WINDOW, KV_DIM), lambda i: (i, 0, 0)),
            pl.BlockSpec((c, WINDOW, KV_DIM), lambda i: (i, 0, 0)),
            pl.BlockSpec((1, 1, KV_DIM), lambda i: (0, 0, 0)),
            pl.BlockSpec((1, 1, KV_DIM), lambda i: (0, 0, 0)),
            pl.BlockSpec((N_QROWS, 1), lambda i: (0, 0)),
        ],
        out_specs=[
            pl.BlockSpec((c, N_QROWS, HEAD_DIM), lambda i: (i, 0, 0)),
            pl.BlockSpec((c, WINDOW, KV_DIM), lambda i: (i, 0, 0)),
            pl.BlockSpec((c, WINDOW, KV_DIM), lambda i: (i, 0, 0)),
        ],
        out_shape=[
            jax.ShapeDtypeStruct((DEC_BATCH, N_QROWS, HEAD_DIM), _f32),
            jax.ShapeDtypeStruct((DEC_BATCH, WINDOW, KV_DIM), _f32),
            jax.ShapeDtypeStruct((DEC_BATCH, WINDOW, KV_DIM), _f32),
        ],
        scratch_shapes=[
            pltpu.VMEM((c, N_KEYS_PAD, KV_DIM), _f32),
            pltpu.VMEM((c, N_KEYS_PAD, KV_DIM), _f32),
        ],
        compiler_params=_params(("arbitrary",)),
        name="attn_sample",
    )(q_s, k_new, v_new, cache_k, cache_v, qg, kg, sink_rows)


def _mix_kernel(x1m_ref, x1s_ref, cbm_ref, cbs_ref, atm_ref, ats_ref, gcm_ref, gcs_ref,
                gam_ref, gas_ref, wc_ref, wa_ref, wo_ref, om_ref, os_ref,
                cb_all, at_all, t_all):
    tm = PROJ_TM
    n = pl.program_id(1)

    @pl.when(n == 0)
    def _():
        om_ref[...] = x1m_ref[...]
        os_ref[...] = x1s_ref[...]
        cb_all[0:tm, :] = cbm_ref[...]
        cb_all[tm:, :] = cbs_ref[...]
        at_all[0:tm, :] = atm_ref[...]
        at_all[tm:, :] = ats_ref[...]

    yc = jnp.dot(cb_all[...], wc_ref[...], preferred_element_type=_f32)
    ya = jnp.dot(at_all[...], wa_ref[...], preferred_element_type=_f32)
    t_all[0:tm, :] = (jax.nn.sigmoid(gcm_ref[...]) * yc[0:tm, :]
                      + jax.nn.sigmoid(gam_ref[...]) * ya[0:tm, :]).astype(_bf16)
    t_all[tm:, :] = (jax.nn.sigmoid(gcs_ref[...]) * yc[tm:, :]
                     + jax.nn.sigmoid(gas_ref[...]) * ya[tm:, :]).astype(_bf16)
    r = jnp.dot(t_all[...], wo_ref[...], preferred_element_type=_f32)
    om_ref[...] += r[0:tm, :]
    os_ref[...] += r[tm:, :]


def _mix(x1m, x1s, cbm, cbs, atm, ats, zm, zs, wc, wa, wo):
    tm, ts = PROJ_TM, PROJ_TS
    gc0 = OFF_GC // TN_OUT
    ga0 = OFF_GA // TN_OUT
    row = lambda i, n: (i, 0)
    return pl.pallas_call(
        _mix_kernel,
        grid=(N_MAIN // tm, N_OUT_CHUNKS),
        in_specs=[
            pl.BlockSpec((tm, D_MODEL), row),
            pl.BlockSpec((ts, D_MODEL), row),
            pl.BlockSpec((tm, D_CONV), row),
            pl.BlockSpec((ts, D_CONV), row),
            pl.BlockSpec((tm, Q_DIM), row),
            pl.BlockSpec((ts, Q_DIM), row),
            pl.BlockSpec((tm, TN_OUT), lambda i, n: (i, gc0 + n)),
            pl.BlockSpec((ts, TN_OUT), lambda i, n: (i, gc0 + n)),
            pl.BlockSpec((tm, TN_OUT), lambda i, n: (i, ga0 + n)),
            pl.BlockSpec((ts, TN_OUT), lambda i, n: (i, ga0 + n)),
            pl.BlockSpec((D_CONV, TN_OUT), lambda i, n: (0, n)),
            pl.BlockSpec((Q_DIM, TN_OUT), lambda i, n: (0, n)),
            pl.BlockSpec((TN_OUT, D_MODEL), lambda i, n: (n, 0)),
        ],
        out_specs=[
            pl.BlockSpec((tm, D_MODEL), row),
            pl.BlockSpec((ts, D_MODEL), row),
        ],
        out_shape=[
            jax.ShapeDtypeStruct((N_MAIN, D_MODEL), _f32),
            jax.ShapeDtypeStruct((N_SMALL, D_MODEL), _f32),
        ],
        scratch_shapes=[
            pltpu.VMEM((tm + ts, D_CONV), _bf16),
            pltpu.VMEM((tm + ts, Q_DIM), _bf16),
            pltpu.VMEM((tm + ts, TN_OUT), _bf16),
        ],
        compiler_params=_params(("parallel", "arbitrary")),
        name="mix",
    )(x1m, x1s, cbm, cbs, atm, ats, zm, zs, zm, zs, wc, wa, wo)


def _sample_rows(a):
    return a.transpose(1, 0, 2).reshape(N_SAMPLE, a.shape[-1])


def _to_small(sample_rows, dtype):
    w = sample_rows.shape[-1]
    return jnp.concatenate([
        jnp.zeros((SAMPLE_ROW0, w), dtype), sample_rows.astype(dtype),
        jnp.zeros((N_SMALL - SAMPLE_ROW0 - N_SAMPLE, w), dtype)], axis=0)


def kernel(x_prompt, x_sample, state_conv, cache_k_win, cache_v_win, meta_tokens, ffn1_norm, ffn1_w_up, ffn1_w_down, mix_norm, w_in, q_norm, k_norm, conv_w, w_conv_out, attn_sinks, w_attn_out, w_o, ffn2_norm, ffn2_w_up, ffn2_w_down):
    l = 0
    xm = x_prompt.reshape(N_MAIN, D_MODEL)
    xs = jnp.concatenate([
        meta_tokens.astype(_f32), _sample_rows(x_sample),
        jnp.zeros((N_SMALL - N_META - N_SAMPLE, D_MODEL), _f32)], axis=0)

    wgu1, wd1 = _prep_ffn_weights(ffn1_w_up[l], ffn1_w_down[l])
    wgu2, wd2 = _prep_ffn_weights(ffn2_w_up[l], ffn2_w_down[l])
    w_in_b = _prep_w_in(w_in[l])
    wc_b = w_conv_out[l].astype(_bf16)
    wa_b = w_attn_out[l].astype(_bf16)
    wo_b = w_o[l].astype(_bf16)

    x1m, x1s = _ffn(xm, xs, ffn1_norm[l].reshape(1, D_MODEL), wgu1, wd1)
    zm, zs = _inproj(x1m, x1s, mix_norm[l].reshape(1, D_MODEL), w_in_b)

    cbm, new_conv_p = _conv_prompt(zm, zs, conv_w[l])
    cbs, new_conv_s = _conv_sample(zs, state_conv[l].transpose(1, 0, 2), conv_w[l])

    atm, new_k_p, new_v_p = _attn_prompt(zm, zs, q_norm[l], k_norm[l], attn_sinks[l])
    zsamp = zs[SAMPLE_ROW0:SAMPLE_ROW0 + N_SAMPLE]
    q_s = (zsamp[:, OFF_Q:OFF_Q + Q_DIM].reshape(DEC_SEQ, DEC_BATCH, N_HEADS, HEAD_DIM)
           .transpose(1, 2, 0, 3).reshape(DEC_BATCH, N_QROWS, HEAD_DIM))
    q_s = jnp.tile(q_s, (1, 1, N_KV_HEADS))
    k_new = zsamp[:, OFF_K:OFF_K + KV_DIM].reshape(DEC_SEQ, DEC_BATCH, KV_DIM).transpose(1, 0, 2)
    v_new = zsamp[:, OFF_V:OFF_V + KV_DIM].reshape(DEC_SEQ, DEC_BATCH, KV_DIM).transpose(1, 0, 2)
    sink_rows = jnp.repeat(attn_sinks[l].astype(_f32), DEC_SEQ).reshape(N_QROWS, 1)
    o_s, new_k_s, new_v_s = _attn_sample(
        q_s, k_new, v_new,
        cache_k_win[l].reshape(DEC_BATCH, WINDOW, KV_DIM),
        cache_v_win[l].reshape(DEC_BATCH, WINDOW, KV_DIM),
        q_norm[l], k_norm[l], sink_rows)
    at_rows = (o_s.reshape(DEC_BATCH, N_HEADS, DEC_SEQ, HEAD_DIM)
               .transpose(2, 0, 1, 3).reshape(N_SAMPLE, Q_DIM))
    ats = _to_small(at_rows, _bf16)

    x2m, x2s = _mix(x1m, x1s, cbm, cbs, atm, ats, zm, zs, wc_b, wa_b, wo_b)
    ym, ys = _ffn(x2m, x2s, ffn2_norm[l].reshape(1, D_MODEL), wgu2, wd2)

    y_prompt = ym.reshape(BATCH, SEQ, D_MODEL)
    y_sample = (ys[SAMPLE_ROW0:SAMPLE_ROW0 + N_SAMPLE]
                .reshape(DEC_SEQ, DEC_BATCH, D_MODEL).transpose(1, 0, 2))
    kv_shape_p = (1, BATCH, WINDOW, N_KV_HEADS, HEAD_DIM)
    kv_shape_s = (1, DEC_BATCH, WINDOW, N_KV_HEADS, HEAD_DIM)
    return (y_prompt, y_sample,
            new_conv_p[None],
            new_k_p.reshape(kv_shape_p), new_v_p.reshape(kv_shape_p),
            new_conv_s.transpose(1, 0, 2)[None],
            new_k_s.reshape(kv_shape_s), new_v_s.reshape(kv_shape_s))
```

```python
import functools
from typing import NamedTuple

import jax
import jax.numpy as jnp
from jax import lax
from jax.experimental import pallas as pl
from jax.experimental.pallas import tpu as pltpu

D_MODEL = 2048
BATCH = 4
SEQ = 2048
DEC_BATCH = 32
DEC_SEQ = 4
PAST_LEN = 16384
N_META = 16
D_CONV = D_MODEL // 2
CONV_WIDTH = 3
HEAD_DIM = 64
N_HEADS = (D_MODEL // 2) // HEAD_DIM
N_KV_HEADS = N_HEADS // 4
GROUP = N_HEADS // N_KV_HEADS
Q_DIM = N_HEADS * HEAD_DIM
KV_DIM = N_KV_HEADS * HEAD_DIM
PAIR = 2 * HEAD_DIM
WINDOW = 128
D_FF = ((8 * D_MODEL // 3 + 127) // 128) * 128
D_IN = 3 * D_CONV + Q_DIM + 2 * KV_DIM + 2 * D_MODEL
EPS = 1e-6
NEG_INF = -1e30

SRC_K = 3 * D_CONV + Q_DIM
SRC_GC = SRC_K + 2 * KV_DIM
OFF_XC = 0
OFF_BG = D_CONV
OFF_CG = 2 * D_CONV
OFF_Q = 3 * D_CONV
OFF_GC = OFF_Q + Q_DIM
OFF_GA = OFF_GC + D_MODEL
OFF_K = OFF_GA + D_MODEL
OFF_V = OFF_K + KV_DIM

N_MAIN = BATCH * SEQ
N_SAMPLE = DEC_BATCH * DEC_SEQ
N_SMALL = 256
SAMPLE_ROW0 = N_META

N_TILES = 16
TM = N_MAIN // N_TILES
TS = N_SMALL // N_TILES

TF = 512
D_FF_PAD = ((D_FF + TF - 1) // TF) * TF
N_FF_CHUNKS = D_FF_PAD // TF
FFN_STEPS = N_TILES * N_FF_CHUNKS
TN_IN = 2176
N_IN_CHUNKS = D_IN // TN_IN
Q_CHUNK = OFF_Q // TN_IN
K_CHUNK = OFF_K // TN_IN
assert (OFF_Q + Q_DIM - 1) // TN_IN == Q_CHUNK and (OFF_K + KV_DIM - 1) // TN_IN == K_CHUNK
TN_OUT = 512
N_OUT_CHUNKS = D_MODEL // TN_OUT

QBLK = 128
KPAD = QBLK - N_META
N_QBLK = SEQ // QBLK
KBUF_ROWS = QBLK + SEQ

VMEM_LIMIT = 56 * 1024 * 1024

_bf16 = jnp.bfloat16
_f32 = jnp.float32


def _params(sem):
    return pltpu.CompilerParams(dimension_semantics=sem, vmem_limit_bytes=VMEM_LIMIT)


def _rms_rows(x, g):
    ms = jnp.mean(x * x, axis=-1, keepdims=True)
    return x * lax.rsqrt(ms + EPS) * g


def _pair_rms(x, g_pair):
    low = lax.broadcasted_iota(jnp.int32, x.shape, 1) < HEAD_DIM
    x2 = x * x
    s_lo = jnp.sum(jnp.where(low, x2, 0.0), axis=-1, keepdims=True)
    s_hi = jnp.sum(jnp.where(low, 0.0, x2), axis=-1, keepdims=True)
    ms = jnp.where(low, s_lo, s_hi) * (1.0 / HEAD_DIM)
    return x * lax.rsqrt(ms + EPS) * g_pair


class _CastJob(NamedTuple):
    slab: int
    n_in_slabs: int
    n_out_slabs: int
    n_in_cols: int
    n_out_cols: int
    segments: tuple
    zero_ranges: tuple


def _cast_job(rows_in, cols_in, slab, segments=None, zero_ranges=(), rows_out=None, cols_out=None):
    rows_out = rows_in if rows_out is None else rows_out
    cols_out = cols_in if cols_out is None else cols_out
    segments = ((0, cols_in, 0),) if segments is None else segments
    return _CastJob(slab, rows_in // slab, rows_out // slab, cols_in, cols_out,
                    tuple(segments), tuple(zero_ranges))


def _job_specs(job, step_of):
    in_idx = lambda *ids: (jnp.minimum(step_of(*ids), job.n_in_slabs - 1), 0)
    out_idx = lambda *ids: (jnp.minimum(step_of(*ids), job.n_out_slabs - 1), 0)
    return (pl.BlockSpec((job.slab, job.n_in_cols), in_idx),
            pl.BlockSpec((job.slab, job.n_out_cols), out_idx))


def _run_cast_job(job, step, x_ref, o_ref):
    @pl.when(step < job.n_in_slabs)
    def _():
        for src, width, dst in job.segments:
            o_ref[:, dst:dst + width] = x_ref[:, src:src + width].astype(_bf16)
        for lo, hi in job.zero_ranges:
            o_ref[:, lo:hi] = jnp.zeros((job.slab, hi - lo), _bf16)

    if job.n_out_slabs > job.n_in_slabs:
        @pl.when((step >= job.n_in_slabs) & (step < job.n_out_slabs))
        def _():
            o_ref[...] = jnp.zeros(o_ref.shape, _bf16)


def _cast_kernel(job, x_ref, o_ref):
    _run_cast_job(job, pl.program_id(0), x_ref, o_ref)


def _cast(job, w):
    in_spec, out_spec = _job_specs(job, lambda i: i)
    return pl.pallas_call(
        functools.partial(_cast_kernel, job),
        grid=(job.n_out_slabs,),
        in_specs=[in_spec],
        out_specs=out_spec,
        out_shape=jax.ShapeDtypeStruct((job.n_out_slabs * job.slab, job.n_out_cols), _bf16),
        compiler_params=_params(("arbitrary",)),
        name="cast",
    )(w)


def _w_up_job(slab):
    return _cast_job(D_MODEL, 2 * D_FF, slab,
                     segments=((0, D_FF, 0), (D_FF, D_FF, D_FF_PAD)),
                     zero_ranges=((D_FF, D_FF_PAD), (D_FF_PAD + D_FF, 2 * D_FF_PAD)),
                     cols_out=2 * D_FF_PAD)


def _w_down_job(slab):
    return _cast_job(D_FF, D_MODEL, slab, rows_out=D_FF_PAD)


def _w_in_job(slab):
    return _cast_job(D_MODEL, D_IN, slab,
                     segments=((0, SRC_K, 0), (SRC_GC, 2 * D_MODEL, OFF_GC),
                               (SRC_K, 2 * KV_DIM, OFF_K)))


def _ffn_kernel(jobs, *refs):
    nj = len(jobs)
    xm_ref, xs_ref, g_ref, wg_ref, wu_ref, wd_ref = refs[:6]
    job_in = refs[6:6 + nj]
    om_ref, os_ref = refs[6 + nj:8 + nj]
    job_out = refs[8 + nj:8 + 2 * nj]
    h_ref = refs[8 + 2 * nj]
    i = pl.program_id(0)
    j = pl.program_id(1)

    @pl.when(j == 0)
    def _():
        g = g_ref[...]
        xm = xm_ref[...]
        h_ref[0:TM, :] = _rms_rows(xm, g).astype(_bf16)
        om_ref[...] = xm
        xs = xs_ref[...]
        h_ref[TM:, :] = _rms_rows(xs, g).astype(_bf16)
        os_ref[...] = xs

    h = h_ref[...]
    gate = jnp.dot(h, wg_ref[...], preferred_element_type=_f32)
    up = jnp.dot(h, wu_ref[...], preferred_element_type=_f32)
    a = (gate * jax.nn.sigmoid(gate) * up * 0.5).astype(_bf16)
    r = jnp.dot(a, wd_ref[...], preferred_element_type=_f32)
    om_ref[...] += r[0:TM, :]
    os_ref[...] += r[TM:, :]

    for job, x_ref, o_ref in zip(jobs, job_in, job_out):
        _run_cast_job(job, i * N_FF_CHUNKS + j, x_ref, o_ref)


def _ffn(xm, xs, g, wgu, wd, jobs=(), job_inputs=()):
    assert all(job.n_out_slabs <= FFN_STEPS for job in jobs)
    job_specs = [_job_specs(job, lambda i, j: i * N_FF_CHUNKS + j) for job in jobs]
    return pl.pallas_call(
        functools.partial(_ffn_kernel, tuple(jobs)),
        grid=(N_TILES, N_FF_CHUNKS),
        in_specs=[
            pl.BlockSpec((TM, D_MODEL), lambda i, j: (i, 0)),
            pl.BlockSpec((TS, D_MODEL), lambda i, j: (i, 0)),
            pl.BlockSpec((1, D_MODEL), lambda i, j: (0, 0)),
            pl.BlockSpec((D_MODEL, TF), lambda i, j: (0, j)),
            pl.BlockSpec((D_MODEL, TF), lambda i, j: (0, N_FF_CHUNKS + j)),
            pl.BlockSpec((TF, D_MODEL), lambda i, j: (j, 0)),
        ] + [s[0] for s in job_specs],
        out_specs=[
            pl.BlockSpec((TM, D_MODEL), lambda i, j: (i, 0)),
            pl.BlockSpec((TS, D_MODEL), lambda i, j: (i, 0)),
        ] + [s[1] for s in job_specs],
        out_shape=[
            jax.ShapeDtypeStruct((N_MAIN, D_MODEL), _f32),
            jax.ShapeDtypeStruct((N_SMALL, D_MODEL), _f32),
        ] + [jax.ShapeDtypeStruct((job.n_out_slabs * job.slab, job.n_out_cols), _bf16)
             for job in jobs],
        scratch_shapes=[pltpu.VMEM((TM + TS, D_MODEL), _bf16)],
        compiler_params=_params(("arbitrary", "arbitrary")),
        name="ffn",
    )(xm, xs, g, wgu, wgu, wd, *job_inputs)


def _inproj_kernel(xm_ref, xs_ref, g_ref, w_ref, qg_ref, kg_ref, zm_ref, zs_ref, h_ref):
    j = pl.program_id(1)

    @pl.when(j == 0)
    def _():
        g = g_ref[...]
        h_ref[0:TM, :] = _rms_rows(xm_ref[...], g).astype(_bf16)
        h_ref[TM:, :] = _rms_rows(xs_ref[...], g).astype(_bf16)

    z = jnp.dot(h_ref[...], w_ref[...], preferred_element_type=_f32)
    zm_ref[...] = z[0:TM, :]
    zs_ref[...] = z[TM:, :]

    def norm_heads(chunk, col0, width, gain_ref, scale):
        @pl.when(j == chunk)
        def _():
            g_pair = gain_ref[...]
            for z_ref in (zm_ref, zs_ref):
                for t in range(width // PAIR):
                    cols = slice(col0 + t * PAIR, col0 + (t + 1) * PAIR)
                    z_ref[:, cols] = _pair_rms(z_ref[:, cols], g_pair) * scale

    norm_heads(Q_CHUNK, OFF_Q - Q_CHUNK * TN_IN, Q_DIM, qg_ref, HEAD_DIM ** -0.5)
    norm_heads(K_CHUNK, OFF_K - K_CHUNK * TN_IN, KV_DIM, kg_ref, 1.0)


def _inproj(xm, xs, g, w, q_gain, k_gain):
    qg = jnp.tile(q_gain.reshape(1, HEAD_DIM), (1, 2))
    kg = jnp.tile(k_gain.reshape(1, HEAD_DIM), (1, 2))
    return pl.pallas_call(
        _inproj_kernel,
        grid=(N_TILES, N_IN_CHUNKS),
        in_specs=[
            pl.BlockSpec((TM, D_MODEL), lambda i, j: (i, 0)),
            pl.BlockSpec((TS, D_MODEL), lambda i, j: (i, 0)),
            pl.BlockSpec((1, D_MODEL), lambda i, j: (0, 0)),
            pl.BlockSpec((D_MODEL, TN_IN), lambda i, j: (0, j)),
            pl.BlockSpec((1, PAIR), lambda i, j: (0, 0)),
            pl.BlockSpec((1, PAIR), lambda i, j: (0, 0)),
        ],
        out_specs=[
            pl.BlockSpec((TM, TN_IN), lambda i, j: (i, j)),
            pl.BlockSpec((TS, TN_IN), lambda i, j: (i, j)),
        ],
        out_shape=[
            jax.ShapeDtypeStruct((N_MAIN, D_IN), _f32),
            jax.ShapeDtypeStruct((N_SMALL, D_IN), _f32),
        ],
        scratch_shapes=[pltpu.VMEM((TM + TS, D_MODEL), _bf16)],
        compiler_params=_params(("parallel", "arbitrary")),
        name="inproj",
    )(xm, xs, g, w, qg, kg)


CONV_TILE = 512
CONV_TILES_PER_SEQ = SEQ // CONV_TILE
CONV_HALO = 8


def _conv_prompt_kernel(xc_ref, bg_ref, cg_ref, xcm_ref, cgm_ref, w_ref, cb_ref, nc_ref, ubuf):
    r = pl.program_id(1)

    @pl.when(r == 0)
    def _():
        um = cgm_ref[...] * xcm_ref[...]
        ubuf[0:CONV_HALO, :] = um[N_META - CONV_HALO:N_META, :]

    u = cg_ref[...] * xc_ref[...]
    ubuf[CONV_HALO:CONV_HALO + CONV_TILE, :] = u
    u1 = ubuf[CONV_HALO - 1:CONV_HALO - 1 + CONV_TILE, :]
    u2 = ubuf[CONV_HALO - 2:CONV_HALO - 2 + CONV_TILE, :]
    w = w_ref[...]
    y = w[0:1, :] * u2 + w[1:2, :] * u1 + w[2:3, :] * u
    cb_ref[...] = (bg_ref[...] * y).astype(_bf16)
    tail = ubuf[CONV_TILE:CONV_TILE + CONV_HALO, :]
    ubuf[0:CONV_HALO, :] = tail

    @pl.when(r == CONV_TILES_PER_SEQ - 1)
    def _():
        nc_ref[0] = tail[CONV_HALO - (CONV_WIDTH - 1):CONV_HALO, :]


def _conv_prompt(zm, zs, conv_w):
    nb = D_CONV
    return pl.pallas_call(
        _conv_prompt_kernel,
        grid=(BATCH, CONV_TILES_PER_SEQ),
        in_specs=[
            pl.BlockSpec((CONV_TILE, nb), lambda b, r: (b * CONV_TILES_PER_SEQ + r, OFF_XC // nb)),
            pl.BlockSpec((CONV_TILE, nb), lambda b, r: (b * CONV_TILES_PER_SEQ + r, OFF_BG // nb)),
            pl.BlockSpec((CONV_TILE, nb), lambda b, r: (b * CONV_TILES_PER_SEQ + r, OFF_CG // nb)),
            pl.BlockSpec((N_META, nb), lambda b, r: (0, OFF_XC // nb)),
            pl.BlockSpec((N_META, nb), lambda b, r: (0, OFF_CG // nb)),
            pl.BlockSpec((CONV_WIDTH, nb), lambda b, r: (0, 0)),
        ],
        out_specs=[
            pl.BlockSpec((CONV_TILE, nb), lambda b, r: (b * CONV_TILES_PER_SEQ + r, 0)),
            pl.BlockSpec((1, CONV_WIDTH - 1, nb), lambda b, r: (b, 0, 0)),
        ],
        out_shape=[
            jax.ShapeDtypeStruct((N_MAIN, nb), _bf16),
            jax.ShapeDtypeStruct((BATCH, CONV_WIDTH - 1, nb), _f32),
        ],
        scratch_shapes=[pltpu.VMEM((CONV_HALO + CONV_TILE, nb), _f32)],
        compiler_params=_params(("arbitrary", "arbitrary")),
        name="conv_prompt",
    )(zm, zm, zm, zs, zs, conv_w)


def _conv_sample_kernel(xc_ref, bg_ref, cg_ref, st_ref, w_ref, cb_ref, nc_ref):
    w = w_ref[...]
    w0, w1, w2 = w[0:1, :], w[1:2, :], w[2:3, :]
    rows = lambda t: slice(SAMPLE_ROW0 + t * DEC_BATCH, SAMPLE_ROW0 + (t + 1) * DEC_BATCH)
    up = [st_ref[0], st_ref[1]]
    for t in range(DEC_SEQ):
        up.append(cg_ref[rows(t), :] * xc_ref[rows(t), :])
    cb_ref[...] = jnp.zeros(cb_ref.shape, cb_ref.dtype)
    for t in range(DEC_SEQ):
        y = w0 * up[t] + w1 * up[t + 1] + w2 * up[t + 2]
        cb_ref[rows(t), :] = (bg_ref[rows(t), :] * y).astype(_bf16)
    nc_ref[0] = up[DEC_SEQ]
    nc_ref[1] = up[DEC_SEQ + 1]


def _conv_sample(zs, state_t, conv_w):
    nb = D_CONV
    return pl.pallas_call(
        _conv_sample_kernel,
        grid=(1,),
        in_specs=[
            pl.BlockSpec((N_SMALL, nb), lambda i: (0, OFF_XC // nb)),
            pl.BlockSpec((N_SMALL, nb), lambda i: (0, OFF_BG // nb)),
            pl.BlockSpec((N_SMALL, nb), lambda i: (0, OFF_CG // nb)),
            pl.BlockSpec((CONV_WIDTH - 1, DEC_BATCH, nb), lambda i: (0, 0, 0)),
            pl.BlockSpec((CONV_WIDTH, nb), lambda i: (0, 0)),
        ],
        out_specs=[
            pl.BlockSpec((N_SMALL, nb), lambda i: (0, 0)),
            pl.BlockSpec((CONV_WIDTH - 1, DEC_BATCH, nb), lambda i: (0, 0, 0)),
        ],
        out_shape=[
            jax.ShapeDtypeStruct((N_SMALL, nb), _bf16),
            jax.ShapeDtypeStruct((CONV_WIDTH - 1, DEC_BATCH, nb), _f32),
        ],
        compiler_params=_params(("arbitrary",)),
        name="conv_sample",
    )(zs, zs, zs, state_t, conv_w)


QPAIR = 2 * GROUP * HEAD_DIM
HEADS_PER_STEP = 2 * GROUP


def _attn_prompt_kernel(sink_ref, q_ref, k_ref, v_ref, km_ref, vm_ref,
                        o_ref, nk_ref, nv_ref, kbuf, vbuf, bias_ref):
    gp = pl.program_id(1)

    kbuf[0:KPAD, :] = jnp.zeros((KPAD, PAIR), _bf16)
    vbuf[0:KPAD, :] = jnp.zeros((KPAD, PAIR), _bf16)
    kbuf[KPAD:QBLK, :] = km_ref[...].astype(_bf16)
    vbuf[KPAD:QBLK, :] = vm_ref[...].astype(_bf16)
    kbuf[QBLK:KBUF_ROWS, :] = k_ref[...].astype(_bf16)
    vbuf[QBLK:KBUF_ROWS, :] = v_ref[...].astype(_bf16)
    nk_ref[0] = k_ref[SEQ - WINDOW:SEQ, :]
    nv_ref[0] = v_ref[SEQ - WINDOW:SEQ, :]

    row = lax.broadcasted_iota(jnp.int32, (QBLK, 2 * QBLK), 0)
    col = lax.broadcasted_iota(jnp.int32, (QBLK, 2 * QBLK), 1)
    dist = QBLK + row - col
    in_window = (dist >= 0) & (dist <= WINDOW)
    distf = dist.astype(_f32)
    for hh in range(HEADS_PER_STEP):
        head = (gp * HEADS_PER_STEP + hh + 1).astype(_f32)
        slope = jnp.exp2(jnp.full((QBLK, 2 * QBLK), -8.0 / N_HEADS, _f32) * head)
        bias_ref[hh] = jnp.where(in_window, -slope * distf, NEG_INF)

    lane = lax.broadcasted_iota(jnp.int32, (QBLK, PAIR), 1)
    low_half = lane < HEAD_DIM

    def block(i, carry):
        r0 = pl.multiple_of(i * QBLK, QBLK)
        qn = q_ref[pl.ds(r0 - QBLK, QBLK), :]
        kk = kbuf[pl.ds(r0 - QBLK, 2 * QBLK), :]
        vv = vbuf[pl.ds(r0 - QBLK, 2 * QBLK), :]
        pad_key = col < jnp.where(i == 1, KPAD, 0)
        for g in range(2):
            keep = low_half if g == 0 else jnp.logical_not(low_half)
            outs = []
            for hp in range(GROUP):
                hh = g * GROUP + hp
                tile = qn[:, (hh // 2) * PAIR:(hh // 2 + 1) * PAIR]
                if hp % 2 != g:
                    tile = pltpu.roll(tile, HEAD_DIM, axis=1)
                qp = jnp.where(keep, tile, 0.0).astype(_bf16)
                s = lax.dot_general(qp, kk, (((1,), (1,)), ((), ())),
                                    preferred_element_type=_f32)
                s = jnp.where(pad_key, NEG_INF, s + bias_ref[hh])
                sink = sink_ref[gp * HEADS_PER_STEP + hh]
                m = jnp.maximum(jnp.max(s, axis=-1, keepdims=True), sink)
                e = jnp.exp(s - m)
                denom = jnp.sum(e, axis=-1, keepdims=True) + jnp.exp(sink - m)
                o = jnp.dot(e.astype(_bf16), vv, preferred_element_type=_f32)
                outs.append(o / denom)
            for pr in range(GROUP // 2):
                even, odd = outs[2 * pr], outs[2 * pr + 1]
                if g == 0:
                    odd = pltpu.roll(odd, HEAD_DIM, axis=1)
                else:
                    even = pltpu.roll(even, HEAD_DIM, axis=1)
                t = g * (GROUP // 2) + pr
                o_ref[pl.ds(r0 - QBLK, QBLK), t * PAIR:(t + 1) * PAIR] = (
                    jnp.where(low_half, even, odd).astype(_bf16))
        return carry

    lax.fori_loop(1, N_QBLK + 1, block, 0)


def _attn_prompt(zm, zs, sinks):
    grid_spec = pltpu.PrefetchScalarGridSpec(
        num_scalar_prefetch=1,
        grid=(BATCH, N_KV_HEADS // 2),
        in_specs=[
            pl.BlockSpec((SEQ, QPAIR), lambda b, p, s: (b, OFF_Q // QPAIR + p)),
            pl.BlockSpec((SEQ, PAIR), lambda b, p, s: (b, OFF_K // PAIR + p)),
            pl.BlockSpec((SEQ, PAIR), lambda b, p, s: (b, OFF_V // PAIR + p)),
            pl.BlockSpec((N_META, PAIR), lambda b, p, s: (0, OFF_K // PAIR + p)),
            pl.BlockSpec((N_META, PAIR), lambda b, p, s: (0, OFF_V // PAIR + p)),
        ],
        out_specs=[
            pl.BlockSpec((SEQ, QPAIR), lambda b, p, s: (b, p)),
            pl.BlockSpec((1, WINDOW, PAIR), lambda b, p, s: (b, 0, p)),
            pl.BlockSpec((1, WINDOW, PAIR), lambda b, p, s: (b, 0, p)),
        ],
        scratch_shapes=[
            pltpu.VMEM((KBUF_ROWS, PAIR), _bf16),
            pltpu.VMEM((KBUF_ROWS, PAIR), _bf16),
            pltpu.VMEM((HEADS_PER_STEP, QBLK, 2 * QBLK), _f32),
        ],
    )
    return pl.pallas_call(
        _attn_prompt_kernel,
        grid_spec=grid_spec,
        out_shape=[
            jax.ShapeDtypeStruct((N_MAIN, Q_DIM), _bf16),
            jax.ShapeDtypeStruct((BATCH, WINDOW, KV_DIM), _f32),
            jax.ShapeDtypeStruct((BATCH, WINDOW, KV_DIM), _f32),
        ],
        compiler_params=_params(("arbitrary", "arbitrary")),
        name="attn_prompt",
    )(sinks, zm, zm, zm, zs, zs)


SEQ_CHUNK = 8
N_QROWS = N_HEADS * DEC_SEQ
N_KEYS = WINDOW + DEC_SEQ
N_KEYS_PAD = ((N_KEYS + 7) // 8) * 8


def _attn_sample_kernel(q_ref, kn_ref, vn_ref, ck_ref, cv_ref, sink_ref,
                        o_ref, nk_ref, nv_ref, kk_buf, vv_buf):
    q4 = q_ref[...]
    shp = (SEQ_CHUNK, N_QROWS, KV_DIM)
    row_kv = lax.broadcasted_iota(jnp.int32, shp, 1) // (GROUP * DEC_SEQ)
    lane_kv = lax.broadcasted_iota(jnp.int32, shp, 2) // HEAD_DIM
    own = row_kv == lane_kv
    qp = jnp.where(own, q4, 0.0)

    kk_buf[:, 0:WINDOW, :] = ck_ref[...]
    kk_buf[:, WINDOW:N_KEYS, :] = kn_ref[...]
    kk_buf[:, N_KEYS:N_KEYS_PAD, :] = jnp.zeros((SEQ_CHUNK, N_KEYS_PAD - N_KEYS, KV_DIM), _f32)
    vv_buf[:, 0:WINDOW, :] = cv_ref[...]
    vv_buf[:, WINDOW:N_KEYS, :] = vn_ref[...]
    vv_buf[:, N_KEYS:N_KEYS_PAD, :] = jnp.zeros((SEQ_CHUNK, N_KEYS_PAD - N_KEYS, KV_DIM), _f32)
    nk_ref[...] = kk_buf[:, DEC_SEQ:N_KEYS, :]
    nv_ref[...] = vv_buf[:, DEC_SEQ:N_KEYS, :]

    kk = kk_buf[...]
    vv = vv_buf[...]
    s = jnp.einsum('bqd,bkd->bqk', qp, kk, preferred_element_type=_f32)

    row = lax.broadcasted_iota(jnp.int32, (N_QROWS, N_KEYS_PAD), 0)
    col = lax.broadcasted_iota(jnp.int32, (N_QROWS, N_KEYS_PAD), 1)
    step = row % DEC_SEQ
    head = (row // DEC_SEQ + 1).astype(_f32)
    dist = step + WINDOW - col
    valid = (dist >= 0) & (dist <= WINDOW) & (col < N_KEYS)
    slope = jnp.exp2(head * (-8.0 / N_HEADS))
    bias = jnp.where(valid, -slope * dist.astype(_f32), NEG_INF)
    s = jnp.where(valid[None], s + bias[None], NEG_INF)
    sink = sink_ref[...][None]
    m = jnp.maximum(jnp.max(s, axis=-1, keepdims=True), sink)
    e = jnp.exp(s - m)
    denom = jnp.sum(e, axis=-1, keepdims=True) + jnp.exp(sink - m)
    o = jnp.einsum('bqk,bkd->bqd', e, vv, preferred_element_type=_f32) / denom
    o = jnp.where(own, o, 0.0)
    acc = o
    for g in range(1, N_KV_HEADS):
        acc = acc + pltpu.roll(o, g * HEAD_DIM, axis=2)
    o_ref[...] = acc[:, :, 0:HEAD_DIM]


def _attn_sample(q_s, k_new, v_new, cache_k, cache_v, sink_rows):
    c = SEQ_CHUNK
    return pl.pallas_call(
        _attn_sample_kernel,
        grid=(DEC_BATCH // c,),
        in_specs=[
            pl.BlockSpec((c, N_QROWS, KV_DIM), lambda i: (i, 0, 0)),
            pl.BlockSpec((c, DEC_SEQ, KV_DIM), lambda i: (i, 0, 0)),
            pl.BlockSpec((c, DEC_SEQ, KV_DIM), lambda i: (i, 0, 0)),
            pl.BlockSpec((c, WINDOW, KV_DIM), lambda i: (i, 0, 0)),
            pl.BlockSpec((c, WINDOW, KV_DIM), lambda i: (i, 0, 0)),
            pl.BlockSpec((N_QROWS, 1), lambda i: (0, 0)),
        ],
        out_specs=[
            pl.BlockSpec((c, N_QROWS, HEAD_DIM), lambda i: (i, 0, 0)),
            pl.BlockSpec((c, WINDOW, KV_DIM), lambda i: (i, 0, 0)),
            pl.BlockSpec((c, WINDOW, KV_DIM), lambda i: (i, 0, 0)),
        ],
        out_shape=[
            jax.ShapeDtypeStruct((DEC_BATCH, N_QROWS, HEAD_DIM), _f32),
            jax.ShapeDtypeStruct((DEC_BATCH, WINDOW, KV_DIM), _f32),
            jax.ShapeDtypeStruct((DEC_BATCH, WINDOW, KV_DIM), _f32),
        ],
        scratch_shapes=[
            pltpu.VMEM((c, N_KEYS_PAD, KV_DIM), _f32),
            pltpu.VMEM((c, N_KEYS_PAD, KV_DIM), _f32),
        ],
        compiler_params=_params(("arbitrary",)),
        name="attn_sample",
    )(q_s, k_new, v_new, cache_k, cache_v, sink_rows)


def _mix_kernel(x1m_ref, x1s_ref, cbm_ref, cbs_ref, atm_ref, ats_ref, gcm_ref, gcs_ref,
                gam_ref, gas_ref, wc_ref, wa_ref, wo_ref, om_ref, os_ref,
                cb_all, at_all, t_all):
    n = pl.program_id(1)

    @pl.when(n == 0)
    def _():
        om_ref[...] = x1m_ref[...]
        os_ref[...] = x1s_ref[...]
        cb_all[0:TM, :] = cbm_ref[...]
        cb_all[TM:, :] = cbs_ref[...]
        at_all[0:TM, :] = atm_ref[...]
        at_all[TM:, :] = ats_ref[...]

    yc = jnp.dot(cb_all[...], wc_ref[...], preferred_element_type=_f32)
    ya = jnp.dot(at_all[...], wa_ref[...], preferred_element_type=_f32)
    t_all[0:TM, :] = (jax.nn.sigmoid(gcm_ref[...]) * yc[0:TM, :]
                      + jax.nn.sigmoid(gam_ref[...]) * ya[0:TM, :]).astype(_bf16)
    t_all[TM:, :] = (jax.nn.sigmoid(gcs_ref[...]) * yc[TM:, :]
                     + jax.nn.sigmoid(gas_ref[...]) * ya[TM:, :]).astype(_bf16)
    r = jnp.dot(t_all[...], wo_ref[...], preferred_element_type=_f32)
    om_ref[...] += r[0:TM, :]
    os_ref[...] += r[TM:, :]


def _mix(x1m, x1s, cbm, cbs, atm, ats, zm, zs, wc, wa, wo):
    gc0 = OFF_GC // TN_OUT
    ga0 = OFF_GA // TN_OUT
    row = lambda i, n: (i, 0)
    return pl.pallas_call(
        _mix_kernel,
        grid=(N_TILES, N_OUT_CHUNKS),
        in_specs=[
            pl.BlockSpec((TM, D_MODEL), row),
            pl.BlockSpec((TS, D_MODEL), row),
            pl.BlockSpec((TM, D_CONV), row),
            pl.BlockSpec((TS, D_CONV), row),
            pl.BlockSpec((TM, Q_DIM), row),
            pl.BlockSpec((TS, Q_DIM), row),
            pl.BlockSpec((TM, TN_OUT), lambda i, n: (i, gc0 + n)),
            pl.BlockSpec((TS, TN_OUT), lambda i, n: (i, gc0 + n)),
            pl.BlockSpec((TM, TN_OUT), lambda i, n: (i, ga0 + n)),
            pl.BlockSpec((TS, TN_OUT), lambda i, n: (i, ga0 + n)),
            pl.BlockSpec((D_CONV, TN_OUT), lambda i, n: (0, n)),
            pl.BlockSpec((Q_DIM, TN_OUT), lambda i, n: (0, n)),
            pl.BlockSpec((TN_OUT, D_MODEL), lambda i, n: (n, 0)),
        ],
        out_specs=[
            pl.BlockSpec((TM, D_MODEL), row),
            pl.BlockSpec((TS, D_MODEL), row),
        ],
        out_shape=[
            jax.ShapeDtypeStruct((N_MAIN, D_MODEL), _f32),
            jax.ShapeDtypeStruct((N_SMALL, D_MODEL), _f32),
        ],
        scratch_shapes=[
            pltpu.VMEM((TM + TS, D_CONV), _bf16),
            pltpu.VMEM((TM + TS, Q_DIM), _bf16),
            pltpu.VMEM((TM + TS, TN_OUT), _bf16),
        ],
        compiler_params=_params(("parallel", "arbitrary")),
        name="mix",
    )(x1m, x1s, cbm, cbs, atm, ats, zm, zs, zm, zs, wc, wa, wo)


def _sample_rows(a):
    return a.transpose(1, 0, 2).reshape(N_SAMPLE, a.shape[-1])


def _to_small(sample_rows, dtype):
    w = sample_rows.shape[-1]
    return jnp.concatenate([
        jnp.zeros((SAMPLE_ROW0, w), dtype), sample_rows.astype(dtype),
        jnp.zeros((N_SMALL - SAMPLE_ROW0 - N_SAMPLE, w), dtype)], axis=0)


def kernel(x_prompt, x_sample, state_conv, cache_k_win, cache_v_win, meta_tokens, ffn1_norm, ffn1_w_up, ffn1_w_down, mix_norm, w_in, q_norm, k_norm, conv_w, w_conv_out, attn_sinks, w_attn_out, w_o, ffn2_norm, ffn2_w_up, ffn2_w_down):
    l = 0
    xm = x_prompt.reshape(N_MAIN, D_MODEL)
    xs = jnp.concatenate([
        meta_tokens.astype(_f32), _sample_rows(x_sample),
        jnp.zeros((N_SMALL - N_META - N_SAMPLE, D_MODEL), _f32)], axis=0)

    wgu1 = _cast(_w_up_job(256), ffn1_w_up[l])
    wd1 = _cast(_w_down_job(128), ffn1_w_down[l])
    jobs = (_w_up_job(16), _w_down_job(32), _w_in_job(16),
            _cast_job(D_CONV, D_MODEL, 16), _cast_job(Q_DIM, D_MODEL, 16),
            _cast_job(D_MODEL, D_MODEL, 16))
    x1m, x1s, wgu2, wd2, w_in_b, wc_b, wa_b, wo_b = _ffn(
        xm, xs, ffn1_norm[l].reshape(1, D_MODEL), wgu1, wd1, jobs,
        (ffn2_w_up[l], ffn2_w_down[l], w_in[l], w_conv_out[l], w_attn_out[l], w_o[l]))
    zm, zs = _inproj(x1m, x1s, mix_norm[l].reshape(1, D_MODEL), w_in_b, q_norm[l], k_norm[l])

    cbm, new_conv_p = _conv_prompt(zm, zs, conv_w[l])
    cbs, new_conv_s = _conv_sample(zs, state_conv[l].transpose(1, 0, 2), conv_w[l])

    atm, new_k_p, new_v_p = _attn_prompt(zm, zs, attn_sinks[l])
    zsamp = zs[SAMPLE_ROW0:SAMPLE_ROW0 + N_SAMPLE]
    q_s = (zsamp[:, OFF_Q:OFF_Q + Q_DIM].reshape(DEC_SEQ, DEC_BATCH, N_HEADS, HEAD_DIM)
           .transpose(1, 2, 0, 3).reshape(DEC_BATCH, N_QROWS, HEAD_DIM))
    q_s = jnp.tile(q_s, (1, 1, N_KV_HEADS))
    k_new = zsamp[:, OFF_K:OFF_K + KV_DIM].reshape(DEC_SEQ, DEC_BATCH, KV_DIM).transpose(1, 0, 2)
    v_new = zsamp[:, OFF_V:OFF_V + KV_DIM].reshape(DEC_SEQ, DEC_BATCH, KV_DIM).transpose(1, 0, 2)
    sink_rows = jnp.repeat(attn_sinks[l].astype(_f32), DEC_SEQ).reshape(N_QROWS, 1)
    o_s, new_k_s, new_v_s = _attn_sample(
        q_s, k_new, v_new,
        cache_k_win[l].reshape(DEC_BATCH, WINDOW, KV_DIM),
        cache_v_win[l].reshape(DEC_BATCH, WINDOW, KV_DIM), sink_rows)
    at_rows = (o_s.reshape(DEC_BATCH, N_HEADS, DEC_SEQ, HEAD_DIM)
               .transpose(2, 0, 1, 3).reshape(N_SAMPLE, Q_DIM))
    ats = _to_small(at_rows, _bf16)

    x2m, x2s = _mix(x1m, x1s, cbm, cbs, atm, ats, zm, zs, wc_b, wa_b, wo_b)
    ym, ys = _ffn(x2m, x2s, ffn2_norm[l].reshape(1, D_MODEL), wgu2, wd2)

    y_prompt = ym.reshape(BATCH, SEQ, D_MODEL)
    y_sample = (ys[SAMPLE_ROW0:SAMPLE_ROW0 + N_SAMPLE]
                .reshape(DEC_SEQ, DEC_BATCH, D_MODEL).transpose(1, 0, 2))
    kv_shape_p = (1, BATCH, WINDOW, N_KV_HEADS, HEAD_DIM)
    kv_shape_s = (1, DEC_BATCH, WINDOW, N_KV_HEADS, HEAD_DIM)
    return (y_prompt, y_sample,
            new_conv_p[None],
            new_k_p.reshape(kv_shape_p), new_v_p.reshape(kv_shape_p),
            new_conv_s.transpose(1, 0, 2)[None],
            new_k_s.reshape(kv_shape_s), new_v_s.reshape(kv_shape_s))
```

```python
import functools
from typing import NamedTuple

import jax
import jax.numpy as jnp
from jax import lax
from jax.experimental import pallas as pl
from jax.experimental.pallas import tpu as pltpu

D_MODEL = 2048
BATCH = 4
SEQ = 2048
DEC_BATCH = 32
DEC_SEQ = 4
PAST_LEN = 16384
N_META = 16
D_CONV = D_MODEL // 2
CONV_WIDTH = 3
HEAD_DIM = 64
N_HEADS = (D_MODEL // 2) // HEAD_DIM
N_KV_HEADS = N_HEADS // 4
GROUP = N_HEADS // N_KV_HEADS
Q_DIM = N_HEADS * HEAD_DIM
KV_DIM = N_KV_HEADS * HEAD_DIM
PAIR = 2 * HEAD_DIM
WINDOW = 128
D_FF = ((8 * D_MODEL // 3 + 127) // 128) * 128
D_IN = 3 * D_CONV + Q_DIM + 2 * KV_DIM + 2 * D_MODEL
EPS = 1e-6
NEG_INF = -1e30

SRC_K = 3 * D_CONV + Q_DIM
SRC_GC = SRC_K + 2 * KV_DIM
OFF_XC = 0
OFF_BG = D_CONV
OFF_CG = 2 * D_CONV
OFF_Q = 3 * D_CONV
OFF_GC = OFF_Q + Q_DIM
OFF_GA = OFF_GC + D_MODEL
OFF_K = OFF_GA + D_MODEL
OFF_V = OFF_K + KV_DIM

N_MAIN = BATCH * SEQ
N_SAMPLE = DEC_BATCH * DEC_SEQ
N_SMALL = 256
SAMPLE_ROW0 = N_META

N_TILES = 16
TM = N_MAIN // N_TILES
TS = N_SMALL // N_TILES

TF = 512
D_FF_PAD = ((D_FF + TF - 1) // TF) * TF
N_FF_CHUNKS = D_FF_PAD // TF
FFN_STEPS = N_TILES * N_FF_CHUNKS
TN_IN = 2176
N_IN_CHUNKS = D_IN // TN_IN
Q_CHUNK = OFF_Q // TN_IN
K_CHUNK = OFF_K // TN_IN
assert (OFF_Q + Q_DIM - 1) // TN_IN == Q_CHUNK and (OFF_K + KV_DIM - 1) // TN_IN == K_CHUNK
TN_OUT = 1024
N_OUT_CHUNKS = D_MODEL // TN_OUT
GATE_PIECE = 512

QBLK = 128
KPAD = QBLK - N_META
N_QBLK = SEQ // QBLK
KBUF_ROWS = QBLK + SEQ

VMEM_LIMIT = 56 * 1024 * 1024

_bf16 = jnp.bfloat16
_f32 = jnp.float32


def _params(sem):
    return pltpu.CompilerParams(dimension_semantics=sem, vmem_limit_bytes=VMEM_LIMIT)


def _rms_rows(x, g):
    ms = jnp.mean(x * x, axis=-1, keepdims=True)
    return x * lax.rsqrt(ms + EPS) * g


def _pair_rms(x, g_pair):
    low = lax.broadcasted_iota(jnp.int32, x.shape, 1) < HEAD_DIM
    x2 = x * x
    s_lo = jnp.sum(jnp.where(low, x2, 0.0), axis=-1, keepdims=True)
    s_hi = jnp.sum(jnp.where(low, 0.0, x2), axis=-1, keepdims=True)
    ms = jnp.where(low, s_lo, s_hi) * (1.0 / HEAD_DIM)
    return x * lax.rsqrt(ms + EPS) * g_pair


class _CastJob(NamedTuple):
    slab: int
    rows_in: int
    n_in_slabs: int
    n_out_slabs: int
    n_in_cols: int
    n_out_cols: int
    segments: tuple
    zero_ranges: tuple


def _cast_job(rows_in, cols_in, slab, segments=None, zero_ranges=(), rows_out=None, cols_out=None):
    rows_out = rows_in if rows_out is None else rows_out
    cols_out = cols_in if cols_out is None else cols_out
    segments = ((0, cols_in, 0),) if segments is None else segments
    assert rows_out % slab == 0
    return _CastJob(slab, rows_in, -(-rows_in // slab), rows_out // slab, cols_in, cols_out,
                    tuple(segments), tuple(zero_ranges))


def _job_specs(job, step_of):
    in_idx = lambda *ids: (jnp.minimum(step_of(*ids), job.n_in_slabs - 1), 0)
    out_idx = lambda *ids: (jnp.minimum(step_of(*ids), job.n_out_slabs - 1), 0)
    return (pl.BlockSpec((job.slab, job.n_in_cols), in_idx),
            pl.BlockSpec((job.slab, job.n_out_cols), out_idx))


def _run_cast_job(job, step, x_ref, o_ref):
    out_slab = jnp.minimum(step, job.n_out_slabs - 1)
    padded = job.n_out_slabs * job.slab > job.rows_in
    for src, width, dst in job.segments:
        x = x_ref[:, src:src + width]
        if padded:
            row = out_slab * job.slab + lax.broadcasted_iota(jnp.int32, x.shape, 0)
            x = jnp.where(row < job.rows_in, x, 0.0)
        o_ref[:, dst:dst + width] = x.astype(_bf16)
    for lo, hi in job.zero_ranges:
        o_ref[:, lo:hi] = jnp.zeros((job.slab, hi - lo), _bf16)


def _cast_kernel(job, x_ref, o_ref):
    _run_cast_job(job, pl.program_id(0), x_ref, o_ref)


def _cast(job, w):
    in_spec, out_spec = _job_specs(job, lambda i: i)
    return pl.pallas_call(
        functools.partial(_cast_kernel, job),
        grid=(job.n_out_slabs,),
        in_specs=[in_spec],
        out_specs=out_spec,
        out_shape=jax.ShapeDtypeStruct((job.n_out_slabs * job.slab, job.n_out_cols), _bf16),
        compiler_params=_params(("arbitrary",)),
        name="cast",
    )(w)


def _w_up_job(slab):
    return _cast_job(D_MODEL, 2 * D_FF, slab,
                     segments=((0, D_FF, 0), (D_FF, D_FF, D_FF_PAD)),
                     zero_ranges=((D_FF, D_FF_PAD), (D_FF_PAD + D_FF, 2 * D_FF_PAD)),
                     cols_out=2 * D_FF_PAD)


def _w_down_job(slab):
    return _cast_job(D_FF, D_MODEL, slab, rows_out=D_FF_PAD)


def _w_in_job(slab):
    return _cast_job(D_MODEL, D_IN, slab,
                     segments=((0, SRC_K, 0), (SRC_GC, 2 * D_MODEL, OFF_GC),
                               (SRC_K, 2 * KV_DIM, OFF_K)))


def _ffn_kernel(jobs, with_proj, *refs):
    nj = len(jobs)
    refs = list(refs)
    xm_ref, xs_ref = refs[:2]
    del refs[:2]
    if with_proj:
        tm_ref, ts_ref, wo_ref = refs[:3]
        del refs[:3]
    g_ref, wg_ref, wu_ref, wd_ref = refs[:4]
    job_in = refs[4:4 + nj]
    om_ref, os_ref = refs[4 + nj:6 + nj]
    job_out = refs[6 + nj:6 + 2 * nj]
    scratch = refs[6 + 2 * nj:]
    h_ref = scratch[0]
    i = pl.program_id(0)
    j = pl.program_id(1)

    @pl.when(j == 0)
    def _():
        g = g_ref[...]
        xm = xm_ref[...]
        xs = xs_ref[...]
        if with_proj:
            t_all = scratch[1]
            t_all[0:TM, :] = tm_ref[...]
            t_all[TM:, :] = ts_ref[...]
            p = jnp.dot(t_all[...], wo_ref[...], preferred_element_type=_f32)
            xm = xm + p[0:TM, :]
            xs = xs + p[TM:, :]
        h_ref[0:TM, :] = _rms_rows(xm, g).astype(_bf16)
        om_ref[...] = xm
        h_ref[TM:, :] = _rms_rows(xs, g).astype(_bf16)
        os_ref[...] = xs

    for job, x_ref, o_ref in zip(jobs, job_in, job_out):
        _run_cast_job(job, i * N_FF_CHUNKS + j, x_ref, o_ref)

    h = h_ref[...]
    gate = jnp.dot(h, wg_ref[...], preferred_element_type=_f32)
    up = jnp.dot(h, wu_ref[...], preferred_element_type=_f32)
    a = (gate * jax.nn.sigmoid(gate) * up * 0.5).astype(_bf16)
    r = jnp.dot(a, wd_ref[...], preferred_element_type=_f32)
    om_ref[...] += r[0:TM, :]
    os_ref[...] += r[TM:, :]


def _ffn(xm, xs, g, wgu, wd, proj=None, jobs=(), job_inputs=()):
    assert all(job.n_out_slabs <= FFN_STEPS for job in jobs)
    job_specs = [_job_specs(job, lambda i, j: i * N_FF_CHUNKS + j) for job in jobs]
    proj_specs, proj_scratch = [], []
    if proj is not None:
        proj_specs = [
            pl.BlockSpec((TM, D_MODEL), lambda i, j: (i, 0)),
            pl.BlockSpec((TS, D_MODEL), lambda i, j: (i, 0)),
            pl.BlockSpec((D_MODEL, D_MODEL), lambda i, j: (0, 0), pipeline_mode=pl.Buffered(1)),
        ]
        proj_scratch = [pltpu.VMEM((TM + TS, D_MODEL), _bf16)]
    return pl.pallas_call(
        functools.partial(_ffn_kernel, tuple(jobs), proj is not None),
        grid=(N_TILES, N_FF_CHUNKS),
        in_specs=[
            pl.BlockSpec((TM, D_MODEL), lambda i, j: (i, 0)),
            pl.BlockSpec((TS, D_MODEL), lambda i, j: (i, 0)),
        ] + proj_specs + [
            pl.BlockSpec((1, D_MODEL), lambda i, j: (0, 0)),
            pl.BlockSpec((D_MODEL, TF), lambda i, j: (0, j)),
            pl.BlockSpec((D_MODEL, TF), lambda i, j: (0, N_FF_CHUNKS + j)),
            pl.BlockSpec((TF, D_MODEL), lambda i, j: (j, 0)),
        ] + [s[0] for s in job_specs],
        out_specs=[
            pl.BlockSpec((TM, D_MODEL), lambda i, j: (i, 0)),
            pl.BlockSpec((TS, D_MODEL), lambda i, j: (i, 0)),
        ] + [s[1] for s in job_specs],
        out_shape=[
            jax.ShapeDtypeStruct((N_MAIN, D_MODEL), _f32),
            jax.ShapeDtypeStruct((N_SMALL, D_MODEL), _f32),
        ] + [jax.ShapeDtypeStruct((job.n_out_slabs * job.slab, job.n_out_cols), _bf16)
             for job in jobs],
        scratch_shapes=[pltpu.VMEM((TM + TS, D_MODEL), _bf16)] + proj_scratch,
        compiler_params=_params(("arbitrary", "arbitrary")),
        name="ffn",
    )(xm, xs, *(proj or ()), g, wgu, wgu, wd, *job_inputs)


def _inproj_kernel(xm_ref, xs_ref, g_ref, w_ref, qg_ref, kg_ref, zm_ref, zs_ref, h_ref):
    j = pl.program_id(1)

    @pl.when(j == 0)
    def _():
        g = g_ref[...]
        h_ref[0:TM, :] = _rms_rows(xm_ref[...], g).astype(_bf16)
        h_ref[TM:, :] = _rms_rows(xs_ref[...], g).astype(_bf16)

    z = jnp.dot(h_ref[...], w_ref[...], preferred_element_type=_f32)
    zm_ref[...] = z[0:TM, :]
    zs_ref[...] = z[TM:, :]

    def norm_heads(chunk, col0, width, gain_ref, scale):
        @pl.when(j == chunk)
        def _():
            g_pair = gain_ref[...]
            for z_ref in (zm_ref, zs_ref):
                for t in range(width // PAIR):
                    cols = slice(col0 + t * PAIR, col0 + (t + 1) * PAIR)
                    z_ref[:, cols] = _pair_rms(z_ref[:, cols], g_pair) * scale

    norm_heads(Q_CHUNK, OFF_Q - Q_CHUNK * TN_IN, Q_DIM, qg_ref, HEAD_DIM ** -0.5)
    norm_heads(K_CHUNK, OFF_K - K_CHUNK * TN_IN, KV_DIM, kg_ref, 1.0)


def _inproj(xm, xs, g, w, q_gain, k_gain):
    qg = jnp.tile(q_gain.reshape(1, HEAD_DIM), (1, 2))
    kg = jnp.tile(k_gain.reshape(1, HEAD_DIM), (1, 2))
    return pl.pallas_call(
        _inproj_kernel,
        grid=(N_TILES, N_IN_CHUNKS),
        in_specs=[
            pl.BlockSpec((TM, D_MODEL), lambda i, j: (i, 0)),
            pl.BlockSpec((TS, D_MODEL), lambda i, j: (i, 0)),
            pl.BlockSpec((1, D_MODEL), lambda i, j: (0, 0)),
            pl.BlockSpec((D_MODEL, TN_IN), lambda i, j: (0, j)),
            pl.BlockSpec((1, PAIR), lambda i, j: (0, 0)),
            pl.BlockSpec((1, PAIR), lambda i, j: (0, 0)),
        ],
        out_specs=[
            pl.BlockSpec((TM, TN_IN), lambda i, j: (i, j)),
            pl.BlockSpec((TS, TN_IN), lambda i, j: (i, j)),
        ],
        out_shape=[
            jax.ShapeDtypeStruct((N_MAIN, D_IN), _f32),
            jax.ShapeDtypeStruct((N_SMALL, D_IN), _f32),
        ],
        scratch_shapes=[pltpu.VMEM((TM + TS, D_MODEL), _bf16)],
        compiler_params=_params(("parallel", "arbitrary")),
        name="inproj",
    )(xm, xs, g, w, qg, kg)


CONV_TILE = 512
CONV_TILES_PER_SEQ = SEQ // CONV_TILE
CONV_HALO = 8


def _conv_prompt_kernel(xc_ref, bg_ref, cg_ref, xcm_ref, cgm_ref, w_ref, cb_ref, nc_ref, ubuf):
    r = pl.program_id(1)

    @pl.when(r == 0)
    def _():
        um = cgm_ref[...] * xcm_ref[...]
        ubuf[0:CONV_HALO, :] = um[N_META - CONV_HALO:N_META, :]

    u = cg_ref[...] * xc_ref[...]
    ubuf[CONV_HALO:CONV_HALO + CONV_TILE, :] = u
    u1 = ubuf[CONV_HALO - 1:CONV_HALO - 1 + CONV_TILE, :]
    u2 = ubuf[CONV_HALO - 2:CONV_HALO - 2 + CONV_TILE, :]
    w = w_ref[...]
    y = w[0:1, :] * u2 + w[1:2, :] * u1 + w[2:3, :] * u
    cb_ref[...] = (bg_ref[...] * y).astype(_bf16)
    tail = ubuf[CONV_TILE:CONV_TILE + CONV_HALO, :]
    ubuf[0:CONV_HALO, :] = tail

    @pl.when(r == CONV_TILES_PER_SEQ - 1)
    def _():
        nc_ref[0] = tail[CONV_HALO - (CONV_WIDTH - 1):CONV_HALO, :]


def _conv_prompt(zm, zs, conv_w):
    nb = D_CONV
    return pl.pallas_call(
        _conv_prompt_kernel,
        grid=(BATCH, CONV_TILES_PER_SEQ),
        in_specs=[
            pl.BlockSpec((CONV_TILE, nb), lambda b, r: (b * CONV_TILES_PER_SEQ + r, OFF_XC // nb)),
            pl.BlockSpec((CONV_TILE, nb), lambda b, r: (b * CONV_TILES_PER_SEQ + r, OFF_BG // nb)),
            pl.BlockSpec((CONV_TILE, nb), lambda b, r: (b * CONV_TILES_PER_SEQ + r, OFF_CG // nb)),
            pl.BlockSpec((N_META, nb), lambda b, r: (0, OFF_XC // nb)),
            pl.BlockSpec((N_META, nb), lambda b, r: (0, OFF_CG // nb)),
            pl.BlockSpec((CONV_WIDTH, nb), lambda b, r: (0, 0)),
        ],
        out_specs=[
            pl.BlockSpec((CONV_TILE, nb), lambda b, r: (b * CONV_TILES_PER_SEQ + r, 0)),
            pl.BlockSpec((1, CONV_WIDTH - 1, nb), lambda b, r: (b, 0, 0)),
        ],
        out_shape=[
            jax.ShapeDtypeStruct((N_MAIN, nb), _bf16),
            jax.ShapeDtypeStruct((BATCH, CONV_WIDTH - 1, nb), _f32),
        ],
        scratch_shapes=[pltpu.VMEM((CONV_HALO + CONV_TILE, nb), _f32)],
        compiler_params=_params(("arbitrary", "arbitrary")),
        name="conv_prompt",
    )(zm, zm, zm, zs, zs, conv_w)


def _conv_sample_kernel(xc_ref, bg_ref, cg_ref, st_ref, w_ref, cb_ref, nc_ref):
    w = w_ref[...]
    w0, w1, w2 = w[0:1, :], w[1:2, :], w[2:3, :]
    rows = lambda t: slice(SAMPLE_ROW0 + t * DEC_BATCH, SAMPLE_ROW0 + (t + 1) * DEC_BATCH)
    up = [st_ref[0], st_ref[1]]
    for t in range(DEC_SEQ):
        up.append(cg_ref[rows(t), :] * xc_ref[rows(t), :])
    cb_ref[...] = jnp.zeros(cb_ref.shape, cb_ref.dtype)
    for t in range(DEC_SEQ):
        y = w0 * up[t] + w1 * up[t + 1] + w2 * up[t + 2]
        cb_ref[rows(t), :] = (bg_ref[rows(t), :] * y).astype(_bf16)
    nc_ref[0] = up[DEC_SEQ]
    nc_ref[1] = up[DEC_SEQ + 1]


def _conv_sample(zs, state_t, conv_w):
    nb = D_CONV
    return pl.pallas_call(
        _conv_sample_kernel,
        grid=(1,),
        in_specs=[
            pl.BlockSpec((N_SMALL, nb), lambda i: (0, OFF_XC // nb)),
            pl.BlockSpec((N_SMALL, nb), lambda i: (0, OFF_BG // nb)),
            pl.BlockSpec((N_SMALL, nb), lambda i: (0, OFF_CG // nb)),
            pl.BlockSpec((CONV_WIDTH - 1, DEC_BATCH, nb), lambda i: (0, 0, 0)),
            pl.BlockSpec((CONV_WIDTH, nb), lambda i: (0, 0)),
        ],
        out_specs=[
            pl.BlockSpec((N_SMALL, nb), lambda i: (0, 0)),
            pl.BlockSpec((CONV_WIDTH - 1, DEC_BATCH, nb), lambda i: (0, 0, 0)),
        ],
        out_shape=[
            jax.ShapeDtypeStruct((N_SMALL, nb), _bf16),
            jax.ShapeDtypeStruct((CONV_WIDTH - 1, DEC_BATCH, nb), _f32),
        ],
        compiler_params=_params(("arbitrary",)),
        name="conv_sample",
    )(zs, zs, zs, state_t, conv_w)


QPAIR = 2 * GROUP * HEAD_DIM
HEADS_PER_STEP = 2 * GROUP


def _attn_prompt_kernel(sink_ref, q_ref, k_ref, v_ref, km_ref, vm_ref,
                        o_ref, nk_ref, nv_ref, kbuf, vbuf, bias_ref):
    gp = pl.program_id(1)

    kbuf[0:KPAD, :] = jnp.zeros((KPAD, PAIR), _bf16)
    vbuf[0:KPAD, :] = jnp.zeros((KPAD, PAIR), _bf16)
    kbuf[KPAD:QBLK, :] = km_ref[...].astype(_bf16)
    vbuf[KPAD:QBLK, :] = vm_ref[...].astype(_bf16)
    kbuf[QBLK:KBUF_ROWS, :] = k_ref[...].astype(_bf16)
    vbuf[QBLK:KBUF_ROWS, :] = v_ref[...].astype(_bf16)
    nk_ref[0] = k_ref[SEQ - WINDOW:SEQ, :]
    nv_ref[0] = v_ref[SEQ - WINDOW:SEQ, :]

    row = lax.broadcasted_iota(jnp.int32, (QBLK, 2 * QBLK), 0)
    col = lax.broadcasted_iota(jnp.int32, (QBLK, 2 * QBLK), 1)
    dist = QBLK + row - col
    in_window = (dist >= 0) & (dist <= WINDOW)
    distf = dist.astype(_f32)
    for hh in range(HEADS_PER_STEP):
        head = (gp * HEADS_PER_STEP + hh + 1).astype(_f32)
        slope = jnp.exp2(jnp.full((QBLK, 2 * QBLK), -8.0 / N_HEADS, _f32) * head)
        bias_ref[hh] = jnp.where(in_window, -slope * distf, NEG_INF)

    lane = lax.broadcasted_iota(jnp.int32, (QBLK, PAIR), 1)
    low_half = lane < HEAD_DIM

    def block(i, carry):
        r0 = pl.multiple_of(i * QBLK, QBLK)
        qn = q_ref[pl.ds(r0 - QBLK, QBLK), :]
        kk = kbuf[pl.ds(r0 - QBLK, 2 * QBLK), :]
        vv = vbuf[pl.ds(r0 - QBLK, 2 * QBLK), :]
        pad_key = col < jnp.where(i == 1, KPAD, 0)
        for g in range(2):
            keep = low_half if g == 0 else jnp.logical_not(low_half)
            outs = []
            for hp in range(GROUP):
                hh = g * GROUP + hp
                tile = qn[:, (hh // 2) * PAIR:(hh // 2 + 1) * PAIR]
                if hp % 2 != g:
                    tile = pltpu.roll(tile, HEAD_DIM, axis=1)
                qp = jnp.where(keep, tile, 0.0).astype(_bf16)
                s = lax.dot_general(qp, kk, (((1,), (1,)), ((), ())),
                                    preferred_element_type=_f32)
                s = jnp.where(pad_key, NEG_INF, s + bias_ref[hh])
                sink = sink_ref[gp * HEADS_PER_STEP + hh]
                m = jnp.maximum(jnp.max(s, axis=-1, keepdims=True), sink)
                e = jnp.exp(s - m)
                denom = jnp.sum(e, axis=-1, keepdims=True) + jnp.exp(sink - m)
                o = jnp.dot(e.astype(_bf16), vv, preferred_element_type=_f32)
                outs.append(o / denom)
            for pr in range(GROUP // 2):
                even, odd = outs[2 * pr], outs[2 * pr + 1]
                if g == 0:
                    odd = pltpu.roll(odd, HEAD_DIM, axis=1)
                else:
                    even = pltpu.roll(even, HEAD_DIM, axis=1)
                t = g * (GROUP // 2) + pr
                o_ref[pl.ds(r0 - QBLK, QBLK), t * PAIR:(t + 1) * PAIR] = (
                    jnp.where(low_half, even, odd).astype(_bf16))
        return carry

    lax.fori_loop(1, N_QBLK + 1, block, 0)


def _attn_prompt(zm, zs, sinks):
    grid_spec = pltpu.PrefetchScalarGridSpec(
        num_scalar_prefetch=1,
        grid=(BATCH, N_KV_HEADS // 2),
        in_specs=[
            pl.BlockSpec((SEQ, QPAIR), lambda b, p, s: (b, OFF_Q // QPAIR + p)),
            pl.BlockSpec((SEQ, PAIR), lambda b, p, s: (b, OFF_K // PAIR + p)),
            pl.BlockSpec((SEQ, PAIR), lambda b, p, s: (b, OFF_V // PAIR + p)),
            pl.BlockSpec((N_META, PAIR), lambda b, p, s: (0, OFF_K // PAIR + p)),
            pl.BlockSpec((N_META, PAIR), lambda b, p, s: (0, OFF_V // PAIR + p)),
        ],
        out_specs=[
            pl.BlockSpec((SEQ, QPAIR), lambda b, p, s: (b, p)),
            pl.BlockSpec((1, WINDOW, PAIR), lambda b, p, s: (b, 0, p)),
            pl.BlockSpec((1, WINDOW, PAIR), lambda b, p, s: (b, 0, p)),
        ],
        scratch_shapes=[
            pltpu.VMEM((KBUF_ROWS, PAIR), _bf16),
            pltpu.VMEM((KBUF_ROWS, PAIR), _bf16),
            pltpu.VMEM((HEADS_PER_STEP, QBLK, 2 * QBLK), _f32),
        ],
    )
    return pl.pallas_call(
        _attn_prompt_kernel,
        grid_spec=grid_spec,
        out_shape=[
            jax.ShapeDtypeStruct((N_MAIN, Q_DIM), _bf16),
            jax.ShapeDtypeStruct((BATCH, WINDOW, KV_DIM), _f32),
            jax.ShapeDtypeStruct((BATCH, WINDOW, KV_DIM), _f32),
        ],
        compiler_params=_params(("arbitrary", "arbitrary")),
        name="attn_prompt",
    )(sinks, zm, zm, zm, zs, zs)


SEQ_CHUNK = 8
N_QROWS = N_HEADS * DEC_SEQ
N_KEYS = WINDOW + DEC_SEQ
N_KEYS_PAD = ((N_KEYS + 7) // 8) * 8


def _attn_sample_kernel(q_ref, kn_ref, vn_ref, ck_ref, cv_ref, sink_ref,
                        o_ref, nk_ref, nv_ref, kk_buf, vv_buf):
    q4 = q_ref[...]
    shp = (SEQ_CHUNK, N_QROWS, KV_DIM)
    row_kv = lax.broadcasted_iota(jnp.int32, shp, 1) // (GROUP * DEC_SEQ)
    lane_kv = lax.broadcasted_iota(jnp.int32, shp, 2) // HEAD_DIM
    own = row_kv == lane_kv
    qp = jnp.where(own, q4, 0.0)

    kk_buf[:, 0:WINDOW, :] = ck_ref[...]
    kk_buf[:, WINDOW:N_KEYS, :] = kn_ref[...]
    kk_buf[:, N_KEYS:N_KEYS_PAD, :] = jnp.zeros((SEQ_CHUNK, N_KEYS_PAD - N_KEYS, KV_DIM), _f32)
    vv_buf[:, 0:WINDOW, :] = cv_ref[...]
    vv_buf[:, WINDOW:N_KEYS, :] = vn_ref[...]
    vv_buf[:, N_KEYS:N_KEYS_PAD, :] = jnp.zeros((SEQ_CHUNK, N_KEYS_PAD - N_KEYS, KV_DIM), _f32)
    nk_ref[...] = kk_buf[:, DEC_SEQ:N_KEYS, :]
    nv_ref[...] = vv_buf[:, DEC_SEQ:N_KEYS, :]

    kk = kk_buf[...]
    vv = vv_buf[...]
    s = jnp.einsum('bqd,bkd->bqk', qp, kk, preferred_element_type=_f32)

    row = lax.broadcasted_iota(jnp.int32, (N_QROWS, N_KEYS_PAD), 0)
    col = lax.broadcasted_iota(jnp.int32, (N_QROWS, N_KEYS_PAD), 1)
    step = row % DEC_SEQ
    head = (row // DEC_SEQ + 1).astype(_f32)
    dist = step + WINDOW - col
    valid = (dist >= 0) & (dist <= WINDOW) & (col < N_KEYS)
    slope = jnp.exp2(head * (-8.0 / N_HEADS))
    bias = jnp.where(valid, -slope * dist.astype(_f32), NEG_INF)
    s = jnp.where(valid[None], s + bias[None], NEG_INF)
    sink = sink_ref[...][None]
    m = jnp.maximum(jnp.max(s, axis=-1, keepdims=True), sink)
    e = jnp.exp(s - m)
    denom = jnp.sum(e, axis=-1, keepdims=True) + jnp.exp(sink - m)
    o = jnp.einsum('bqk,bkd->bqd', e, vv, preferred_element_type=_f32) / denom
    o = jnp.where(own, o, 0.0)
    acc = o
    for g in range(1, N_KV_HEADS):
        acc = acc + pltpu.roll(o, g * HEAD_DIM, axis=2)
    o_ref[...] = acc[:, :, 0:HEAD_DIM]


def _attn_sample(q_s, k_new, v_new, cache_k, cache_v, sink_rows):
    c = SEQ_CHUNK
    return pl.pallas_call(
        _attn_sample_kernel,
        grid=(DEC_BATCH // c,),
        in_specs=[
            pl.BlockSpec((c, N_QROWS, KV_DIM), lambda i: (i, 0, 0)),
            pl.BlockSpec((c, DEC_SEQ, KV_DIM), lambda i: (i, 0, 0)),
            pl.BlockSpec((c, DEC_SEQ, KV_DIM), lambda i: (i, 0, 0)),
            pl.BlockSpec((c, WINDOW, KV_DIM), lambda i: (i, 0, 0)),
            pl.BlockSpec((c, WINDOW, KV_DIM), lambda i: (i, 0, 0)),
            pl.BlockSpec((N_QROWS, 1), lambda i: (0, 0)),
        ],
        out_specs=[
            pl.BlockSpec((c, N_QROWS, HEAD_DIM), lambda i: (i, 0, 0)),
            pl.BlockSpec((c, WINDOW, KV_DIM), lambda i: (i, 0, 0)),
            pl.BlockSpec((c, WINDOW, KV_DIM), lambda i: (i, 0, 0)),
        ],
        out_shape=[
            jax.ShapeDtypeStruct((DEC_BATCH, N_QROWS, HEAD_DIM), _f32),
            jax.ShapeDtypeStruct((DEC_BATCH, WINDOW, KV_DIM), _f32),
            jax.ShapeDtypeStruct((DEC_BATCH, WINDOW, KV_DIM), _f32),
        ],
        scratch_shapes=[
            pltpu.VMEM((c, N_KEYS_PAD, KV_DIM), _f32),
            pltpu.VMEM((c, N_KEYS_PAD, KV_DIM), _f32),
        ],
        compiler_params=_params(("arbitrary",)),
        name="attn_sample",
    )(q_s, k_new, v_new, cache_k, cache_v, sink_rows)


def _gate_kernel(cbm_ref, cbs_ref, atm_ref, ats_ref, gcm_ref, gcs_ref, gam_ref, gas_ref,
                 wc_ref, wa_ref, tm_ref, ts_ref, cb_all, at_all):
    @pl.when(pl.program_id(1) == 0)
    def _():
        cb_all[0:TM, :] = cbm_ref[...]
        cb_all[TM:, :] = cbs_ref[...]
        at_all[0:TM, :] = atm_ref[...]
        at_all[TM:, :] = ats_ref[...]

    cb = cb_all[...]
    at = at_all[...]
    for c in range(TN_OUT // GATE_PIECE):
        cols = slice(c * GATE_PIECE, (c + 1) * GATE_PIECE)
        yc = jnp.dot(cb, wc_ref[:, cols], preferred_element_type=_f32)
        ya = jnp.dot(at, wa_ref[:, cols], preferred_element_type=_f32)
        tm_ref[:, cols] = (jax.nn.sigmoid(gcm_ref[:, cols]) * yc[0:TM, :]
                           + jax.nn.sigmoid(gam_ref[:, cols]) * ya[0:TM, :]).astype(_bf16)
        ts_ref[:, cols] = (jax.nn.sigmoid(gcs_ref[:, cols]) * yc[TM:, :]
                           + jax.nn.sigmoid(gas_ref[:, cols]) * ya[TM:, :]).astype(_bf16)


def _gate(cbm, cbs, atm, ats, zm, zs, wc, wa):
    gc0 = OFF_GC // TN_OUT
    ga0 = OFF_GA // TN_OUT
    row = lambda i, n: (i, 0)
    return pl.pallas_call(
        _gate_kernel,
        grid=(N_TILES, N_OUT_CHUNKS),
        in_specs=[
            pl.BlockSpec((TM, D_CONV), row),
            pl.BlockSpec((TS, D_CONV), row),
            pl.BlockSpec((TM, Q_DIM), row),
            pl.BlockSpec((TS, Q_DIM), row),
            pl.BlockSpec((TM, TN_OUT), lambda i, n: (i, gc0 + n)),
            pl.BlockSpec((TS, TN_OUT), lambda i, n: (i, gc0 + n)),
            pl.BlockSpec((TM, TN_OUT), lambda i, n: (i, ga0 + n)),
            pl.BlockSpec((TS, TN_OUT), lambda i, n: (i, ga0 + n)),
            pl.BlockSpec((D_CONV, TN_OUT), lambda i, n: (0, n)),
            pl.BlockSpec((Q_DIM, TN_OUT), lambda i, n: (0, n)),
        ],
        out_specs=[
            pl.BlockSpec((TM, TN_OUT), lambda i, n: (i, n)),
            pl.BlockSpec((TS, TN_OUT), lambda i, n: (i, n)),
        ],
        out_shape=[
            jax.ShapeDtypeStruct((N_MAIN, D_MODEL), _bf16),
            jax.ShapeDtypeStruct((N_SMALL, D_MODEL), _bf16),
        ],
        scratch_shapes=[
            pltpu.VMEM((TM + TS, D_CONV), _bf16),
            pltpu.VMEM((TM + TS, Q_DIM), _bf16),
        ],
        compiler_params=_params(("parallel", "arbitrary")),
        name="gate",
    )(cbm, cbs, atm, ats, zm, zs, zm, zs, wc, wa)


def _sample_rows(a):
    return a.transpose(1, 0, 2).reshape(N_SAMPLE, a.shape[-1])


def _to_small(sample_rows, dtype):
    w = sample_rows.shape[-1]
    return jnp.concatenate([
        jnp.zeros((SAMPLE_ROW0, w), dtype), sample_rows.astype(dtype),
        jnp.zeros((N_SMALL - SAMPLE_ROW0 - N_SAMPLE, w), dtype)], axis=0)


def kernel(x_prompt, x_sample, state_conv, cache_k_win, cache_v_win, meta_tokens, ffn1_norm, ffn1_w_up, ffn1_w_down, mix_norm, w_in, q_norm, k_norm, conv_w, w_conv_out, attn_sinks, w_attn_out, w_o, ffn2_norm, ffn2_w_up, ffn2_w_down):
    l = 0
    xm = x_prompt.reshape(N_MAIN, D_MODEL)
    xs = jnp.concatenate([
        meta_tokens.astype(_f32), _sample_rows(x_sample),
        jnp.zeros((N_SMALL - N_META - N_SAMPLE, D_MODEL), _f32)], axis=0)

    wgu1 = _cast(_w_up_job(256), ffn1_w_up[l])
    wd1 = _cast(_w_down_job(TF), ffn1_w_down[l])
    jobs = (_w_up_job(16), _w_down_job(32), _w_in_job(16),
            _cast_job(D_CONV, D_MODEL, 16), _cast_job(Q_DIM, D_MODEL, 16),
            _cast_job(D_MODEL, D_MODEL, 16))
    x1m, x1s, wgu2, wd2, w_in_b, wc_b, wa_b, wo_b = _ffn(
        xm, xs, ffn1_norm[l].reshape(1, D_MODEL), wgu1, wd1, jobs=jobs,
        job_inputs=(ffn2_w_up[l], ffn2_w_down[l], w_in[l], w_conv_out[l], w_attn_out[l], w_o[l]))
    zm, zs = _inproj(x1m, x1s, mix_norm[l].reshape(1, D_MODEL), w_in_b, q_norm[l], k_norm[l])

    cbm, new_conv_p = _conv_prompt(zm, zs, conv_w[l])
    cbs, new_conv_s = _conv_sample(zs, state_conv[l].transpose(1, 0, 2), conv_w[l])

    atm, new_k_p, new_v_p = _attn_prompt(zm, zs, attn_sinks[l])
    zsamp = zs[SAMPLE_ROW0:SAMPLE_ROW0 + N_SAMPLE]
    q_s = (zsamp[:, OFF_Q:OFF_Q + Q_DIM].reshape(DEC_SEQ, DEC_BATCH, N_HEADS, HEAD_DIM)
           .transpose(1, 2, 0, 3).reshape(DEC_BATCH, N_QROWS, HEAD_DIM))
    q_s = jnp.tile(q_s, (1, 1, N_KV_HEADS))
    k_new = zsamp[:, OFF_K:OFF_K + KV_DIM].reshape(DEC_SEQ, DEC_BATCH, KV_DIM).transpose(1, 0, 2)
    v_new = zsamp[:, OFF_V:OFF_V + KV_DIM].reshape(DEC_SEQ, DEC_BATCH, KV_DIM).transpose(1, 0, 2)
    sink_rows = jnp.repeat(attn_sinks[l].astype(_f32), DEC_SEQ).reshape(N_QROWS, 1)
    o_s, new_k_s, new_v_s = _attn_sample(
        q_s, k_new, v_new,
        cache_k_win[l].reshape(DEC_BATCH, WINDOW, KV_DIM),
        cache_v_win[l].reshape(DEC_BATCH, WINDOW, KV_DIM), sink_rows)
    at_rows = (o_s.reshape(DEC_BATCH, N_HEADS, DEC_SEQ, HEAD_DIM)
               .transpose(2, 0, 1, 3).reshape(N_SAMPLE, Q_DIM))
    ats = _to_small(at_rows, _bf16)

    tmn, tsm = _gate(cbm, cbs, atm, ats, zm, zs, wc_b, wa_b)
    ym, ys = _ffn(x1m, x1s, ffn2_norm[l].reshape(1, D_MODEL), wgu2, wd2, proj=(tmn, tsm, wo_b))

    y_prompt = ym.reshape(BATCH, SEQ, D_MODEL)
    y_sample = (ys[SAMPLE_ROW0:SAMPLE_ROW0 + N_SAMPLE]
                .reshape(DEC_SEQ, DEC_BATCH, D_MODEL).transpose(1, 0, 2))
    kv_shape_p = (1, BATCH, WINDOW, N_KV_HEADS, HEAD_DIM)
    kv_shape_s = (1, DEC_BATCH, WINDOW, N_KV_HEADS, HEAD_DIM)
    return (y_prompt, y_sample,
            new_conv_p[None],
            new_k_p.reshape(kv_shape_p), new_v_p.reshape(kv_shape_p),
            new_conv_s.transpose(1, 0, 2)[None],
            new_k_s.reshape(kv_shape_s), new_v_s.reshape(kv_shape_s))
```

```python
import functools
from typing import NamedTuple

import jax
import jax.numpy as jnp
from jax import lax
from jax.experimental import pallas as pl
from jax.experimental.pallas import tpu as pltpu

D_MODEL = 2048
BATCH = 4
SEQ = 2048
DEC_BATCH = 32
DEC_SEQ = 4
PAST_LEN = 16384
N_META = 16
D_CONV = D_MODEL // 2
CONV_WIDTH = 3
HEAD_DIM = 64
N_HEADS = (D_MODEL // 2) // HEAD_DIM
N_KV_HEADS = N_HEADS // 4
GROUP = N_HEADS // N_KV_HEADS
Q_DIM = N_HEADS * HEAD_DIM
KV_DIM = N_KV_HEADS * HEAD_DIM
PAIR = 2 * HEAD_DIM
WINDOW = 128
D_FF = ((8 * D_MODEL // 3 + 127) // 128) * 128
D_IN = 3 * D_CONV + Q_DIM + 2 * KV_DIM + 2 * D_MODEL
EPS = 1e-6
NEG_INF = -1e30

OFF_XC = 0
OFF_BG = D_CONV
OFF_CG = 2 * D_CONV
OFF_Q = 3 * D_CONV
OFF_K = OFF_Q + Q_DIM
OFF_V = OFF_K + KV_DIM
OFF_GC = OFF_V + KV_DIM
OFF_GA = OFF_GC + D_MODEL
N_QKV = Q_DIM + 2 * KV_DIM

N_MAIN = BATCH * SEQ
N_SAMPLE = DEC_BATCH * DEC_SEQ
N_SMALL = 256
SAMPLE_ROW0 = N_META

N_TILES = 16
TM = N_MAIN // N_TILES
TS = N_SMALL // N_TILES
TILE = TM + TS
N_ALL = N_TILES * TILE
TILES_PER_SEQ = SEQ // TM

TF = 512
D_FF_PAD = ((D_FF + TF - 1) // TF) * TF
N_FF_CHUNKS = D_FF_PAD // TF
FFN_STEPS = N_TILES * N_FF_CHUNKS
TN_OUT = 1024
N_OUT_CHUNKS = D_MODEL // TN_OUT
GATE_PIECE = 512
GATE_PIECES = TN_OUT // GATE_PIECE

QBLK = 128
KPAD = QBLK - N_META
N_QBLK = SEQ // QBLK
KBUF_ROWS = QBLK + SEQ

VMEM_LIMIT = 56 * 1024 * 1024

_bf16 = jnp.bfloat16
_f32 = jnp.float32


def _params(sem):
    return pltpu.CompilerParams(dimension_semantics=sem, vmem_limit_bytes=VMEM_LIMIT)


def _rms_rows(x, g):
    ms = jnp.mean(x * x, axis=-1, keepdims=True)
    return x * lax.rsqrt(ms + EPS) * g


def _pair_rms(x, g_pair):
    low = lax.broadcasted_iota(jnp.int32, x.shape, 1) < HEAD_DIM
    x2 = x * x
    s_lo = jnp.sum(jnp.where(low, x2, 0.0), axis=-1, keepdims=True)
    s_hi = jnp.sum(jnp.where(low, 0.0, x2), axis=-1, keepdims=True)
    ms = jnp.where(low, s_lo, s_hi) * (1.0 / HEAD_DIM)
    return x * lax.rsqrt(ms + EPS) * g_pair


class _CastJob(NamedTuple):
    slab: int
    rows_in: int
    n_in_slabs: int
    n_out_slabs: int
    n_in_cols: int
    n_out_cols: int
    segments: tuple
    zero_ranges: tuple


def _cast_job(rows_in, cols_in, slab, segments=None, zero_ranges=(), rows_out=None, cols_out=None):
    rows_out = rows_in if rows_out is None else rows_out
    cols_out = cols_in if cols_out is None else cols_out
    segments = ((0, cols_in, 0),) if segments is None else segments
    assert rows_out % slab == 0
    return _CastJob(slab, rows_in, -(-rows_in // slab), rows_out // slab, cols_in, cols_out,
                    tuple(segments), tuple(zero_ranges))


def _job_specs(job, step_of):
    in_idx = lambda *ids: (jnp.minimum(step_of(*ids), job.n_in_slabs - 1), 0)
    out_idx = lambda *ids: (jnp.minimum(step_of(*ids), job.n_out_slabs - 1), 0)
    return (pl.BlockSpec((job.slab, job.n_in_cols), in_idx),
            pl.BlockSpec((job.slab, job.n_out_cols), out_idx))


def _run_cast_job(job, step, x_ref, o_ref):
    out_slab = jnp.minimum(step, job.n_out_slabs - 1)
    padded = job.n_out_slabs * job.slab > job.rows_in
    for src, width, dst in job.segments:
        x = x_ref[:, src:src + width]
        if padded:
            row = out_slab * job.slab + lax.broadcasted_iota(jnp.int32, x.shape, 0)
            x = jnp.where(row < job.rows_in, x, 0.0)
        o_ref[:, dst:dst + width] = x.astype(_bf16)
    for lo, hi in job.zero_ranges:
        o_ref[:, lo:hi] = jnp.zeros((job.slab, hi - lo), _bf16)


def _cast_kernel(job, x_ref, o_ref):
    _run_cast_job(job, pl.program_id(0), x_ref, o_ref)


def _cast(job, w):
    in_spec, out_spec = _job_specs(job, lambda i: i)
    return pl.pallas_call(
        functools.partial(_cast_kernel, job),
        grid=(job.n_out_slabs,),
        in_specs=[in_spec],
        out_specs=out_spec,
        out_shape=jax.ShapeDtypeStruct((job.n_out_slabs * job.slab, job.n_out_cols), _bf16),
        compiler_params=_params(("arbitrary",)),
        name="cast",
    )(w)


def _w_up_job(slab):
    return _cast_job(D_MODEL, 2 * D_FF, slab,
                     segments=((0, D_FF, 0), (D_FF, D_FF, D_FF_PAD)),
                     zero_ranges=((D_FF, D_FF_PAD), (D_FF_PAD + D_FF, 2 * D_FF_PAD)),
                     cols_out=2 * D_FF_PAD)


def _w_down_job(slab):
    return _cast_job(D_FF, D_MODEL, slab, rows_out=D_FF_PAD)


def _ffn_kernel(jobs, with_proj, with_norm_out, *refs):
    refs = list(refs)
    take = lambda n: [refs.pop(0) for _ in range(n)]
    xm_ref, xs_ref = take(2)
    if with_proj:
        tm_ref, ts_ref, wo_ref = take(3)
    g_ref, wg_ref, wu_ref, wd_ref = take(4)
    if with_norm_out:
        gn_ref, = take(1)
    job_in = take(len(jobs))
    om_ref, os_ref = take(2)
    if with_norm_out:
        hn_ref, = take(1)
    job_out = take(len(jobs))
    h_ref = refs[0]
    i = pl.program_id(0)
    j = pl.program_id(1)

    @pl.when(j == 0)
    def _():
        g = g_ref[...]
        xm = xm_ref[...]
        xs = xs_ref[...]
        if with_proj:
            t_all = refs[1]
            t_all[0:TM, :] = tm_ref[...]
            t_all[TM:, :] = ts_ref[...]
            p = jnp.dot(t_all[...], wo_ref[...], preferred_element_type=_f32)
            xm = xm + p[0:TM, :]
            xs = xs + p[TM:, :]
        h_ref[0:TM, :] = _rms_rows(xm, g).astype(_bf16)
        om_ref[...] = xm
        h_ref[TM:, :] = _rms_rows(xs, g).astype(_bf16)
        os_ref[...] = xs

    for job, x_ref, o_ref in zip(jobs, job_in, job_out):
        _run_cast_job(job, i * N_FF_CHUNKS + j, x_ref, o_ref)

    h = h_ref[...]
    gate = jnp.dot(h, wg_ref[...], preferred_element_type=_f32)
    up = jnp.dot(h, wu_ref[...], preferred_element_type=_f32)
    a = (gate * jax.nn.sigmoid(gate) * up * 0.5).astype(_bf16)
    r = jnp.dot(a, wd_ref[...], preferred_element_type=_f32)
    om_ref[...] += r[0:TM, :]
    os_ref[...] += r[TM:, :]

    if with_norm_out:
        @pl.when(j == N_FF_CHUNKS - 1)
        def _():
            gn = gn_ref[...]
            hn_ref[0:TM, :] = _rms_rows(om_ref[...], gn).astype(_bf16)
            hn_ref[TM:, :] = _rms_rows(os_ref[...], gn).astype(_bf16)


def _ffn(xm, xs, g, wgu, wd, proj=None, next_gain=None, jobs=(), job_inputs=()):
    assert all(job.n_out_slabs <= FFN_STEPS for job in jobs)
    job_specs = [_job_specs(job, lambda i, j: i * N_FF_CHUNKS + j) for job in jobs]
    row = lambda i, j: (i, 0)
    fixed = lambda i, j: (0, 0)
    in_specs = [pl.BlockSpec((TM, D_MODEL), row), pl.BlockSpec((TS, D_MODEL), row)]
    out_specs = [pl.BlockSpec((TM, D_MODEL), row), pl.BlockSpec((TS, D_MODEL), row)]
    out_shape = [jax.ShapeDtypeStruct((N_MAIN, D_MODEL), _f32),
                 jax.ShapeDtypeStruct((N_SMALL, D_MODEL), _f32)]
    scratch = [pltpu.VMEM((TILE, D_MODEL), _bf16)]
    args = [xm, xs]
    if proj is not None:
        in_specs += [pl.BlockSpec((TM, D_MODEL), row), pl.BlockSpec((TS, D_MODEL), row),
                     pl.BlockSpec((D_MODEL, D_MODEL), fixed, pipeline_mode=pl.Buffered(1))]
        scratch += [pltpu.VMEM((TILE, D_MODEL), _bf16)]
        args += list(proj)
    in_specs += [
        pl.BlockSpec((1, D_MODEL), fixed),
        pl.BlockSpec((D_MODEL, TF), lambda i, j: (0, j)),
        pl.BlockSpec((D_MODEL, TF), lambda i, j: (0, N_FF_CHUNKS + j)),
        pl.BlockSpec((TF, D_MODEL), lambda i, j: (j, 0)),
    ]
    args += [g, wgu, wgu, wd]
    if next_gain is not None:
        in_specs += [pl.BlockSpec((1, D_MODEL), fixed)]
        args += [next_gain]
        out_specs += [pl.BlockSpec((TILE, D_MODEL), row)]
        out_shape += [jax.ShapeDtypeStruct((N_ALL, D_MODEL), _bf16)]
    in_specs += [s[0] for s in job_specs]
    args += list(job_inputs)
    out_specs += [s[1] for s in job_specs]
    out_shape += [jax.ShapeDtypeStruct((job.n_out_slabs * job.slab, job.n_out_cols), _bf16)
                  for job in jobs]
    return pl.pallas_call(
        functools.partial(_ffn_kernel, tuple(jobs), proj is not None, next_gain is not None),
        grid=(N_TILES, N_FF_CHUNKS),
        in_specs=in_specs,
        out_specs=out_specs,
        out_shape=out_shape,
        scratch_shapes=scratch,
        compiler_params=_params(("arbitrary", "arbitrary")),
        name="ffn",
    )(*args)


N_BRANCH_IN = OFF_GC
CONV_HALO = 8
U_BASE = 112
U_STATE0 = U_BASE + SAMPLE_ROW0 - (CONV_WIDTH - 1) * DEC_BATCH
U_ROWS = U_BASE + N_SMALL
U_LAST0 = U_BASE + SAMPLE_ROW0 + (DEC_SEQ - (CONV_WIDTH - 1)) * DEC_BATCH
assert U_BASE >= (CONV_WIDTH - 1) * DEC_BATCH and U_BASE % TS == 0 and TS == N_META


def _inproj_kernel(h_ref, w_ref, cw_ref, st_ref, qg_ref, kg_ref,
                   cb_ref, q_ref, k_ref, v_ref, sm_ref, ncp_ref, ncs_ref,
                   ubuf, meta_halo, utab):
    i = pl.program_id(0)

    @pl.when(i == 0)
    def _():
        utab[...] = jnp.zeros(utab.shape, _f32)
        utab[U_STATE0:U_STATE0 + (CONV_WIDTH - 1) * DEC_BATCH, :] = st_ref[...]

    z = jnp.dot(h_ref[...], w_ref[...], preferred_element_type=_f32)
    cw = cw_ref[...]
    w0, w1, w2 = cw[0:1, :], cw[1:2, :], cw[2:3, :]
    qg = qg_ref[...]
    kg = kg_ref[...]
    seg = lambda a, off, width: a[:, off:off + width]

    def qkv(rows, q_out, k_out, v_out):
        for t in range(Q_DIM // PAIR):
            q_out(t, _pair_rms(z[rows, OFF_Q + t * PAIR:OFF_Q + (t + 1) * PAIR], qg)
                  * (HEAD_DIM ** -0.5))
        for t in range(KV_DIM // PAIR):
            k_out(t, _pair_rms(z[rows, OFF_K + t * PAIR:OFF_K + (t + 1) * PAIR], kg))
            v_out(t, z[rows, OFF_V + t * PAIR:OFF_V + (t + 1) * PAIR])

    zs = z[TM:, :]
    u_s = seg(zs, OFF_CG, D_CONV) * seg(zs, OFF_XC, D_CONV)
    cur = pl.multiple_of(U_BASE + TS * i, TS)
    slot = pl.multiple_of(jnp.where(i == 0, U_ROWS, cur), TS)
    utab[pl.ds(slot, TS), :] = u_s
    y_s = (w0 * utab[pl.ds(cur - 2 * DEC_BATCH, TS), :]
           + w1 * utab[pl.ds(cur - DEC_BATCH, TS), :] + w2 * u_s)
    cb_ref[TM:, :] = (seg(zs, OFF_BG, D_CONV) * y_s).astype(_bf16)

    @pl.when(i == 0)
    def _():
        meta_halo[...] = u_s[N_META - CONV_HALO:N_META, :]

    def sm_store(off):
        def store(t, val):
            sm_ref[:, off + t * PAIR:off + (t + 1) * PAIR] = val
        return store
    qkv(slice(TM, TILE), sm_store(0), sm_store(Q_DIM), sm_store(Q_DIM + KV_DIM))

    @pl.when(i == N_TILES - 1)
    def _():
        ncs_ref[...] = utab[U_LAST0:U_LAST0 + (CONV_WIDTH - 1) * DEC_BATCH, :]

    @pl.when(i % TILES_PER_SEQ == 0)
    def _():
        ubuf[0:CONV_HALO, :] = meta_halo[...]

    zm = z[0:TM, :]
    u = seg(zm, OFF_CG, D_CONV) * seg(zm, OFF_XC, D_CONV)
    ubuf[CONV_HALO:CONV_HALO + TM, :] = u
    u1 = ubuf[CONV_HALO - 1:CONV_HALO - 1 + TM, :]
    u2 = ubuf[CONV_HALO - 2:CONV_HALO - 2 + TM, :]
    cb_ref[0:TM, :] = (seg(zm, OFF_BG, D_CONV) * (w0 * u2 + w1 * u1 + w2 * u)).astype(_bf16)
    tail = ubuf[TM:TM + CONV_HALO, :]
    ubuf[0:CONV_HALO, :] = tail

    @pl.when(i % TILES_PER_SEQ == TILES_PER_SEQ - 1)
    def _():
        ncp_ref[0] = tail[CONV_HALO - (CONV_WIDTH - 1):CONV_HALO, :]

    def lane_store(ref):
        def store(t, val):
            ref[:, t * PAIR:(t + 1) * PAIR] = val
        return store
    qkv(slice(0, TM), lane_store(q_ref), lane_store(k_ref), lane_store(v_ref))


def _inproj(h_all, w_in_b, conv_w, state_rows, q_gain, k_gain):
    qg = jnp.tile(q_gain.reshape(1, HEAD_DIM), (1, 2))
    kg = jnp.tile(k_gain.reshape(1, HEAD_DIM), (1, 2))
    row = lambda i: (i, 0)
    fixed = lambda i: (0, 0)
    return pl.pallas_call(
        _inproj_kernel,
        grid=(N_TILES,),
        in_specs=[
            pl.BlockSpec((TILE, D_MODEL), row),
            pl.BlockSpec((D_MODEL, N_BRANCH_IN), fixed, pipeline_mode=pl.Buffered(1)),
            pl.BlockSpec((CONV_WIDTH, D_CONV), fixed),
            pl.BlockSpec(((CONV_WIDTH - 1) * DEC_BATCH, D_CONV), fixed),
            pl.BlockSpec((1, PAIR), fixed),
            pl.BlockSpec((1, PAIR), fixed),
        ],
        out_specs=[
            pl.BlockSpec((TILE, D_CONV), row),
            pl.BlockSpec((TM, Q_DIM), row),
            pl.BlockSpec((TM, KV_DIM), row),
            pl.BlockSpec((TM, KV_DIM), row),
            pl.BlockSpec((TS, N_QKV), row),
            pl.BlockSpec((1, CONV_WIDTH - 1, D_CONV), lambda i: (i // TILES_PER_SEQ, 0, 0)),
            pl.BlockSpec(((CONV_WIDTH - 1) * DEC_BATCH, D_CONV), fixed),
        ],
        out_shape=[
            jax.ShapeDtypeStruct((N_ALL, D_CONV), _bf16),
            jax.ShapeDtypeStruct((N_MAIN, Q_DIM), _f32),
            jax.ShapeDtypeStruct((N_MAIN, KV_DIM), _f32),
            jax.ShapeDtypeStruct((N_MAIN, KV_DIM), _f32),
            jax.ShapeDtypeStruct((N_SMALL, N_QKV), _f32),
            jax.ShapeDtypeStruct((BATCH, CONV_WIDTH - 1, D_CONV), _f32),
            jax.ShapeDtypeStruct(((CONV_WIDTH - 1) * DEC_BATCH, D_CONV), _f32),
        ],
        scratch_shapes=[
            pltpu.VMEM((CONV_HALO + TM, D_CONV), _f32),
            pltpu.VMEM((CONV_HALO, D_CONV), _f32),
            pltpu.VMEM((U_ROWS + TS, D_CONV), _f32),
        ],
        compiler_params=_params(("arbitrary",)),
        name="inproj",
    )(h_all, w_in_b, conv_w, state_rows, qg, kg)


QPAIR = 2 * GROUP * HEAD_DIM
HEADS_PER_STEP = 2 * GROUP


def _attn_prompt_kernel(sink_ref, q_ref, k_ref, v_ref, km_ref, vm_ref,
                        o_ref, nk_ref, nv_ref, kbuf, vbuf, bias_ref):
    gp = pl.program_id(1)

    kbuf[0:KPAD, :] = jnp.zeros((KPAD, PAIR), _bf16)
    vbuf[0:KPAD, :] = jnp.zeros((KPAD, PAIR), _bf16)
    kbuf[KPAD:QBLK, :] = km_ref[...].astype(_bf16)
    vbuf[KPAD:QBLK, :] = vm_ref[...].astype(_bf16)
    kbuf[QBLK:KBUF_ROWS, :] = k_ref[...].astype(_bf16)
    vbuf[QBLK:KBUF_ROWS, :] = v_ref[...].astype(_bf16)
    nk_ref[0] = k_ref[SEQ - WINDOW:SEQ, :]
    nv_ref[0] = v_ref[SEQ - WINDOW:SEQ, :]

    row = lax.broadcasted_iota(jnp.int32, (QBLK, 2 * QBLK), 0)
    col = lax.broadcasted_iota(jnp.int32, (QBLK, 2 * QBLK), 1)
    dist = QBLK + row - col
    in_window = (dist >= 0) & (dist <= WINDOW)
    distf = dist.astype(_f32)
    for hh in range(HEADS_PER_STEP):
        head = (gp * HEADS_PER_STEP + hh + 1).astype(_f32)
        slope = jnp.exp2(jnp.full((QBLK, 2 * QBLK), -8.0 / N_HEADS, _f32) * head)
        bias_ref[hh] = jnp.where(in_window, -slope * distf, NEG_INF)

    lane = lax.broadcasted_iota(jnp.int32, (QBLK, PAIR), 1)
    low_half = lane < HEAD_DIM

    def block(i, carry):
        r0 = pl.multiple_of(i * QBLK, QBLK)
        qn = q_ref[pl.ds(r0 - QBLK, QBLK), :]
        kk = kbuf[pl.ds(r0 - QBLK, 2 * QBLK), :]
        vv = vbuf[pl.ds(r0 - QBLK, 2 * QBLK), :]
        pad_key = col < jnp.where(i == 1, KPAD, 0)
        for g in range(2):
            keep = low_half if g == 0 else jnp.logical_not(low_half)
            outs = []
            for hp in range(GROUP):
                hh = g * GROUP + hp
                tile = qn[:, (hh // 2) * PAIR:(hh // 2 + 1) * PAIR]
                if hp % 2 != g:
                    tile = pltpu.roll(tile, HEAD_DIM, axis=1)
                qp = jnp.where(keep, tile, 0.0).astype(_bf16)
                s = lax.dot_general(qp, kk, (((1,), (1,)), ((), ())),
                                    preferred_element_type=_f32)
                s = jnp.where(pad_key, NEG_INF, s + bias_ref[hh])
                sink = sink_ref[gp * HEADS_PER_STEP + hh]
                m = jnp.maximum(jnp.max(s, axis=-1, keepdims=True), sink)
                e = jnp.exp(s - m)
                denom = jnp.sum(e, axis=-1, keepdims=True) + jnp.exp(sink - m)
                o = jnp.dot(e.astype(_bf16), vv, preferred_element_type=_f32)
                outs.append(o / denom)
            for pr in range(GROUP // 2):
                even, odd = outs[2 * pr], outs[2 * pr + 1]
                if g == 0:
                    odd = pltpu.roll(odd, HEAD_DIM, axis=1)
                else:
                    even = pltpu.roll(even, HEAD_DIM, axis=1)
                t = g * (GROUP // 2) + pr
                o_ref[pl.ds(r0 - QBLK, QBLK), t * PAIR:(t + 1) * PAIR] = (
                    jnp.where(low_half, even, odd).astype(_bf16))
        return carry

    lax.fori_loop(1, N_QBLK + 1, block, 0)


def _attn_prompt(qm, km, vm, qkv_s, sinks):
    grid_spec = pltpu.PrefetchScalarGridSpec(
        num_scalar_prefetch=1,
        grid=(BATCH, N_KV_HEADS // 2),
        in_specs=[
            pl.BlockSpec((SEQ, QPAIR), lambda b, p, s: (b, p)),
            pl.BlockSpec((SEQ, PAIR), lambda b, p, s: (b, p)),
            pl.BlockSpec((SEQ, PAIR), lambda b, p, s: (b, p)),
            pl.BlockSpec((N_META, PAIR), lambda b, p, s: (0, Q_DIM // PAIR + p)),
            pl.BlockSpec((N_META, PAIR), lambda b, p, s: (0, (Q_DIM + KV_DIM) // PAIR + p)),
        ],
        out_specs=[
            pl.BlockSpec((SEQ, QPAIR), lambda b, p, s: (b, p)),
            pl.BlockSpec((1, WINDOW, PAIR), lambda b, p, s: (b, 0, p)),
            pl.BlockSpec((1, WINDOW, PAIR), lambda b, p, s: (b, 0, p)),
        ],
        scratch_shapes=[
            pltpu.VMEM((KBUF_ROWS, PAIR), _bf16),
            pltpu.VMEM((KBUF_ROWS, PAIR), _bf16),
            pltpu.VMEM((HEADS_PER_STEP, QBLK, 2 * QBLK), _f32),
        ],
    )
    return pl.pallas_call(
        _attn_prompt_kernel,
        grid_spec=grid_spec,
        out_shape=[
            jax.ShapeDtypeStruct((N_MAIN, Q_DIM), _bf16),
            jax.ShapeDtypeStruct((BATCH, WINDOW, KV_DIM), _f32),
            jax.ShapeDtypeStruct((BATCH, WINDOW, KV_DIM), _f32),
        ],
        compiler_params=_params(("arbitrary", "arbitrary")),
        name="attn_prompt",
    )(sinks, qm, km, vm, qkv_s, qkv_s)


SEQ_CHUNK = 8
N_QROWS = N_HEADS * DEC_SEQ
N_KEYS = WINDOW + DEC_SEQ
N_KEYS_PAD = ((N_KEYS + 7) // 8) * 8


def _attn_sample_kernel(q_ref, kn_ref, vn_ref, ck_ref, cv_ref, sink_ref,
                        o_ref, nk_ref, nv_ref, kk_buf, vv_buf):
    q4 = q_ref[...]
    shp = (SEQ_CHUNK, N_QROWS, KV_DIM)
    row_kv = lax.broadcasted_iota(jnp.int32, shp, 1) // (GROUP * DEC_SEQ)
    lane_kv = lax.broadcasted_iota(jnp.int32, shp, 2) // HEAD_DIM
    own = row_kv == lane_kv
    qp = jnp.where(own, q4, 0.0)

    kk_buf[:, 0:WINDOW, :] = ck_ref[...]
    kk_buf[:, WINDOW:N_KEYS, :] = kn_ref[...]
    kk_buf[:, N_KEYS:N_KEYS_PAD, :] = jnp.zeros((SEQ_CHUNK, N_KEYS_PAD - N_KEYS, KV_DIM), _f32)
    vv_buf[:, 0:WINDOW, :] = cv_ref[...]
    vv_buf[:, WINDOW:N_KEYS, :] = vn_ref[...]
    vv_buf[:, N_KEYS:N_KEYS_PAD, :] = jnp.zeros((SEQ_CHUNK, N_KEYS_PAD - N_KEYS, KV_DIM), _f32)
    nk_ref[...] = kk_buf[:, DEC_SEQ:N_KEYS, :]
    nv_ref[...] = vv_buf[:, DEC_SEQ:N_KEYS, :]

    kk = kk_buf[...]
    vv = vv_buf[...]
    s = jnp.einsum('bqd,bkd->bqk', qp, kk, preferred_element_type=_f32)

    row = lax.broadcasted_iota(jnp.int32, (N_QROWS, N_KEYS_PAD), 0)
    col = lax.broadcasted_iota(jnp.int32, (N_QROWS, N_KEYS_PAD), 1)
    step = row % DEC_SEQ
    head = (row // DEC_SEQ + 1).astype(_f32)
    dist = step + WINDOW - col
    valid = (dist >= 0) & (dist <= WINDOW) & (col < N_KEYS)
    slope = jnp.exp2(head * (-8.0 / N_HEADS))
    bias = jnp.where(valid, -slope * dist.astype(_f32), NEG_INF)
    s = jnp.where(valid[None], s + bias[None], NEG_INF)
    sink = sink_ref[...][None]
    m = jnp.maximum(jnp.max(s, axis=-1, keepdims=True), sink)
    e = jnp.exp(s - m)
    denom = jnp.sum(e, axis=-1, keepdims=True) + jnp.exp(sink - m)
    o = jnp.einsum('bqk,bkd->bqd', e, vv, preferred_element_type=_f32) / denom
    o = jnp.where(own, o, 0.0)
    acc = o
    for g in range(1, N_KV_HEADS):
        acc = acc + pltpu.roll(o, g * HEAD_DIM, axis=2)
    o_ref[...] = acc[:, :, 0:HEAD_DIM]


def _attn_sample(q_s, k_new, v_new, cache_k, cache_v, sink_rows):
    c = SEQ_CHUNK
    return pl.pallas_call(
        _attn_sample_kernel,
        grid=(DEC_BATCH // c,),
        in_specs=[
            pl.BlockSpec((c, N_QROWS, KV_DIM), lambda i: (i, 0, 0)),
            pl.BlockSpec((c, DEC_SEQ, KV_DIM), lambda i: (i, 0, 0)),
            pl.BlockSpec((c, DEC_SEQ, KV_DIM), lambda i: (i, 0, 0)),
            pl.BlockSpec((c, WINDOW, KV_DIM), lambda i: (i, 0, 0)),
            pl.BlockSpec((c, WINDOW, KV_DIM), lambda i: (i, 0, 0)),
            pl.BlockSpec((N_QROWS, 1), lambda i: (0, 0)),
        ],
        out_specs=[
            pl.BlockSpec((c, N_QROWS, HEAD_DIM), lambda i: (i, 0, 0)),
            pl.BlockSpec((c, WINDOW, KV_DIM), lambda i: (i, 0, 0)),
            pl.BlockSpec((c, WINDOW, KV_DIM), lambda i: (i, 0, 0)),
        ],
        out_shape=[
            jax.ShapeDtypeStruct((DEC_BATCH, N_QROWS, HEAD_DIM), _f32),
            jax.ShapeDtypeStruct((DEC_BATCH, WINDOW, KV_DIM), _f32),
            jax.ShapeDtypeStruct((DEC_BATCH, WINDOW, KV_DIM), _f32),
        ],
        scratch_shapes=[
            pltpu.VMEM((c, N_KEYS_PAD, KV_DIM), _f32),
            pltpu.VMEM((c, N_KEYS_PAD, KV_DIM), _f32),
        ],
        compiler_params=_params(("arbitrary",)),
        name="attn_sample",
    )(q_s, k_new, v_new, cache_k, cache_v, sink_rows)


def _gate_kernel(*refs):
    h_ref, cb_ref, atm_ref, ats_ref = refs[:4]
    w_refs = refs[4:4 + 4 * GATE_PIECES]
    tm_ref, ts_ref, at_all = refs[4 + 4 * GATE_PIECES:]
    at_all[0:TM, :] = atm_ref[...]
    at_all[TM:, :] = ats_ref[...]
    h = h_ref[...]
    cb = cb_ref[...]
    at = at_all[...]
    for c in range(GATE_PIECES):
        wgc_ref, wga_ref, wc_ref, wa_ref = w_refs[4 * c:4 * c + 4]
        cols = slice(c * GATE_PIECE, (c + 1) * GATE_PIECE)
        gc = jnp.dot(h, wgc_ref[...], preferred_element_type=_f32)
        ga = jnp.dot(h, wga_ref[...], preferred_element_type=_f32)
        yc = jnp.dot(cb, wc_ref[...], preferred_element_type=_f32)
        ya = jnp.dot(at, wa_ref[...], preferred_element_type=_f32)
        t = (jax.nn.sigmoid(gc) * yc + jax.nn.sigmoid(ga) * ya).astype(_bf16)
        tm_ref[:, cols] = t[0:TM, :]
        ts_ref[:, cols] = t[TM:, :]


def _gate(h_all, cb_all, atm, ats, w_in_b, wc, wa):
    row = lambda n, i: (i, 0)
    once = pl.Buffered(1)
    w_specs, w_args = [], []
    for c in range(GATE_PIECES):
        piece = lambda n, i, c=c, base=0: (0, base + n * GATE_PIECES + c)
        w_specs += [
            pl.BlockSpec((D_MODEL, GATE_PIECE),
                         functools.partial(piece, base=OFF_GC // GATE_PIECE), pipeline_mode=once),
            pl.BlockSpec((D_MODEL, GATE_PIECE),
                         functools.partial(piece, base=OFF_GA // GATE_PIECE), pipeline_mode=once),
            pl.BlockSpec((D_CONV, GATE_PIECE), piece, pipeline_mode=once),
            pl.BlockSpec((Q_DIM, GATE_PIECE), piece, pipeline_mode=once),
        ]
        w_args += [w_in_b, w_in_b, wc, wa]
    return pl.pallas_call(
        _gate_kernel,
        grid=(N_OUT_CHUNKS, N_TILES),
        in_specs=[
            pl.BlockSpec((TILE, D_MODEL), row),
            pl.BlockSpec((TILE, D_CONV), row),
            pl.BlockSpec((TM, Q_DIM), row),
            pl.BlockSpec((TS, Q_DIM), row),
        ] + w_specs,
        out_specs=[
            pl.BlockSpec((TM, TN_OUT), lambda n, i: (i, n)),
            pl.BlockSpec((TS, TN_OUT), lambda n, i: (i, n)),
        ],
        out_shape=[
            jax.ShapeDtypeStruct((N_MAIN, D_MODEL), _bf16),
            jax.ShapeDtypeStruct((N_SMALL, D_MODEL), _bf16),
        ],
        scratch_shapes=[pltpu.VMEM((TILE, Q_DIM), _bf16)],
        compiler_params=_params(("arbitrary", "arbitrary")),
        name="gate",
    )(h_all, cb_all, atm, ats, *w_args)


def _sample_rows(a):
    return a.transpose(1, 0, 2).reshape(N_SAMPLE, a.shape[-1])


def _to_small(sample_rows, dtype):
    w = sample_rows.shape[-1]
    return jnp.concatenate([
        jnp.zeros((SAMPLE_ROW0, w), dtype), sample_rows.astype(dtype),
        jnp.zeros((N_SMALL - SAMPLE_ROW0 - N_SAMPLE, w), dtype)], axis=0)


def kernel(x_prompt, x_sample, state_conv, cache_k_win, cache_v_win, meta_tokens, ffn1_norm, ffn1_w_up, ffn1_w_down, mix_norm, w_in, q_norm, k_norm, conv_w, w_conv_out, attn_sinks, w_attn_out, w_o, ffn2_norm, ffn2_w_up, ffn2_w_down):
    l = 0
    xm = x_prompt.reshape(N_MAIN, D_MODEL)
    xs = jnp.concatenate([
        meta_tokens.astype(_f32), _sample_rows(x_sample),
        jnp.zeros((N_SMALL - N_META - N_SAMPLE, D_MODEL), _f32)], axis=0)

    wgu1 = _cast(_w_up_job(256), ffn1_w_up[l])
    wd1 = _cast(_w_down_job(TF), ffn1_w_down[l])
    jobs = (_w_up_job(16), _w_down_job(32), _cast_job(D_MODEL, D_IN, 16),
            _cast_job(D_CONV, D_MODEL, 16), _cast_job(Q_DIM, D_MODEL, 16),
            _cast_job(D_MODEL, D_MODEL, 16))
    x1m, x1s, h_all, wgu2, wd2, w_in_b, wc_b, wa_b, wo_b = _ffn(
        xm, xs, ffn1_norm[l].reshape(1, D_MODEL), wgu1, wd1,
        next_gain=mix_norm[l].reshape(1, D_MODEL), jobs=jobs,
        job_inputs=(ffn2_w_up[l], ffn2_w_down[l], w_in[l], w_conv_out[l], w_attn_out[l], w_o[l]))

    state_rows = state_conv[l].transpose(1, 0, 2).reshape((CONV_WIDTH - 1) * DEC_BATCH, D_CONV)
    cb_all, qm, km, vm, qkv_s, new_conv_p, new_conv_s = _inproj(
        h_all, w_in_b, conv_w[l], state_rows, q_norm[l], k_norm[l])

    atm, new_k_p, new_v_p = _attn_prompt(qm, km, vm, qkv_s, attn_sinks[l])
    samp = qkv_s[SAMPLE_ROW0:SAMPLE_ROW0 + N_SAMPLE]
    q_s = (samp[:, 0:Q_DIM].reshape(DEC_SEQ, DEC_BATCH, N_HEADS, HEAD_DIM)
           .transpose(1, 2, 0, 3).reshape(DEC_BATCH, N_QROWS, HEAD_DIM))
    q_s = jnp.tile(q_s, (1, 1, N_KV_HEADS))
    k_new = samp[:, Q_DIM:Q_DIM + KV_DIM].reshape(DEC_SEQ, DEC_BATCH, KV_DIM).transpose(1, 0, 2)
    v_new = samp[:, Q_DIM + KV_DIM:N_QKV].reshape(DEC_SEQ, DEC_BATCH, KV_DIM).transpose(1, 0, 2)
    sink_rows = jnp.repeat(attn_sinks[l].astype(_f32), DEC_SEQ).reshape(N_QROWS, 1)
    o_s, new_k_s, new_v_s = _attn_sample(
        q_s, k_new, v_new,
        cache_k_win[l].reshape(DEC_BATCH, WINDOW, KV_DIM),
        cache_v_win[l].reshape(DEC_BATCH, WINDOW, KV_DIM), sink_rows)
    at_rows = (o_s.reshape(DEC_BATCH, N_HEADS, DEC_SEQ, HEAD_DIM)
               .transpose(2, 0, 1, 3).reshape(N_SAMPLE, Q_DIM))
    ats = _to_small(at_rows, _bf16)

    t_main, t_small = _gate(h_all, cb_all, atm, ats, w_in_b, wc_b, wa_b)
    ym, ys = _ffn(x1m, x1s, ffn2_norm[l].reshape(1, D_MODEL), wgu2, wd2,
                  proj=(t_main, t_small, wo_b))

    y_prompt = ym.reshape(BATCH, SEQ, D_MODEL)
    y_sample = (ys[SAMPLE_ROW0:SAMPLE_ROW0 + N_SAMPLE]
                .reshape(DEC_SEQ, DEC_BATCH, D_MODEL).transpose(1, 0, 2))
    kv_shape_p = (1, BATCH, WINDOW, N_KV_HEADS, HEAD_DIM)
    kv_shape_s = (1, DEC_BATCH, WINDOW, N_KV_HEADS, HEAD_DIM)
    return (y_prompt, y_sample,
            new_conv_p[None],
            new_k_p.reshape(kv_shape_p), new_v_p.reshape(kv_shape_p),
            new_conv_s.reshape(CONV_WIDTH - 1, DEC_BATCH, D_CONV).transpose(1, 0, 2)[None],
            new_k_s.reshape(kv_shape_s), new_v_s.reshape(kv_shape_s))
```

```python
import functools
from typing import NamedTuple

import jax
import jax.numpy as jnp
from jax import lax
from jax.experimental import pallas as pl
from jax.experimental.pallas import tpu as pltpu

D_MODEL = 2048
BATCH = 4
SEQ = 2048
DEC_BATCH = 32
DEC_SEQ = 4
PAST_LEN = 16384
N_META = 16
D_CONV = D_MODEL // 2
CONV_WIDTH = 3
HEAD_DIM = 64
N_HEADS = (D_MODEL // 2) // HEAD_DIM
N_KV_HEADS = N_HEADS // 4
GROUP = N_HEADS // N_KV_HEADS
Q_DIM = N_HEADS * HEAD_DIM
KV_DIM = N_KV_HEADS * HEAD_DIM
PAIR = 2 * HEAD_DIM
WINDOW = 128
D_FF = ((8 * D_MODEL // 3 + 127) // 128) * 128
D_IN = 3 * D_CONV + Q_DIM + 2 * KV_DIM + 2 * D_MODEL
EPS = 1e-6
NEG_INF = -1e30
LOG2E = 1.4426950408889634
Q_SCALE = HEAD_DIM ** -0.5 * LOG2E

OFF_XC = 0
OFF_BG = D_CONV
OFF_CG = 2 * D_CONV
OFF_Q = 3 * D_CONV
OFF_K = OFF_Q + Q_DIM
OFF_V = OFF_K + KV_DIM
OFF_GC = OFF_V + KV_DIM
OFF_GA = OFF_GC + D_MODEL
N_QKV = Q_DIM + 2 * KV_DIM

N_MAIN = BATCH * SEQ
N_SAMPLE = DEC_BATCH * DEC_SEQ
N_SMALL = 256
SAMPLE_ROW0 = N_META

N_TILES = 16
TM = N_MAIN // N_TILES
TS = N_SMALL // N_TILES
TILE = TM + TS
N_ALL = N_TILES * TILE
TILES_PER_SEQ = SEQ // TM

TF = 512
D_FF_PAD = ((D_FF + TF - 1) // TF) * TF
N_FF_CHUNKS = D_FF_PAD // TF
FFN_STEPS = N_TILES * N_FF_CHUNKS
TN_OUT = 1024
N_OUT_CHUNKS = D_MODEL // TN_OUT
GATE_PIECE = 512
GATE_PIECES = TN_OUT // GATE_PIECE

QBLK = 128
KPAD = QBLK - N_META
N_QBLK = SEQ // QBLK
KBUF_ROWS = QBLK + SEQ

VMEM_LIMIT = 56 * 1024 * 1024

_bf16 = jnp.bfloat16
_f32 = jnp.float32


def _params(sem):
    return pltpu.CompilerParams(dimension_semantics=sem, vmem_limit_bytes=VMEM_LIMIT)


def _rms_rows(x, g):
    ms = jnp.mean(x * x, axis=-1, keepdims=True)
    return x * lax.rsqrt(ms + EPS) * g


def _pair_rms(x, g_pair):
    low = lax.broadcasted_iota(jnp.int32, x.shape, 1) < HEAD_DIM
    x2 = x * x
    s_lo = jnp.sum(jnp.where(low, x2, 0.0), axis=-1, keepdims=True)
    s_hi = jnp.sum(jnp.where(low, 0.0, x2), axis=-1, keepdims=True)
    ms = jnp.where(low, s_lo, s_hi) * (1.0 / HEAD_DIM)
    return x * lax.rsqrt(ms + EPS) * g_pair


class _CastJob(NamedTuple):
    slab: int
    rows_in: int
    n_in_slabs: int
    n_out_slabs: int
    n_in_cols: int
    n_out_cols: int
    segments: tuple
    zero_ranges: tuple


def _cast_job(rows_in, cols_in, slab, segments=None, zero_ranges=(), rows_out=None, cols_out=None):
    rows_out = rows_in if rows_out is None else rows_out
    cols_out = cols_in if cols_out is None else cols_out
    segments = ((0, cols_in, 0),) if segments is None else segments
    assert rows_out % slab == 0
    return _CastJob(slab, rows_in, -(-rows_in // slab), rows_out // slab, cols_in, cols_out,
                    tuple(segments), tuple(zero_ranges))


def _job_specs(job, step_of):
    in_idx = lambda *ids: (jnp.minimum(step_of(*ids), job.n_in_slabs - 1), 0)
    out_idx = lambda *ids: (jnp.minimum(step_of(*ids), job.n_out_slabs - 1), 0)
    return (pl.BlockSpec((job.slab, job.n_in_cols), in_idx),
            pl.BlockSpec((job.slab, job.n_out_cols), out_idx))


def _run_cast_job(job, step, x_ref, o_ref):
    out_slab = jnp.minimum(step, job.n_out_slabs - 1)
    padded = job.n_out_slabs * job.slab > job.rows_in
    for src, width, dst in job.segments:
        x = x_ref[:, src:src + width]
        if padded:
            row = out_slab * job.slab + lax.broadcasted_iota(jnp.int32, x.shape, 0)
            x = jnp.where(row < job.rows_in, x, 0.0)
        o_ref[:, dst:dst + width] = x.astype(_bf16)
    for lo, hi in job.zero_ranges:
        o_ref[:, lo:hi] = jnp.zeros((job.slab, hi - lo), _bf16)


def _cast_kernel(job, x_ref, o_ref):
    _run_cast_job(job, pl.program_id(0), x_ref, o_ref)


def _cast(job, w):
    in_spec, out_spec = _job_specs(job, lambda i: i)
    return pl.pallas_call(
        functools.partial(_cast_kernel, job),
        grid=(job.n_out_slabs,),
        in_specs=[in_spec],
        out_specs=out_spec,
        out_shape=jax.ShapeDtypeStruct((job.n_out_slabs * job.slab, job.n_out_cols), _bf16),
        compiler_params=_params(("arbitrary",)),
        name="cast",
    )(w)


def _w_up_job(slab):
    return _cast_job(D_MODEL, 2 * D_FF, slab,
                     segments=((0, D_FF, 0), (D_FF, D_FF, D_FF_PAD)),
                     zero_ranges=((D_FF, D_FF_PAD), (D_FF_PAD + D_FF, 2 * D_FF_PAD)),
                     cols_out=2 * D_FF_PAD)


def _w_down_job(slab):
    return _cast_job(D_FF, D_MODEL, slab, rows_out=D_FF_PAD)


def _ffn_kernel(jobs, with_proj, with_norm_out, *refs):
    refs = list(refs)
    take = lambda n: [refs.pop(0) for _ in range(n)]
    xm_ref, xs_ref = take(2)
    if with_proj:
        tm_ref, ts_ref, wo_ref = take(3)
    g_ref, wg_ref, wu_ref, wd_ref = take(4)
    if with_norm_out:
        gn_ref, = take(1)
    job_in = take(len(jobs))
    om_ref, os_ref = take(2)
    if with_norm_out:
        hn_ref, = take(1)
    job_out = take(len(jobs))
    h_ref = refs[0]
    i = pl.program_id(0)
    j = pl.program_id(1)

    @pl.when(j == 0)
    def _():
        g = g_ref[...]
        xm = xm_ref[...]
        xs = xs_ref[...]
        if with_proj:
            t_all = refs[1]
            t_all[0:TM, :] = tm_ref[...]
            t_all[TM:, :] = ts_ref[...]
            p = jnp.dot(t_all[...], wo_ref[...], preferred_element_type=_f32)
            xm = xm + p[0:TM, :]
            xs = xs + p[TM:, :]
        h_ref[0:TM, :] = _rms_rows(xm, g).astype(_bf16)
        om_ref[...] = xm
        h_ref[TM:, :] = _rms_rows(xs, g).astype(_bf16)
        os_ref[...] = xs

    for job, x_ref, o_ref in zip(jobs, job_in, job_out):
        _run_cast_job(job, i * N_FF_CHUNKS + j, x_ref, o_ref)

    h = h_ref[...]
    gate = jnp.dot(h, wg_ref[...], preferred_element_type=_f32)
    up = jnp.dot(h, wu_ref[...], preferred_element_type=_f32)
    a = (gate * jax.nn.sigmoid(gate) * up * 0.5).astype(_bf16)
    r = jnp.dot(a, wd_ref[...], preferred_element_type=_f32)
    om_ref[...] += r[0:TM, :]
    os_ref[...] += r[TM:, :]

    if with_norm_out:
        @pl.when(j == N_FF_CHUNKS - 1)
        def _():
            gn = gn_ref[...]
            hn_ref[0:TM, :] = _rms_rows(om_ref[...], gn).astype(_bf16)
            hn_ref[TM:, :] = _rms_rows(os_ref[...], gn).astype(_bf16)


def _ffn(xm, xs, g, wgu, wd, proj=None, next_gain=None, jobs=(), job_inputs=()):
    assert all(job.n_out_slabs <= FFN_STEPS for job in jobs)
    job_specs = [_job_specs(job, lambda i, j: i * N_FF_CHUNKS + j) for job in jobs]
    row = lambda i, j: (i, 0)
    fixed = lambda i, j: (0, 0)
    in_specs = [pl.BlockSpec((TM, D_MODEL), row), pl.BlockSpec((TS, D_MODEL), row)]
    out_specs = [pl.BlockSpec((TM, D_MODEL), row), pl.BlockSpec((TS, D_MODEL), row)]
    out_shape = [jax.ShapeDtypeStruct((N_MAIN, D_MODEL), _f32),
                 jax.ShapeDtypeStruct((N_SMALL, D_MODEL), _f32)]
    scratch = [pltpu.VMEM((TILE, D_MODEL), _bf16)]
    args = [xm, xs]
    if proj is not None:
        in_specs += [pl.BlockSpec((TM, D_MODEL), row), pl.BlockSpec((TS, D_MODEL), row),
                     pl.BlockSpec((D_MODEL, D_MODEL), fixed, pipeline_mode=pl.Buffered(1))]
        scratch += [pltpu.VMEM((TILE, D_MODEL), _bf16)]
        args += list(proj)
    in_specs += [
        pl.BlockSpec((1, D_MODEL), fixed),
        pl.BlockSpec((D_MODEL, TF), lambda i, j: (0, j)),
        pl.BlockSpec((D_MODEL, TF), lambda i, j: (0, N_FF_CHUNKS + j)),
        pl.BlockSpec((TF, D_MODEL), lambda i, j: (j, 0)),
    ]
    args += [g, wgu, wgu, wd]
    if next_gain is not None:
        in_specs += [pl.BlockSpec((1, D_MODEL), fixed)]
        args += [next_gain]
        out_specs += [pl.BlockSpec((TILE, D_MODEL), row)]
        out_shape += [jax.ShapeDtypeStruct((N_ALL, D_MODEL), _bf16)]
    in_specs += [s[0] for s in job_specs]
    args += list(job_inputs)
    out_specs += [s[1] for s in job_specs]
    out_shape += [jax.ShapeDtypeStruct((job.n_out_slabs * job.slab, job.n_out_cols), _bf16)
                  for job in jobs]
    return pl.pallas_call(
        functools.partial(_ffn_kernel, tuple(jobs), proj is not None, next_gain is not None),
        grid=(N_TILES, N_FF_CHUNKS),
        in_specs=in_specs,
        out_specs=out_specs,
        out_shape=out_shape,
        scratch_shapes=scratch,
        compiler_params=_params(("arbitrary", "arbitrary")),
        name="ffn",
    )(*args)


N_BRANCH_IN = OFF_GC
CONV_HALO = 8
U_BASE = 112
U_STATE0 = U_BASE + SAMPLE_ROW0 - (CONV_WIDTH - 1) * DEC_BATCH
U_ROWS = U_BASE + N_SMALL
U_LAST0 = U_BASE + SAMPLE_ROW0 + (DEC_SEQ - (CONV_WIDTH - 1)) * DEC_BATCH
assert U_BASE >= (CONV_WIDTH - 1) * DEC_BATCH and U_BASE % TS == 0 and TS == N_META


def _inproj_kernel(h_ref, w_ref, cw_ref, st_ref, qg_ref, kg_ref,
                   cb_ref, q_ref, k_ref, v_ref, sm_ref, ncp_ref, ncs_ref,
                   ubuf, meta_halo, utab):
    i = pl.program_id(0)

    @pl.when(i == 0)
    def _():
        utab[...] = jnp.zeros(utab.shape, _f32)
        utab[U_STATE0:U_STATE0 + (CONV_WIDTH - 1) * DEC_BATCH, :] = st_ref[...]
        meta_halo[...] = jnp.zeros(meta_halo.shape, _f32)
        ubuf[0:CONV_HALO, :] = jnp.zeros((CONV_HALO, D_CONV), _f32)

    h = h_ref[...]
    z_q = jnp.dot(h, w_ref[:, OFF_Q:OFF_K], preferred_element_type=_f32)
    z_kv = jnp.dot(h, w_ref[:, OFF_K:OFF_GC], preferred_element_type=_f32)
    z = jnp.dot(h, w_ref[:, 0:OFF_Q], preferred_element_type=_f32)
    cw = cw_ref[...]
    w0, w1, w2 = cw[0:1, :], cw[1:2, :], cw[2:3, :]
    qg = qg_ref[...]
    kg = kg_ref[...]
    seg = lambda a, off, width: a[:, off:off + width]

    def qkv(rows, q_out, k_out, v_out):
        for t in range(Q_DIM // PAIR):
            q_out(t, _pair_rms(z_q[rows, t * PAIR:(t + 1) * PAIR], qg) * Q_SCALE)
        for t in range(KV_DIM // PAIR):
            k_out(t, _pair_rms(z_kv[rows, t * PAIR:(t + 1) * PAIR], kg))
            v_out(t, z_kv[rows, KV_DIM + t * PAIR:KV_DIM + (t + 1) * PAIR])

    zs = z[TM:, :]
    u_s = seg(zs, OFF_CG, D_CONV) * seg(zs, OFF_XC, D_CONV)
    cur = pl.multiple_of(U_BASE + TS * i, TS)
    slot = pl.multiple_of(jnp.where(i == 0, U_ROWS, cur), TS)
    utab[pl.ds(slot, TS), :] = u_s
    y_s = (w0 * utab[pl.ds(cur - 2 * DEC_BATCH, TS), :]
           + w1 * utab[pl.ds(cur - DEC_BATCH, TS), :] + w2 * u_s)
    cb_ref[TM:, :] = (seg(zs, OFF_BG, D_CONV) * y_s).astype(_bf16)
    meta_halo[...] = jnp.where(i == 0, u_s[N_META - CONV_HALO:N_META, :], meta_halo[...])

    def sm_store(off):
        def store(t, val):
            sm_ref[:, off + t * PAIR:off + (t + 1) * PAIR] = val
        return store
    qkv(slice(TM, TILE), sm_store(0), sm_store(Q_DIM), sm_store(Q_DIM + KV_DIM))

    ncs_ref[...] = utab[U_LAST0:U_LAST0 + (CONV_WIDTH - 1) * DEC_BATCH, :]

    ubuf[0:CONV_HALO, :] = jnp.where(i % TILES_PER_SEQ == 0, meta_halo[...], ubuf[0:CONV_HALO, :])

    zm = z[0:TM, :]
    u = seg(zm, OFF_CG, D_CONV) * seg(zm, OFF_XC, D_CONV)
    ubuf[CONV_HALO:CONV_HALO + TM, :] = u
    u1 = ubuf[CONV_HALO - 1:CONV_HALO - 1 + TM, :]
    u2 = ubuf[CONV_HALO - 2:CONV_HALO - 2 + TM, :]
    cb_ref[0:TM, :] = (seg(zm, OFF_BG, D_CONV) * (w0 * u2 + w1 * u1 + w2 * u)).astype(_bf16)
    tail = ubuf[TM:TM + CONV_HALO, :]
    ubuf[0:CONV_HALO, :] = tail
    ncp_ref[0] = tail[CONV_HALO - (CONV_WIDTH - 1):CONV_HALO, :]

    def lane_store(ref):
        def store(t, val):
            ref[:, t * PAIR:(t + 1) * PAIR] = val
        return store
    qkv(slice(0, TM), lane_store(q_ref), lane_store(k_ref), lane_store(v_ref))


def _inproj(h_all, w_in_b, conv_w, state_rows, q_gain, k_gain):
    qg = jnp.tile(q_gain.reshape(1, HEAD_DIM), (1, 2))
    kg = jnp.tile(k_gain.reshape(1, HEAD_DIM), (1, 2))
    row = lambda i: (i, 0)
    fixed = lambda i: (0, 0)
    return pl.pallas_call(
        _inproj_kernel,
        grid=(N_TILES,),
        in_specs=[
            pl.BlockSpec((TILE, D_MODEL), row),
            pl.BlockSpec((D_MODEL, N_BRANCH_IN), fixed, pipeline_mode=pl.Buffered(1)),
            pl.BlockSpec((CONV_WIDTH, D_CONV), fixed),
            pl.BlockSpec(((CONV_WIDTH - 1) * DEC_BATCH, D_CONV), fixed),
            pl.BlockSpec((1, PAIR), fixed),
            pl.BlockSpec((1, PAIR), fixed),
        ],
        out_specs=[
            pl.BlockSpec((TILE, D_CONV), row),
            pl.BlockSpec((TM, Q_DIM), row),
            pl.BlockSpec((TM, KV_DIM), row),
            pl.BlockSpec((TM, KV_DIM), row),
            pl.BlockSpec((TS, N_QKV), row),
            pl.BlockSpec((1, CONV_WIDTH - 1, D_CONV), lambda i: (i // TILES_PER_SEQ, 0, 0)),
            pl.BlockSpec(((CONV_WIDTH - 1) * DEC_BATCH, D_CONV), fixed),
        ],
        out_shape=[
            jax.ShapeDtypeStruct((N_ALL, D_CONV), _bf16),
            jax.ShapeDtypeStruct((N_MAIN, Q_DIM), _f32),
            jax.ShapeDtypeStruct((N_MAIN, KV_DIM), _f32),
            jax.ShapeDtypeStruct((N_MAIN, KV_DIM), _f32),
            jax.ShapeDtypeStruct((N_SMALL, N_QKV), _f32),
            jax.ShapeDtypeStruct((BATCH, CONV_WIDTH - 1, D_CONV), _f32),
            jax.ShapeDtypeStruct(((CONV_WIDTH - 1) * DEC_BATCH, D_CONV), _f32),
        ],
        scratch_shapes=[
            pltpu.VMEM((CONV_HALO + TM, D_CONV), _f32),
            pltpu.VMEM((CONV_HALO, D_CONV), _f32),
            pltpu.VMEM((U_ROWS + TS, D_CONV), _f32),
        ],
        compiler_params=_params(("arbitrary",)),
        name="inproj",
    )(h_all, w_in_b, conv_w, state_rows, qg, kg)


QPAIR = 2 * GROUP * HEAD_DIM
HEADS_PER_STEP = 2 * GROUP


def _attn_prompt_kernel(sink_ref, q_ref, k_ref, v_ref, km_ref, vm_ref,
                        o_ref, nk_ref, nv_ref, kbuf, vbuf, bias_ref):
    gp = pl.program_id(1)

    kbuf[0:KPAD, :] = jnp.zeros((KPAD, PAIR), _bf16)
    vbuf[0:KPAD, :] = jnp.zeros((KPAD, PAIR), _bf16)
    kbuf[KPAD:QBLK, :] = km_ref[...].astype(_bf16)
    vbuf[KPAD:QBLK, :] = vm_ref[...].astype(_bf16)
    kbuf[QBLK:KBUF_ROWS, :] = k_ref[...].astype(_bf16)
    vbuf[QBLK:KBUF_ROWS, :] = v_ref[...].astype(_bf16)
    nk_ref[0] = k_ref[SEQ - WINDOW:SEQ, :]
    nv_ref[0] = v_ref[SEQ - WINDOW:SEQ, :]

    row = lax.broadcasted_iota(jnp.int32, (QBLK, 2 * QBLK), 0)
    col = lax.broadcasted_iota(jnp.int32, (QBLK, 2 * QBLK), 1)
    dist = QBLK + row - col
    in_window = (dist >= 0) & (dist <= WINDOW)
    distf = dist.astype(_f32)
    for hh in range(HEADS_PER_STEP):
        head = (gp * HEADS_PER_STEP + hh + 1).astype(_f32)
        slope = jnp.exp2(jnp.full((QBLK, 2 * QBLK), -8.0 / N_HEADS, _f32) * head)
        bias_ref[hh] = jnp.where(in_window, -slope * distf * LOG2E, NEG_INF)

    lane = lax.broadcasted_iota(jnp.int32, (QBLK, PAIR), 1)
    low_half = lane < HEAD_DIM

    def block(i, first):
        r0 = i * QBLK if first else pl.multiple_of(i * QBLK, QBLK)
        qn = q_ref[pl.ds(r0 - QBLK, QBLK), :]
        kk = kbuf[pl.ds(r0 - QBLK, 2 * QBLK), :]
        vv = vbuf[pl.ds(r0 - QBLK, 2 * QBLK), :]
        for g in range(2):
            keep = low_half if g == 0 else jnp.logical_not(low_half)
            outs = []
            for hp in range(GROUP):
                hh = g * GROUP + hp
                tile = qn[:, (hh // 2) * PAIR:(hh // 2 + 1) * PAIR]
                if hp % 2 != g:
                    tile = pltpu.roll(tile, HEAD_DIM, axis=1)
                qp = jnp.where(keep, tile, 0.0).astype(_bf16)
                s = lax.dot_general(qp, kk, (((1,), (1,)), ((), ())),
                                    preferred_element_type=_f32)
                s = s + bias_ref[hh]
                if first:
                    s = jnp.where(col < KPAD, NEG_INF, s)
                sink = sink_ref[gp * HEADS_PER_STEP + hh] * LOG2E
                m = jnp.maximum(jnp.max(s, axis=-1, keepdims=True), sink)
                e = jnp.exp2(s - m)
                denom = jnp.sum(e, axis=-1, keepdims=True) + jnp.exp2(sink - m)
                o = jnp.dot(e.astype(_bf16), vv, preferred_element_type=_f32)
                outs.append(o / denom)
            for pr in range(GROUP // 2):
                even, odd = outs[2 * pr], outs[2 * pr + 1]
                if g == 0:
                    odd = pltpu.roll(odd, HEAD_DIM, axis=1)
                else:
                    even = pltpu.roll(even, HEAD_DIM, axis=1)
                t = g * (GROUP // 2) + pr
                o_ref[pl.ds(r0 - QBLK, QBLK), t * PAIR:(t + 1) * PAIR] = (
                    jnp.where(low_half, even, odd).astype(_bf16))

    block(1, True)

    def later_block(i, carry):
        block(i, False)
        return carry

    lax.fori_loop(2, N_QBLK + 1, later_block, 0)


def _attn_prompt(qm, km, vm, qkv_s, sinks):
    grid_spec = pltpu.PrefetchScalarGridSpec(
        num_scalar_prefetch=1,
        grid=(BATCH, N_KV_HEADS // 2),
        in_specs=[
            pl.BlockSpec((SEQ, QPAIR), lambda b, p, s: (b, p)),
            pl.BlockSpec((SEQ, PAIR), lambda b, p, s: (b, p)),
            pl.BlockSpec((SEQ, PAIR), lambda b, p, s: (b, p)),
            pl.BlockSpec((N_META, PAIR), lambda b, p, s: (0, Q_DIM // PAIR + p)),
            pl.BlockSpec((N_META, PAIR), lambda b, p, s: (0, (Q_DIM + KV_DIM) // PAIR + p)),
        ],
        out_specs=[
            pl.BlockSpec((SEQ, QPAIR), lambda b, p, s: (b, p)),
            pl.BlockSpec((1, WINDOW, PAIR), lambda b, p, s: (b, 0, p)),
            pl.BlockSpec((1, WINDOW, PAIR), lambda b, p, s: (b, 0, p)),
        ],
        scratch_shapes=[
            pltpu.VMEM((KBUF_ROWS, PAIR), _bf16),
            pltpu.VMEM((KBUF_ROWS, PAIR), _bf16),
            pltpu.VMEM((HEADS_PER_STEP, QBLK, 2 * QBLK), _f32),
        ],
    )
    return pl.pallas_call(
        _attn_prompt_kernel,
        grid_spec=grid_spec,
        out_shape=[
            jax.ShapeDtypeStruct((N_MAIN, Q_DIM), _bf16),
            jax.ShapeDtypeStruct((BATCH, WINDOW, KV_DIM), _f32),
            jax.ShapeDtypeStruct((BATCH, WINDOW, KV_DIM), _f32),
        ],
        compiler_params=_params(("arbitrary", "arbitrary")),
        name="attn_prompt",
    )(sinks, qm, km, vm, qkv_s, qkv_s)


SEQ_CHUNK = 8
N_QROWS = N_HEADS * DEC_SEQ
N_KEYS = WINDOW + DEC_SEQ
N_KEYS_PAD = ((N_KEYS + 7) // 8) * 8


def _attn_sample_kernel(q_ref, kn_ref, vn_ref, ck_ref, cv_ref, sink_ref,
                        o_ref, nk_ref, nv_ref, kk_buf, vv_buf):
    q4 = q_ref[...]
    shp = (SEQ_CHUNK, N_QROWS, KV_DIM)
    row_kv = lax.broadcasted_iota(jnp.int32, shp, 1) // (GROUP * DEC_SEQ)
    lane_kv = lax.broadcasted_iota(jnp.int32, shp, 2) // HEAD_DIM
    own = row_kv == lane_kv
    qp = jnp.where(own, q4, 0.0)

    kk_buf[:, 0:WINDOW, :] = ck_ref[...]
    kk_buf[:, WINDOW:N_KEYS, :] = kn_ref[...]
    kk_buf[:, N_KEYS:N_KEYS_PAD, :] = jnp.zeros((SEQ_CHUNK, N_KEYS_PAD - N_KEYS, KV_DIM), _f32)
    vv_buf[:, 0:WINDOW, :] = cv_ref[...]
    vv_buf[:, WINDOW:N_KEYS, :] = vn_ref[...]
    vv_buf[:, N_KEYS:N_KEYS_PAD, :] = jnp.zeros((SEQ_CHUNK, N_KEYS_PAD - N_KEYS, KV_DIM), _f32)
    nk_ref[...] = kk_buf[:, DEC_SEQ:N_KEYS, :]
    nv_ref[...] = vv_buf[:, DEC_SEQ:N_KEYS, :]

    kk = kk_buf[...]
    vv = vv_buf[...]
    s = jnp.einsum('bqd,bkd->bqk', qp, kk, preferred_element_type=_f32)

    row = lax.broadcasted_iota(jnp.int32, (N_QROWS, N_KEYS_PAD), 0)
    col = lax.broadcasted_iota(jnp.int32, (N_QROWS, N_KEYS_PAD), 1)
    step = row % DEC_SEQ
    head = (row // DEC_SEQ + 1).astype(_f32)
    dist = step + WINDOW - col
    valid = (dist >= 0) & (dist <= WINDOW) & (col < N_KEYS)
    slope = jnp.exp2(head * (-8.0 / N_HEADS))
    bias = jnp.where(valid, -slope * dist.astype(_f32) * LOG2E, NEG_INF)
    s = jnp.where(valid[None], s + bias[None], NEG_INF)
    sink = sink_ref[...][None] * LOG2E
    m = jnp.maximum(jnp.max(s, axis=-1, keepdims=True), sink)
    e = jnp.exp2(s - m)
    denom = jnp.sum(e, axis=-1, keepdims=True) + jnp.exp2(sink - m)
    o = jnp.einsum('bqk,bkd->bqd', e, vv, preferred_element_type=_f32) / denom
    o = jnp.where(own, o, 0.0)
    acc = o
    for g in range(1, N_KV_HEADS):
        acc = acc + pltpu.roll(o, g * HEAD_DIM, axis=2)
    o_ref[...] = acc[:, :, 0:HEAD_DIM]


def _attn_sample(q_s, k_new, v_new, cache_k, cache_v, sink_rows):
    c = SEQ_CHUNK
    return pl.pallas_call(
        _attn_sample_kernel,
        grid=(DEC_BATCH // c,),
        in_specs=[
            pl.BlockSpec((c, N_QROWS, KV_DIM), lambda i: (i, 0, 0)),
            pl.BlockSpec((c, DEC_SEQ, KV_DIM), lambda i: (i, 0, 0)),
            pl.BlockSpec((c, DEC_SEQ, KV_DIM), lambda i: (i, 0, 0)),
            pl.BlockSpec((c, WINDOW, KV_DIM), lambda i: (i, 0, 0)),
            pl.BlockSpec((c, WINDOW, KV_DIM), lambda i: (i, 0, 0)),
            pl.BlockSpec((N_QROWS, 1), lambda i: (0, 0)),
        ],
        out_specs=[
            pl.BlockSpec((c, N_QROWS, HEAD_DIM), lambda i: (i, 0, 0)),
            pl.BlockSpec((c, WINDOW, KV_DIM), lambda i: (i, 0, 0)),
            pl.BlockSpec((c, WINDOW, KV_DIM), lambda i: (i, 0, 0)),
        ],
        out_shape=[
            jax.ShapeDtypeStruct((DEC_BATCH, N_QROWS, HEAD_DIM), _f32),
            jax.ShapeDtypeStruct((DEC_BATCH, WINDOW, KV_DIM), _f32),
            jax.ShapeDtypeStruct((DEC_BATCH, WINDOW, KV_DIM), _f32),
        ],
        scratch_shapes=[
            pltpu.VMEM((c, N_KEYS_PAD, KV_DIM), _f32),
            pltpu.VMEM((c, N_KEYS_PAD, KV_DIM), _f32),
        ],
        compiler_params=_params(("arbitrary",)),
        name="attn_sample",
    )(q_s, k_new, v_new, cache_k, cache_v, sink_rows)


def _gate_kernel(*refs):
    h_ref, cb_ref, atm_ref, ats_ref = refs[:4]
    w_refs = refs[4:4 + 4 * GATE_PIECES]
    tm_ref, ts_ref, at_all = refs[4 + 4 * GATE_PIECES:]
    at_all[0:TM, :] = atm_ref[...]
    at_all[TM:, :] = ats_ref[...]
    h = h_ref[...]
    cb = cb_ref[...]
    at = at_all[...]
    for c in range(GATE_PIECES):
        wgc_ref, wga_ref, wc_ref, wa_ref = w_refs[4 * c:4 * c + 4]
        cols = slice(c * GATE_PIECE, (c + 1) * GATE_PIECE)
        gc = jnp.dot(h, wgc_ref[...], preferred_element_type=_f32)
        ga = jnp.dot(h, wga_ref[...], preferred_element_type=_f32)
        yc = jnp.dot(cb, wc_ref[...], preferred_element_type=_f32)
        ya = jnp.dot(at, wa_ref[...], preferred_element_type=_f32)
        t = (jax.nn.sigmoid(gc) * yc + jax.nn.sigmoid(ga) * ya).astype(_bf16)
        tm_ref[:, cols] = t[0:TM, :]
        ts_ref[:, cols] = t[TM:, :]


def _gate(h_all, cb_all, atm, ats, w_in_b, wc, wa):
    row = lambda n, i: (i, 0)
    once = pl.Buffered(1)
    w_specs, w_args = [], []
    for c in range(GATE_PIECES):
        piece = lambda n, i, c=c, base=0: (0, base + n * GATE_PIECES + c)
        w_specs += [
            pl.BlockSpec((D_MODEL, GATE_PIECE),
                         functools.partial(piece, base=OFF_GC // GATE_PIECE), pipeline_mode=once),
            pl.BlockSpec((D_MODEL, GATE_PIECE),
                         functools.partial(piece, base=OFF_GA // GATE_PIECE), pipeline_mode=once),
            pl.BlockSpec((D_CONV, GATE_PIECE), piece, pipeline_mode=once),
            pl.BlockSpec((Q_DIM, GATE_PIECE), piece, pipeline_mode=once),
        ]
        w_args += [w_in_b, w_in_b, wc, wa]
    return pl.pallas_call(
        _gate_kernel,
        grid=(N_OUT_CHUNKS, N_TILES),
        in_specs=[
            pl.BlockSpec((TILE, D_MODEL), row),
            pl.BlockSpec((TILE, D_CONV), row),
            pl.BlockSpec((TM, Q_DIM), row),
            pl.BlockSpec((TS, Q_DIM), row),
        ] + w_specs,
        out_specs=[
            pl.BlockSpec((TM, TN_OUT), lambda n, i: (i, n)),
            pl.BlockSpec((TS, TN_OUT), lambda n, i: (i, n)),
        ],
        out_shape=[
            jax.ShapeDtypeStruct((N_MAIN, D_MODEL), _bf16),
            jax.ShapeDtypeStruct((N_SMALL, D_MODEL), _bf16),
        ],
        scratch_shapes=[pltpu.VMEM((TILE, Q_DIM), _bf16)],
        compiler_params=_params(("arbitrary", "arbitrary")),
        name="gate",
    )(h_all, cb_all, atm, ats, *w_args)


def _sample_rows(a):
    return a.transpose(1, 0, 2).reshape(N_SAMPLE, a.shape[-1])


def _to_small(sample_rows, dtype):
    w = sample_rows.shape[-1]
    return jnp.concatenate([
        jnp.zeros((SAMPLE_ROW0, w), dtype), sample_rows.astype(dtype),
        jnp.zeros((N_SMALL - SAMPLE_ROW0 - N_SAMPLE, w), dtype)], axis=0)


def kernel(x_prompt, x_sample, state_conv, cache_k_win, cache_v_win, meta_tokens, ffn1_norm, ffn1_w_up, ffn1_w_down, mix_norm, w_in, q_norm, k_norm, conv_w, w_conv_out, attn_sinks, w_attn_out, w_o, ffn2_norm, ffn2_w_up, ffn2_w_down):
    l = 0
    xm = x_prompt.reshape(N_MAIN, D_MODEL)
    xs = jnp.concatenate([
        meta_tokens.astype(_f32), _sample_rows(x_sample),
        jnp.zeros((N_SMALL - N_META - N_SAMPLE, D_MODEL), _f32)], axis=0)

    wgu1 = _cast(_w_up_job(256), ffn1_w_up[l])
    wd1 = _cast(_w_down_job(TF), ffn1_w_down[l])
    jobs = (_w_up_job(16), _w_down_job(32), _cast_job(D_MODEL, D_IN, 16),
            _cast_job(D_CONV, D_MODEL, 16), _cast_job(Q_DIM, D_MODEL, 16),
            _cast_job(D_MODEL, D_MODEL, 16))
    x1m, x1s, h_all, wgu2, wd2, w_in_b, wc_b, wa_b, wo_b = _ffn(
        xm, xs, ffn1_norm[l].reshape(1, D_MODEL), wgu1, wd1,
        next_gain=mix_norm[l].reshape(1, D_MODEL), jobs=jobs,
        job_inputs=(ffn2_w_up[l], ffn2_w_down[l], w_in[l], w_conv_out[l], w_attn_out[l], w_o[l]))

    state_rows = state_conv[l].transpose(1, 0, 2).reshape((CONV_WIDTH - 1) * DEC_BATCH, D_CONV)
    cb_all, qm, km, vm, qkv_s, new_conv_p, new_conv_s = _inproj(
        h_all, w_in_b, conv_w[l], state_rows, q_norm[l], k_norm[l])

    atm, new_k_p, new_v_p = _attn_prompt(qm, km, vm, qkv_s, attn_sinks[l])
    samp = qkv_s[SAMPLE_ROW0:SAMPLE_ROW0 + N_SAMPLE]
    q_s = (samp[:, 0:Q_DIM].reshape(DEC_SEQ, DEC_BATCH, N_HEADS, HEAD_DIM)
           .transpose(1, 2, 0, 3).reshape(DEC_BATCH, N_QROWS, HEAD_DIM))
    q_s = jnp.tile(q_s, (1, 1, N_KV_HEADS))
    k_new = samp[:, Q_DIM:Q_DIM + KV_DIM].reshape(DEC_SEQ, DEC_BATCH, KV_DIM).transpose(1, 0, 2)
    v_new = samp[:, Q_DIM + KV_DIM:N_QKV].reshape(DEC_SEQ, DEC_BATCH, KV_DIM).transpose(1, 0, 2)
    sink_rows = jnp.repeat(attn_sinks[l].astype(_f32), DEC_SEQ).reshape(N_QROWS, 1)
    o_s, new_k_s, new_v_s = _attn_sample(
        q_s, k_new, v_new,
        cache_k_win[l].reshape(DEC_BATCH, WINDOW, KV_DIM),
        cache_v_win[l].reshape(DEC_BATCH, WINDOW, KV_DIM), sink_rows)
    at_rows = (o_s.reshape(DEC_BATCH, N_HEADS, DEC_SEQ, HEAD_DIM)
               .transpose(2, 0, 1, 3).reshape(N_SAMPLE, Q_DIM))
    ats = _to_small(at_rows, _bf16)

    t_main, t_small = _gate(h_all, cb_all, atm, ats, w_in_b, wc_b, wa_b)
    ym, ys = _ffn(x1m, x1s, ffn2_norm[l].reshape(1, D_MODEL), wgu2, wd2,
                  proj=(t_main, t_small, wo_b))

    y_prompt = ym.reshape(BATCH, SEQ, D_MODEL)
    y_sample = (ys[SAMPLE_ROW0:SAMPLE_ROW0 + N_SAMPLE]
                .reshape(DEC_SEQ, DEC_BATCH, D_MODEL).transpose(1, 0, 2))
    kv_shape_p = (1, BATCH, WINDOW, N_KV_HEADS, HEAD_DIM)
    kv_shape_s = (1, DEC_BATCH, WINDOW, N_KV_HEADS, HEAD_DIM)
    return (y_prompt, y_sample,
            new_conv_p[None],
            new_k_p.reshape(kv_shape_p), new_v_p.reshape(kv_shape_p),
            new_conv_s.reshape(CONV_WIDTH - 1, DEC_BATCH, D_CONV).transpose(1, 0, 2)[None],
            new_k_s.reshape(kv_shape_s), new_v_s.reshape(kv_shape_s))
```

```python
import functools
from typing import NamedTuple

import jax
import jax.numpy as jnp
from jax import lax
from jax.experimental import pallas as pl
from jax.experimental.pallas import tpu as pltpu

D_MODEL = 2048
BATCH = 4
SEQ = 2048
DEC_BATCH = 32
DEC_SEQ = 4
PAST_LEN = 16384
N_META = 16
D_CONV = D_MODEL // 2
CONV_WIDTH = 3
HEAD_DIM = 64
N_HEADS = (D_MODEL // 2) // HEAD_DIM
N_KV_HEADS = N_HEADS // 4
GROUP = N_HEADS // N_KV_HEADS
Q_DIM = N_HEADS * HEAD_DIM
KV_DIM = N_KV_HEADS * HEAD_DIM
PAIR = 2 * HEAD_DIM
WINDOW = 128
D_FF = ((8 * D_MODEL // 3 + 127) // 128) * 128
D_IN = 3 * D_CONV + Q_DIM + 2 * KV_DIM + 2 * D_MODEL
EPS = 1e-6
NEG_INF = -1e30
LOG2E = 1.4426950408889634
Q_SCALE = HEAD_DIM ** -0.5 * LOG2E

OFF_XC = 0
OFF_BG = D_CONV
OFF_CG = 2 * D_CONV
OFF_Q = 3 * D_CONV
OFF_K = OFF_Q + Q_DIM
OFF_V = OFF_K + KV_DIM
OFF_GC = OFF_V + KV_DIM
OFF_GA = OFF_GC + D_MODEL
N_QKV = Q_DIM + 2 * KV_DIM

N_MAIN = BATCH * SEQ
N_SAMPLE = DEC_BATCH * DEC_SEQ
N_SMALL = 256
SAMPLE_ROW0 = N_META

N_TILES = 16
TM = N_MAIN // N_TILES
TS = N_SMALL // N_TILES
TILE = TM + TS
N_ALL = N_TILES * TILE
TILES_PER_SEQ = SEQ // TM

TF = 512
D_FF_PAD = ((D_FF + TF - 1) // TF) * TF
N_FF_CHUNKS = D_FF_PAD // TF
FFN_STEPS = N_TILES * N_FF_CHUNKS
TN_OUT = 1024
N_OUT_CHUNKS = D_MODEL // TN_OUT
GATE_PIECE = 512
GATE_PIECES = TN_OUT // GATE_PIECE

QBLK = 128
KPAD = QBLK - N_META
N_QBLK = SEQ // QBLK
KBUF_ROWS = QBLK + SEQ

VMEM_LIMIT = 56 * 1024 * 1024

_bf16 = jnp.bfloat16
_f32 = jnp.float32


def _params(sem):
    return pltpu.CompilerParams(dimension_semantics=sem, vmem_limit_bytes=VMEM_LIMIT)


def _rms_rows(x, g):
    ms = jnp.mean(x * x, axis=-1, keepdims=True)
    return x * lax.rsqrt(ms + EPS) * g


def _pair_rms(x, g_pair):
    low = lax.broadcasted_iota(jnp.int32, x.shape, 1) < HEAD_DIM
    x2 = x * x
    s_lo = jnp.sum(jnp.where(low, x2, 0.0), axis=-1, keepdims=True)
    s_hi = jnp.sum(jnp.where(low, 0.0, x2), axis=-1, keepdims=True)
    ms = jnp.where(low, s_lo, s_hi) * (1.0 / HEAD_DIM)
    return x * lax.rsqrt(ms + EPS) * g_pair


class _CastJob(NamedTuple):
    slab: int
    rows_in: int
    n_in_slabs: int
    n_out_slabs: int
    n_in_cols: int
    n_out_cols: int
    segments: tuple
    zero_ranges: tuple


def _cast_job(rows_in, cols_in, slab, segments=None, zero_ranges=(), rows_out=None, cols_out=None):
    rows_out = rows_in if rows_out is None else rows_out
    cols_out = cols_in if cols_out is None else cols_out
    segments = ((0, cols_in, 0),) if segments is None else segments
    assert rows_out % slab == 0
    return _CastJob(slab, rows_in, -(-rows_in // slab), rows_out // slab, cols_in, cols_out,
                    tuple(segments), tuple(zero_ranges))


def _job_specs(job, step_of):
    in_idx = lambda *ids: (jnp.minimum(step_of(*ids), job.n_in_slabs - 1), 0)
    out_idx = lambda *ids: (jnp.minimum(step_of(*ids), job.n_out_slabs - 1), 0)
    return (pl.BlockSpec((job.slab, job.n_in_cols), in_idx),
            pl.BlockSpec((job.slab, job.n_out_cols), out_idx))


def _run_cast_job(job, step, x_ref, o_ref):
    out_slab = jnp.minimum(step, job.n_out_slabs - 1)
    padded = job.n_out_slabs * job.slab > job.rows_in
    for src, width, dst in job.segments:
        x = x_ref[:, src:src + width]
        if padded:
            row = out_slab * job.slab + lax.broadcasted_iota(jnp.int32, x.shape, 0)
            x = jnp.where(row < job.rows_in, x, 0.0)
        o_ref[:, dst:dst + width] = x.astype(_bf16)
    for lo, hi in job.zero_ranges:
        o_ref[:, lo:hi] = jnp.zeros((job.slab, hi - lo), _bf16)


def _cast_kernel(job, x_ref, o_ref):
    _run_cast_job(job, pl.program_id(0), x_ref, o_ref)


def _cast(job, w):
    in_spec, out_spec = _job_specs(job, lambda i: i)
    return pl.pallas_call(
        functools.partial(_cast_kernel, job),
        grid=(job.n_out_slabs,),
        in_specs=[in_spec],
        out_specs=out_spec,
        out_shape=jax.ShapeDtypeStruct((job.n_out_slabs * job.slab, job.n_out_cols), _bf16),
        compiler_params=_params(("arbitrary",)),
        name="cast",
    )(w)


def _w_up_job(slab):
    segments, zero_ranges = [], []
    for c in range(N_FF_CHUNKS):
        width = min(TF, D_FF - c * TF)
        for half, src0 in enumerate((0, D_FF)):
            dst = (2 * c + half) * TF
            segments.append((src0 + c * TF, width, dst))
            if width < TF:
                zero_ranges.append((dst + width, dst + TF))
    return _cast_job(D_MODEL, 2 * D_FF, slab, segments=segments, zero_ranges=zero_ranges,
                     cols_out=2 * D_FF_PAD)


def _w_down_job(slab):
    return _cast_job(D_FF, D_MODEL, slab, rows_out=D_FF_PAD)


def _ffn_kernel(jobs, with_proj, with_norm_out, *refs):
    refs = list(refs)
    take = lambda n: [refs.pop(0) for _ in range(n)]
    xm_ref, xs_ref = take(2)
    if with_proj:
        tm_ref, ts_ref, wo_ref = take(3)
    g_ref, wgu_ref, wd_ref = take(3)
    if with_norm_out:
        gn_ref, = take(1)
    job_in = take(len(jobs))
    om_ref, os_ref = take(2)
    if with_norm_out:
        hn_ref, = take(1)
    job_out = take(len(jobs))
    h_ref = refs[0]
    i = pl.program_id(0)
    j = pl.program_id(1)

    @pl.when(j == 0)
    def _():
        g = g_ref[...]
        xm = xm_ref[...]
        xs = xs_ref[...]
        if with_proj:
            t_all = refs[1]
            t_all[0:TM, :] = tm_ref[...]
            t_all[TM:, :] = ts_ref[...]
            p = jnp.dot(t_all[...], wo_ref[...], preferred_element_type=_f32)
            xm = xm + p[0:TM, :]
            xs = xs + p[TM:, :]
        h_ref[0:TM, :] = _rms_rows(xm, g).astype(_bf16)
        om_ref[...] = xm
        h_ref[TM:, :] = _rms_rows(xs, g).astype(_bf16)
        os_ref[...] = xs

    for job, x_ref, o_ref in zip(jobs, job_in, job_out):
        _run_cast_job(job, i * N_FF_CHUNKS + j, x_ref, o_ref)

    h = h_ref[...]
    gu = jnp.dot(h, wgu_ref[...], preferred_element_type=_f32)
    gate, up = gu[:, 0:TF], gu[:, TF:2 * TF]
    a = (gate * jax.nn.sigmoid(gate) * up * 0.5).astype(_bf16)
    r = jnp.dot(a, wd_ref[...], preferred_element_type=_f32)
    om_ref[...] += r[0:TM, :]
    os_ref[...] += r[TM:, :]

    if with_norm_out:
        @pl.when(j == N_FF_CHUNKS - 1)
        def _():
            gn = gn_ref[...]
            hn_ref[0:TM, :] = _rms_rows(om_ref[...], gn).astype(_bf16)
            hn_ref[TM:, :] = _rms_rows(os_ref[...], gn).astype(_bf16)


def _ffn(xm, xs, g, wgu, wd, proj=None, next_gain=None, jobs=(), job_inputs=()):
    assert all(job.n_out_slabs <= FFN_STEPS for job in jobs)
    job_specs = [_job_specs(job, lambda i, j: i * N_FF_CHUNKS + j) for job in jobs]
    row = lambda i, j: (i, 0)
    fixed = lambda i, j: (0, 0)
    in_specs = [pl.BlockSpec((TM, D_MODEL), row), pl.BlockSpec((TS, D_MODEL), row)]
    out_specs = [pl.BlockSpec((TM, D_MODEL), row), pl.BlockSpec((TS, D_MODEL), row)]
    out_shape = [jax.ShapeDtypeStruct((N_MAIN, D_MODEL), _f32),
                 jax.ShapeDtypeStruct((N_SMALL, D_MODEL), _f32)]
    scratch = [pltpu.VMEM((TILE, D_MODEL), _bf16)]
    args = [xm, xs]
    if proj is not None:
        in_specs += [pl.BlockSpec((TM, D_MODEL), row), pl.BlockSpec((TS, D_MODEL), row),
                     pl.BlockSpec((D_MODEL, D_MODEL), fixed, pipeline_mode=pl.Buffered(1))]
        scratch += [pltpu.VMEM((TILE, D_MODEL), _bf16)]
        args += list(proj)
    in_specs += [
        pl.BlockSpec((1, D_MODEL), fixed),
        pl.BlockSpec((D_MODEL, 2 * TF), lambda i, j: (0, j)),
        pl.BlockSpec((TF, D_MODEL), lambda i, j: (j, 0)),
    ]
    args += [g, wgu, wd]
    if next_gain is not None:
        in_specs += [pl.BlockSpec((1, D_MODEL), fixed)]
        args += [next_gain]
        out_specs += [pl.BlockSpec((TILE, D_MODEL), row)]
        out_shape += [jax.ShapeDtypeStruct((N_ALL, D_MODEL), _bf16)]
    in_specs += [s[0] for s in job_specs]
    args += list(job_inputs)
    out_specs += [s[1] for s in job_specs]
    out_shape += [jax.ShapeDtypeStruct((job.n_out_slabs * job.slab, job.n_out_cols), _bf16)
                  for job in jobs]
    return pl.pallas_call(
        functools.partial(_ffn_kernel, tuple(jobs), proj is not None, next_gain is not None),
        grid=(N_TILES, N_FF_CHUNKS),
        in_specs=in_specs,
        out_specs=out_specs,
        out_shape=out_shape,
        scratch_shapes=scratch,
        compiler_params=_params(("arbitrary", "arbitrary")),
        name="ffn",
    )(*args)


def _ffn_stream_kernel(xm_ref, xs_ref, tm_ref, ts_ref, wo_ref, g_ref, wgu_hbm, wd_hbm,
                       om_ref, os_ref, h_ref, t_all, wgu_buf, wd_buf, sem):
    i = pl.program_id(0)

    def chunk_copies(chunk, slot):
        return (
            pltpu.make_async_copy(wgu_hbm.at[:, pl.ds(chunk * 2 * TF, 2 * TF)],
                                  wgu_buf.at[slot], sem.at[0, slot]),
            pltpu.make_async_copy(wd_hbm.at[pl.ds(chunk * TF, TF), :],
                                  wd_buf.at[slot], sem.at[1, slot]),
        )

    def start(chunk, slot):
        for c in chunk_copies(chunk, slot):
            c.start()

    @pl.when(i == 0)
    def _():
        start(0, 0)

    g = g_ref[...]
    t_all[0:TM, :] = tm_ref[...]
    t_all[TM:, :] = ts_ref[...]
    p = jnp.dot(t_all[...], wo_ref[...], preferred_element_type=_f32)
    xm = xm_ref[...] + p[0:TM, :]
    xs = xs_ref[...] + p[TM:, :]
    h_ref[0:TM, :] = _rms_rows(xm, g).astype(_bf16)
    om_ref[...] = xm
    h_ref[TM:, :] = _rms_rows(xs, g).astype(_bf16)
    os_ref[...] = xs

    h = h_ref[...]
    for j in range(N_FF_CHUNKS):
        slot = (i * N_FF_CHUNKS + j) % 2
        if j + 1 < N_FF_CHUNKS:
            start(j + 1, 1 - slot)
        else:
            @pl.when(i + 1 < N_TILES)
            def _():
                start(0, 1 - slot)
        for c in chunk_copies(j, slot):
            c.wait()
        gu = jnp.dot(h, wgu_buf[slot], preferred_element_type=_f32)
        gate, up = gu[:, 0:TF], gu[:, TF:2 * TF]
        a = (gate * jax.nn.sigmoid(gate) * up * 0.5).astype(_bf16)
        r = jnp.dot(a, wd_buf[slot], preferred_element_type=_f32)
        om_ref[...] += r[0:TM, :]
        os_ref[...] += r[TM:, :]


def _ffn_stream(xm, xs, g, wgu, wd, proj):
    row = lambda i: (i, 0)
    fixed = lambda i: (0, 0)
    return pl.pallas_call(
        _ffn_stream_kernel,
        grid=(N_TILES,),
        in_specs=[
            pl.BlockSpec((TM, D_MODEL), row),
            pl.BlockSpec((TS, D_MODEL), row),
            pl.BlockSpec((TM, D_MODEL), row),
            pl.BlockSpec((TS, D_MODEL), row),
            pl.BlockSpec((D_MODEL, D_MODEL), fixed, pipeline_mode=pl.Buffered(1)),
            pl.BlockSpec((1, D_MODEL), fixed),
            pl.BlockSpec(memory_space=pl.ANY),
            pl.BlockSpec(memory_space=pl.ANY),
        ],
        out_specs=[pl.BlockSpec((TM, D_MODEL), row), pl.BlockSpec((TS, D_MODEL), row)],
        out_shape=[jax.ShapeDtypeStruct((N_MAIN, D_MODEL), _f32),
                   jax.ShapeDtypeStruct((N_SMALL, D_MODEL), _f32)],
        scratch_shapes=[
            pltpu.VMEM((TILE, D_MODEL), _bf16),
            pltpu.VMEM((TILE, D_MODEL), _bf16),
            pltpu.VMEM((2, D_MODEL, 2 * TF), _bf16),
            pltpu.VMEM((2, TF, D_MODEL), _bf16),
            pltpu.SemaphoreType.DMA((2, 2)),
        ],
        compiler_params=_params(("arbitrary",)),
        name="ffn_stream",
    )(xm, xs, *proj, g, wgu, wd)


N_BRANCH_IN = OFF_GC
CONV_HALO = 8
U_BASE = 112
U_STATE0 = U_BASE + SAMPLE_ROW0 - (CONV_WIDTH - 1) * DEC_BATCH
U_ROWS = U_BASE + N_SMALL
U_LAST0 = U_BASE + SAMPLE_ROW0 + (DEC_SEQ - (CONV_WIDTH - 1)) * DEC_BATCH
assert U_BASE >= (CONV_WIDTH - 1) * DEC_BATCH and U_BASE % TS == 0 and TS == N_META


def _inproj_kernel(h_ref, w_ref, cw_ref, st_ref, qg_ref, kg_ref,
                   cb_ref, q_ref, k_ref, v_ref, sm_ref, ncp_ref, ncs_ref,
                   ubuf, meta_halo, utab):
    i = pl.program_id(0)

    @pl.when(i == 0)
    def _():
        utab[...] = jnp.zeros(utab.shape, _f32)
        utab[U_STATE0:U_STATE0 + (CONV_WIDTH - 1) * DEC_BATCH, :] = st_ref[...]
        meta_halo[...] = jnp.zeros(meta_halo.shape, _f32)
        ubuf[0:CONV_HALO, :] = jnp.zeros((CONV_HALO, D_CONV), _f32)

    h = h_ref[...]
    z_q = jnp.dot(h, w_ref[:, OFF_Q:OFF_K], preferred_element_type=_f32)
    z_kv = jnp.dot(h, w_ref[:, OFF_K:OFF_GC], preferred_element_type=_f32)
    z = jnp.dot(h, w_ref[:, 0:OFF_Q], preferred_element_type=_f32)
    cw = cw_ref[...]
    w0, w1, w2 = cw[0:1, :], cw[1:2, :], cw[2:3, :]
    qg = qg_ref[...]
    kg = kg_ref[...]
    seg = lambda a, off, width: a[:, off:off + width]

    def qkv(rows, q_out, k_out, v_out):
        for t in range(Q_DIM // PAIR):
            q_out(t, _pair_rms(z_q[rows, t * PAIR:(t + 1) * PAIR], qg) * Q_SCALE)
        for t in range(KV_DIM // PAIR):
            k_out(t, _pair_rms(z_kv[rows, t * PAIR:(t + 1) * PAIR], kg))
            v_out(t, z_kv[rows, KV_DIM + t * PAIR:KV_DIM + (t + 1) * PAIR])

    zs = z[TM:, :]
    u_s = seg(zs, OFF_CG, D_CONV) * seg(zs, OFF_XC, D_CONV)
    cur = pl.multiple_of(U_BASE + TS * i, TS)
    slot = pl.multiple_of(jnp.where(i == 0, U_ROWS, cur), TS)
    utab[pl.ds(slot, TS), :] = u_s
    y_s = (w0 * utab[pl.ds(cur - 2 * DEC_BATCH, TS), :]
           + w1 * utab[pl.ds(cur - DEC_BATCH, TS), :] + w2 * u_s)
    cb_ref[TM:, :] = (seg(zs, OFF_BG, D_CONV) * y_s).astype(_bf16)
    meta_halo[...] = jnp.where(i == 0, u_s[N_META - CONV_HALO:N_META, :], meta_halo[...])

    def sm_store(off):
        def store(t, val):
            sm_ref[:, off + t * PAIR:off + (t + 1) * PAIR] = val
        return store
    qkv(slice(TM, TILE), sm_store(0), sm_store(Q_DIM), sm_store(Q_DIM + KV_DIM))

    ncs_ref[...] = utab[U_LAST0:U_LAST0 + (CONV_WIDTH - 1) * DEC_BATCH, :]

    ubuf[0:CONV_HALO, :] = jnp.where(i % TILES_PER_SEQ == 0, meta_halo[...], ubuf[0:CONV_HALO, :])

    zm = z[0:TM, :]
    u = seg(zm, OFF_CG, D_CONV) * seg(zm, OFF_XC, D_CONV)
    ubuf[CONV_HALO:CONV_HALO + TM, :] = u
    u1 = ubuf[CONV_HALO - 1:CONV_HALO - 1 + TM, :]
    u2 = ubuf[CONV_HALO - 2:CONV_HALO - 2 + TM, :]
    cb_ref[0:TM, :] = (seg(zm, OFF_BG, D_CONV) * (w0 * u2 + w1 * u1 + w2 * u)).astype(_bf16)
    tail = ubuf[TM:TM + CONV_HALO, :]
    ubuf[0:CONV_HALO, :] = tail
    ncp_ref[0] = tail[CONV_HALO - (CONV_WIDTH - 1):CONV_HALO, :]

    def lane_store(ref):
        def store(t, val):
            ref[:, t * PAIR:(t + 1) * PAIR] = val
        return store
    qkv(slice(0, TM), lane_store(q_ref), lane_store(k_ref), lane_store(v_ref))


def _inproj(h_all, w_in_b, conv_w, state_rows, q_gain, k_gain):
    qg = jnp.tile(q_gain.reshape(1, HEAD_DIM), (1, 2))
    kg = jnp.tile(k_gain.reshape(1, HEAD_DIM), (1, 2))
    row = lambda i: (i, 0)
    fixed = lambda i: (0, 0)
    return pl.pallas_call(
        _inproj_kernel,
        grid=(N_TILES,),
        in_specs=[
            pl.BlockSpec((TILE, D_MODEL), row),
            pl.BlockSpec((D_MODEL, N_BRANCH_IN), fixed, pipeline_mode=pl.Buffered(1)),
            pl.BlockSpec((CONV_WIDTH, D_CONV), fixed),
            pl.BlockSpec(((CONV_WIDTH - 1) * DEC_BATCH, D_CONV), fixed),
            pl.BlockSpec((1, PAIR), fixed),
            pl.BlockSpec((1, PAIR), fixed),
        ],
        out_specs=[
            pl.BlockSpec((TILE, D_CONV), row),
            pl.BlockSpec((TM, Q_DIM), row),
            pl.BlockSpec((TM, KV_DIM), row),
            pl.BlockSpec((TM, KV_DIM), row),
            pl.BlockSpec((TS, N_QKV), row),
            pl.BlockSpec((1, CONV_WIDTH - 1, D_CONV), lambda i: (i // TILES_PER_SEQ, 0, 0)),
            pl.BlockSpec(((CONV_WIDTH - 1) * DEC_BATCH, D_CONV), fixed),
        ],
        out_shape=[
            jax.ShapeDtypeStruct((N_ALL, D_CONV), _bf16),
            jax.ShapeDtypeStruct((N_MAIN, Q_DIM), _f32),
            jax.ShapeDtypeStruct((N_MAIN, KV_DIM), _f32),
            jax.ShapeDtypeStruct((N_MAIN, KV_DIM), _f32),
            jax.ShapeDtypeStruct((N_SMALL, N_QKV), _f32),
            jax.ShapeDtypeStruct((BATCH, CONV_WIDTH - 1, D_CONV), _f32),
            jax.ShapeDtypeStruct(((CONV_WIDTH - 1) * DEC_BATCH, D_CONV), _f32),
        ],
        scratch_shapes=[
            pltpu.VMEM((CONV_HALO + TM, D_CONV), _f32),
            pltpu.VMEM((CONV_HALO, D_CONV), _f32),
            pltpu.VMEM((U_ROWS + TS, D_CONV), _f32),
        ],
        compiler_params=_params(("arbitrary",)),
        name="inproj",
    )(h_all, w_in_b, conv_w, state_rows, qg, kg)


QPAIR = 2 * GROUP * HEAD_DIM
HEADS_PER_STEP = 2 * GROUP
BLOCK_UNROLL = 1
assert (N_QBLK - 1) % BLOCK_UNROLL == 0


def _attn_prompt_kernel(sink_ref, q_ref, k_ref, v_ref, km_ref, vm_ref,
                        o_ref, nk_ref, nv_ref, kbuf, vbuf, bias_ref):
    gp = pl.program_id(1)

    kbuf[0:KPAD, :] = jnp.zeros((KPAD, PAIR), _bf16)
    vbuf[0:KPAD, :] = jnp.zeros((KPAD, PAIR), _bf16)
    kbuf[KPAD:QBLK, :] = km_ref[...].astype(_bf16)
    vbuf[KPAD:QBLK, :] = vm_ref[...].astype(_bf16)
    kbuf[QBLK:KBUF_ROWS, :] = k_ref[...].astype(_bf16)
    vbuf[QBLK:KBUF_ROWS, :] = v_ref[...].astype(_bf16)
    nk_ref[0] = k_ref[SEQ - WINDOW:SEQ, :]
    nv_ref[0] = v_ref[SEQ - WINDOW:SEQ, :]

    row = lax.broadcasted_iota(jnp.int32, (QBLK, 2 * QBLK), 0)
    col = lax.broadcasted_iota(jnp.int32, (QBLK, 2 * QBLK), 1)
    dist = QBLK + row - col
    in_window = (dist >= 0) & (dist <= WINDOW)
    distf = dist.astype(_f32)
    for g in range(2):
        for hp in range(GROUP):
            head = gp * HEADS_PER_STEP + g * GROUP + hp
            slope = jnp.exp2(jnp.full((QBLK, 2 * QBLK), -8.0 / N_HEADS, _f32)
                             * (head + 1).astype(_f32))
            bias_ref[g, hp * QBLK:(hp + 1) * QBLK, :] = jnp.where(
                in_window, -slope * distf * LOG2E, NEG_INF)

    lane = lax.broadcasted_iota(jnp.int32, (QBLK, PAIR), 1)
    low_half = lane < HEAD_DIM

    def block(i, first):
        r0 = i * QBLK if first else pl.multiple_of(i * QBLK, QBLK)
        qn = q_ref[pl.ds(r0 - QBLK, QBLK), :]
        kk = kbuf[pl.ds(r0 - QBLK, 2 * QBLK), :]
        vv = vbuf[pl.ds(r0 - QBLK, 2 * QBLK), :]
        for g in range(2):
            keep = low_half if g == 0 else jnp.logical_not(low_half)
            outs = []
            for hp in range(GROUP):
                hh = g * GROUP + hp
                tile = qn[:, (hh // 2) * PAIR:(hh // 2 + 1) * PAIR]
                if hp % 2 != g:
                    tile = pltpu.roll(tile, HEAD_DIM, axis=1)
                qp = jnp.where(keep, tile, 0.0).astype(_bf16)
                s = lax.dot_general(qp, kk, (((1,), (1,)), ((), ())),
                                    preferred_element_type=_f32)
                s = s + bias_ref[g, hp * QBLK:(hp + 1) * QBLK, :]
                if first:
                    s = jnp.where(col < KPAD, NEG_INF, s)
                sink = sink_ref[gp * HEADS_PER_STEP + hh] * LOG2E
                m = jnp.maximum(jnp.max(s, axis=-1, keepdims=True), sink)
                e = jnp.exp2(s - m)
                denom = jnp.sum(e, axis=-1, keepdims=True) + jnp.exp2(sink - m)
                o = jnp.dot(e.astype(_bf16), vv, preferred_element_type=_f32)
                outs.append(o / denom)
            for pr in range(GROUP // 2):
                even, odd = outs[2 * pr], outs[2 * pr + 1]
                if g == 0:
                    odd = pltpu.roll(odd, HEAD_DIM, axis=1)
                else:
                    even = pltpu.roll(even, HEAD_DIM, axis=1)
                t = g * (GROUP // 2) + pr
                o_ref[pl.ds(r0 - QBLK, QBLK), t * PAIR:(t + 1) * PAIR] = (
                    jnp.where(low_half, even, odd).astype(_bf16))

    block(1, True)

    def later_block(i, carry):
        block(i, False)
        return carry

    lax.fori_loop(2, N_QBLK + 1, later_block, 0, unroll=BLOCK_UNROLL)


def _attn_prompt(qm, km, vm, qkv_s, sinks):
    grid_spec = pltpu.PrefetchScalarGridSpec(
        num_scalar_prefetch=1,
        grid=(BATCH, N_KV_HEADS // 2),
        in_specs=[
            pl.BlockSpec((SEQ, QPAIR), lambda b, p, s: (b, p)),
            pl.BlockSpec((SEQ, PAIR), lambda b, p, s: (b, p)),
            pl.BlockSpec((SEQ, PAIR), lambda b, p, s: (b, p)),
            pl.BlockSpec((N_META, PAIR), lambda b, p, s: (0, Q_DIM // PAIR + p)),
            pl.BlockSpec((N_META, PAIR), lambda b, p, s: (0, (Q_DIM + KV_DIM) // PAIR + p)),
        ],
        out_specs=[
            pl.BlockSpec((SEQ, QPAIR), lambda b, p, s: (b, p)),
            pl.BlockSpec((1, WINDOW, PAIR), lambda b, p, s: (b, 0, p)),
            pl.BlockSpec((1, WINDOW, PAIR), lambda b, p, s: (b, 0, p)),
        ],
        scratch_shapes=[
            pltpu.VMEM((KBUF_ROWS, PAIR), _bf16),
            pltpu.VMEM((KBUF_ROWS, PAIR), _bf16),
            pltpu.VMEM((2, GROUP * QBLK, 2 * QBLK), _f32),
        ],
    )
    return pl.pallas_call(
        _attn_prompt_kernel,
        grid_spec=grid_spec,
        out_shape=[
            jax.ShapeDtypeStruct((N_MAIN, Q_DIM), _bf16),
            jax.ShapeDtypeStruct((BATCH, WINDOW, KV_DIM), _f32),
            jax.ShapeDtypeStruct((BATCH, WINDOW, KV_DIM), _f32),
        ],
        compiler_params=_params(("arbitrary", "arbitrary")),
        name="attn_prompt",
    )(sinks, qm, km, vm, qkv_s, qkv_s)


SEQ_CHUNK = 8
N_QROWS = N_HEADS * DEC_SEQ
N_KEYS = WINDOW + DEC_SEQ
N_KEYS_PAD = ((N_KEYS + 7) // 8) * 8


def _attn_sample_kernel(q_ref, kn_ref, vn_ref, ck_ref, cv_ref, sink_ref,
                        o_ref, nk_ref, nv_ref, kk_buf, vv_buf):
    q4 = q_ref[...]
    shp = (SEQ_CHUNK, N_QROWS, KV_DIM)
    row_kv = lax.broadcasted_iota(jnp.int32, shp, 1) // (GROUP * DEC_SEQ)
    lane_kv = lax.broadcasted_iota(jnp.int32, shp, 2) // HEAD_DIM
    own = row_kv == lane_kv
    qp = jnp.where(own, q4, 0.0)

    kk_buf[:, 0:WINDOW, :] = ck_ref[...]
    kk_buf[:, WINDOW:N_KEYS, :] = kn_ref[...]
    kk_buf[:, N_KEYS:N_KEYS_PAD, :] = jnp.zeros((SEQ_CHUNK, N_KEYS_PAD - N_KEYS, KV_DIM), _f32)
    vv_buf[:, 0:WINDOW, :] = cv_ref[...]
    vv_buf[:, WINDOW:N_KEYS, :] = vn_ref[...]
    vv_buf[:, N_KEYS:N_KEYS_PAD, :] = jnp.zeros((SEQ_CHUNK, N_KEYS_PAD - N_KEYS, KV_DIM), _f32)
    nk_ref[...] = kk_buf[:, DEC_SEQ:N_KEYS, :]
    nv_ref[...] = vv_buf[:, DEC_SEQ:N_KEYS, :]

    kk = kk_buf[...]
    vv = vv_buf[...]
    s = jnp.einsum('bqd,bkd->bqk', qp, kk, preferred_element_type=_f32)

    row = lax.broadcasted_iota(jnp.int32, (N_QROWS, N_KEYS_PAD), 0)
    col = lax.broadcasted_iota(jnp.int32, (N_QROWS, N_KEYS_PAD), 1)
    step = row % DEC_SEQ
    head = (row // DEC_SEQ + 1).astype(_f32)
    dist = step + WINDOW - col
    valid = (dist >= 0) & (dist <= WINDOW) & (col < N_KEYS)
    slope = jnp.exp2(head * (-8.0 / N_HEADS))
    bias = jnp.where(valid, -slope * dist.astype(_f32) * LOG2E, NEG_INF)
    s = jnp.where(valid[None], s + bias[None], NEG_INF)
    sink = sink_ref[...][None] * LOG2E
    m = jnp.maximum(jnp.max(s, axis=-1, keepdims=True), sink)
    e = jnp.exp2(s - m)
    denom = jnp.sum(e, axis=-1, keepdims=True) + jnp.exp2(sink - m)
    o = jnp.einsum('bqk,bkd->bqd', e, vv, preferred_element_type=_f32) / denom
    o = jnp.where(own, o, 0.0)
    acc = o
    for g in range(1, N_KV_HEADS):
        acc = acc + pltpu.roll(o, g * HEAD_DIM, axis=2)
    o_ref[...] = acc[:, :, 0:HEAD_DIM]


def _attn_sample(q_s, k_new, v_new, cache_k, cache_v, sink_rows):
    c = SEQ_CHUNK
    return pl.pallas_call(
        _attn_sample_kernel,
        grid=(DEC_BATCH // c,),
        in_specs=[
            pl.BlockSpec((c, N_QROWS, KV_DIM), lambda i: (i, 0, 0)),
            pl.BlockSpec((c, DEC_SEQ, KV_DIM), lambda i: (i, 0, 0)),
            pl.BlockSpec((c, DEC_SEQ, KV_DIM), lambda i: (i, 0, 0)),
            pl.BlockSpec((c, WINDOW, KV_DIM), lambda i: (i, 0, 0)),
            pl.BlockSpec((c, WINDOW, KV_DIM), lambda i: (i, 0, 0)),
            pl.BlockSpec((N_QROWS, 1), lambda i: (0, 0)),
        ],
        out_specs=[
            pl.BlockSpec((c, N_QROWS, HEAD_DIM), lambda i: (i, 0, 0)),
            pl.BlockSpec((c, WINDOW, KV_DIM), lambda i: (i, 0, 0)),
            pl.BlockSpec((c, WINDOW, KV_DIM), lambda i: (i, 0, 0)),
        ],
        out_shape=[
            jax.ShapeDtypeStruct((DEC_BATCH, N_QROWS, HEAD_DIM), _f32),
            jax.ShapeDtypeStruct((DEC_BATCH, WINDOW, KV_DIM), _f32),
            jax.ShapeDtypeStruct((DEC_BATCH, WINDOW, KV_DIM), _f32),
        ],
        scratch_shapes=[
            pltpu.VMEM((c, N_KEYS_PAD, KV_DIM), _f32),
            pltpu.VMEM((c, N_KEYS_PAD, KV_DIM), _f32),
        ],
        compiler_params=_params(("arbitrary",)),
        name="attn_sample",
    )(q_s, k_new, v_new, cache_k, cache_v, sink_rows)


def _gate_kernel(*refs):
    h_ref, cb_ref, atm_ref, ats_ref = refs[:4]
    w_refs = refs[4:4 + 4 * GATE_PIECES]
    tm_ref, ts_ref, at_all = refs[4 + 4 * GATE_PIECES:]
    at_all[0:TM, :] = atm_ref[...]
    at_all[TM:, :] = ats_ref[...]
    h = h_ref[...]
    cb = cb_ref[...]
    at = at_all[...]
    for c in range(GATE_PIECES):
        wgc_ref, wga_ref, wc_ref, wa_ref = w_refs[4 * c:4 * c + 4]
        cols = slice(c * GATE_PIECE, (c + 1) * GATE_PIECE)
        gc = jnp.dot(h, wgc_ref[...], preferred_element_type=_f32)
        ga = jnp.dot(h, wga_ref[...], preferred_element_type=_f32)
        yc = jnp.dot(cb, wc_ref[...], preferred_element_type=_f32)
        ya = jnp.dot(at, wa_ref[...], preferred_element_type=_f32)
        t = (jax.nn.sigmoid(gc) * yc + jax.nn.sigmoid(ga) * ya).astype(_bf16)
        tm_ref[:, cols] = t[0:TM, :]
        ts_ref[:, cols] = t[TM:, :]


def _gate(h_all, cb_all, atm, ats, w_in_b, wc, wa):
    row = lambda n, i: (i, 0)
    once = pl.Buffered(1)
    w_specs, w_args = [], []
    for c in range(GATE_PIECES):
        piece = lambda n, i, c=c, base=0: (0, base + n * GATE_PIECES + c)
        w_specs += [
            pl.BlockSpec((D_MODEL, GATE_PIECE),
                         functools.partial(piece, base=OFF_GC // GATE_PIECE), pipeline_mode=once),
            pl.BlockSpec((D_MODEL, GATE_PIECE),
                         functools.partial(piece, base=OFF_GA // GATE_PIECE), pipeline_mode=once),
            pl.BlockSpec((D_CONV, GATE_PIECE), piece, pipeline_mode=once),
            pl.BlockSpec((Q_DIM, GATE_PIECE), piece, pipeline_mode=once),
        ]
        w_args += [w_in_b, w_in_b, wc, wa]
    return pl.pallas_call(
        _gate_kernel,
        grid=(N_OUT_CHUNKS, N_TILES),
        in_specs=[
            pl.BlockSpec((TILE, D_MODEL), row),
            pl.BlockSpec((TILE, D_CONV), row),
            pl.BlockSpec((TM, Q_DIM), row),
            pl.BlockSpec((TS, Q_DIM), row),
        ] + w_specs,
        out_specs=[
            pl.BlockSpec((TM, TN_OUT), lambda n, i: (i, n)),
            pl.BlockSpec((TS, TN_OUT), lambda n, i: (i, n)),
        ],
        out_shape=[
            jax.ShapeDtypeStruct((N_MAIN, D_MODEL), _bf16),
            jax.ShapeDtypeStruct((N_SMALL, D_MODEL), _bf16),
        ],
        scratch_shapes=[pltpu.VMEM((TILE, Q_DIM), _bf16)],
        compiler_params=_params(("arbitrary", "arbitrary")),
        name="gate",
    )(h_all, cb_all, atm, ats, *w_args)


def _sample_rows(a):
    return a.transpose(1, 0, 2).reshape(N_SAMPLE, a.shape[-1])


def _to_small(sample_rows, dtype):
    w = sample_rows.shape[-1]
    return jnp.concatenate([
        jnp.zeros((SAMPLE_ROW0, w), dtype), sample_rows.astype(dtype),
        jnp.zeros((N_SMALL - SAMPLE_ROW0 - N_SAMPLE, w), dtype)], axis=0)


def kernel(x_prompt, x_sample, state_conv, cache_k_win, cache_v_win, meta_tokens, ffn1_norm, ffn1_w_up, ffn1_w_down, mix_norm, w_in, q_norm, k_norm, conv_w, w_conv_out, attn_sinks, w_attn_out, w_o, ffn2_norm, ffn2_w_up, ffn2_w_down):
    l = 0
    xm = x_prompt.reshape(N_MAIN, D_MODEL)
    xs = jnp.concatenate([
        meta_tokens.astype(_f32), _sample_rows(x_sample),
        jnp.zeros((N_SMALL - N_META - N_SAMPLE, D_MODEL), _f32)], axis=0)

    wgu1 = _cast(_w_up_job(256), ffn1_w_up[l])
    wd1 = _cast(_w_down_job(TF), ffn1_w_down[l])
    jobs = (_w_up_job(16), _w_down_job(32), _cast_job(D_MODEL, D_IN, 16),
            _cast_job(D_CONV, D_MODEL, 16), _cast_job(Q_DIM, D_MODEL, 16),
            _cast_job(D_MODEL, D_MODEL, 16))
    x1m, x1s, h_all, wgu2, wd2, w_in_b, wc_b, wa_b, wo_b = _ffn(
        xm, xs, ffn1_norm[l].reshape(1, D_MODEL), wgu1, wd1,
        next_gain=mix_norm[l].reshape(1, D_MODEL), jobs=jobs,
        job_inputs=(ffn2_w_up[l], ffn2_w_down[l], w_in[l], w_conv_out[l], w_attn_out[l], w_o[l]))

    state_rows = state_conv[l].transpose(1, 0, 2).reshape((CONV_WIDTH - 1) * DEC_BATCH, D_CONV)
    cb_all, qm, km, vm, qkv_s, new_conv_p, new_conv_s = _inproj(
        h_all, w_in_b, conv_w[l], state_rows, q_norm[l], k_norm[l])

    atm, new_k_p, new_v_p = _attn_prompt(qm, km, vm, qkv_s, attn_sinks[l])
    samp = qkv_s[SAMPLE_ROW0:SAMPLE_ROW0 + N_SAMPLE]
    q_s = (samp[:, 0:Q_DIM].reshape(DEC_SEQ, DEC_BATCH, N_HEADS, HEAD_DIM)
           .transpose(1, 2, 0, 3).reshape(DEC_BATCH, N_QROWS, HEAD_DIM))
    q_s = jnp.tile(q_s, (1, 1, N_KV_HEADS))
    k_new = samp[:, Q_DIM:Q_DIM + KV_DIM].reshape(DEC_SEQ, DEC_BATCH, KV_DIM).transpose(1, 0, 2)
    v_new = samp[:, Q_DIM + KV_DIM:N_QKV].reshape(DEC_SEQ, DEC_BATCH, KV_DIM).transpose(1, 0, 2)
    sink_rows = jnp.repeat(attn_sinks[l].astype(_f32), DEC_SEQ).reshape(N_QROWS, 1)
    o_s, new_k_s, new_v_s = _attn_sample(
        q_s, k_new, v_new,
        cache_k_win[l].reshape(DEC_BATCH, WINDOW, KV_DIM),
        cache_v_win[l].reshape(DEC_BATCH, WINDOW, KV_DIM), sink_rows)
    at_rows = (o_s.reshape(DEC_BATCH, N_HEADS, DEC_SEQ, HEAD_DIM)
               .transpose(2, 0, 1, 3).reshape(N_SAMPLE, Q_DIM))
    ats = _to_small(at_rows, _bf16)

    t_main, t_small = _gate(h_all, cb_all, atm, ats, w_in_b, wc_b, wa_b)
    ym, ys = _ffn_stream(x1m, x1s, ffn2_norm[l].reshape(1, D_MODEL), wgu2, wd2,
                         proj=(t_main, t_small, wo_b))

    y_prompt = ym.reshape(BATCH, SEQ, D_MODEL)
    y_sample = (ys[SAMPLE_ROW0:SAMPLE_ROW0 + N_SAMPLE]
                .reshape(DEC_SEQ, DEC_BATCH, D_MODEL).transpose(1, 0, 2))
    kv_shape_p = (1, BATCH, WINDOW, N_KV_HEADS, HEAD_DIM)
    kv_shape_s = (1, DEC_BATCH, WINDOW, N_KV_HEADS, HEAD_DIM)
    return (y_prompt, y_sample,
            new_conv_p[None],
            new_k_p.reshape(kv_shape_p), new_v_p.reshape(kv_shape_p),
            new_conv_s.reshape(CONV_WIDTH - 1, DEC_BATCH, D_CONV).transpose(1, 0, 2)[None],
            new_k_s.reshape(kv_shape_s), new_v_s.reshape(kv_shape_s))
```

```python
import functools
from typing import NamedTuple

import jax
import jax.numpy as jnp
from jax import lax
from jax.experimental import pallas as pl
from jax.experimental.pallas import tpu as pltpu

D_MODEL = 2048
BATCH = 4
SEQ = 2048
DEC_BATCH = 32
DEC_SEQ = 4
PAST_LEN = 16384
N_META = 16
D_CONV = D_MODEL // 2
CONV_WIDTH = 3
HEAD_DIM = 64
N_HEADS = (D_MODEL // 2) // HEAD_DIM
N_KV_HEADS = N_HEADS // 4
GROUP = N_HEADS // N_KV_HEADS
Q_DIM = N_HEADS * HEAD_DIM
KV_DIM = N_KV_HEADS * HEAD_DIM
PAIR = 2 * HEAD_DIM
WINDOW = 128
D_FF = ((8 * D_MODEL // 3 + 127) // 128) * 128
D_IN = 3 * D_CONV + Q_DIM + 2 * KV_DIM + 2 * D_MODEL
EPS = 1e-6
NEG_INF = -1e30
LOG2E = 1.4426950408889634
Q_SCALE = HEAD_DIM ** -0.5 * LOG2E

OFF_XC = 0
OFF_BG = D_CONV
OFF_CG = 2 * D_CONV
OFF_Q = 3 * D_CONV
OFF_K = OFF_Q + Q_DIM
OFF_V = OFF_K + KV_DIM
OFF_GC = OFF_V + KV_DIM
OFF_GA = OFF_GC + D_MODEL
N_QKV = Q_DIM + 2 * KV_DIM

N_MAIN = BATCH * SEQ
N_SAMPLE = DEC_BATCH * DEC_SEQ
N_SMALL = 256
SAMPLE_ROW0 = N_META

N_TILES = 16
TM = N_MAIN // N_TILES
TS = N_SMALL // N_TILES
TILE = TM + TS
N_ALL = N_TILES * TILE
TILES_PER_SEQ = SEQ // TM

TF = 512
D_FF_PAD = ((D_FF + TF - 1) // TF) * TF
N_FF_CHUNKS = D_FF_PAD // TF
FFN_STEPS = N_TILES * N_FF_CHUNKS
TN_OUT = 1024
N_OUT_CHUNKS = D_MODEL // TN_OUT
GATE_PIECE = 512
GATE_PIECES = TN_OUT // GATE_PIECE

QBLK = 128
KPAD = QBLK - N_META
N_QBLK = SEQ // QBLK
KBUF_ROWS = QBLK + SEQ

VMEM_LIMIT = 56 * 1024 * 1024

_bf16 = jnp.bfloat16
_f32 = jnp.float32


def _params(sem):
    return pltpu.CompilerParams(dimension_semantics=sem, vmem_limit_bytes=VMEM_LIMIT)


def _rms_rows(x, g):
    ms = jnp.mean(x * x, axis=-1, keepdims=True)
    return x * lax.rsqrt(ms + EPS) * g


def _pair_rms(x, g_pair):
    low = lax.broadcasted_iota(jnp.int32, x.shape, 1) < HEAD_DIM
    x2 = x * x
    s_lo = jnp.sum(jnp.where(low, x2, 0.0), axis=-1, keepdims=True)
    s_hi = jnp.sum(jnp.where(low, 0.0, x2), axis=-1, keepdims=True)
    ms = jnp.where(low, s_lo, s_hi) * (1.0 / HEAD_DIM)
    return x * lax.rsqrt(ms + EPS) * g_pair


class _CastJob(NamedTuple):
    slab: int
    rows_in: int
    n_in_slabs: int
    n_out_slabs: int
    n_in_cols: int
    n_out_cols: int
    segments: tuple
    zero_ranges: tuple


def _cast_job(rows_in, cols_in, slab, segments=None, zero_ranges=(), rows_out=None, cols_out=None):
    rows_out = rows_in if rows_out is None else rows_out
    cols_out = cols_in if cols_out is None else cols_out
    segments = ((0, cols_in, 0),) if segments is None else segments
    assert rows_out % slab == 0
    return _CastJob(slab, rows_in, -(-rows_in // slab), rows_out // slab, cols_in, cols_out,
                    tuple(segments), tuple(zero_ranges))


def _job_specs(job, step_of):
    in_idx = lambda *ids: (jnp.minimum(step_of(*ids), job.n_in_slabs - 1), 0)
    out_idx = lambda *ids: (jnp.minimum(step_of(*ids), job.n_out_slabs - 1), 0)
    return (pl.BlockSpec((job.slab, job.n_in_cols), in_idx),
            pl.BlockSpec((job.slab, job.n_out_cols), out_idx))


def _run_cast_job(job, step, x_ref, o_ref):
    out_slab = jnp.minimum(step, job.n_out_slabs - 1)
    padded = job.n_out_slabs * job.slab > job.rows_in
    for src, width, dst in job.segments:
        x = x_ref[:, src:src + width]
        if padded:
            row = out_slab * job.slab + lax.broadcasted_iota(jnp.int32, x.shape, 0)
            x = jnp.where(row < job.rows_in, x, 0.0)
        o_ref[:, dst:dst + width] = x.astype(_bf16)
    for lo, hi in job.zero_ranges:
        o_ref[:, lo:hi] = jnp.zeros((job.slab, hi - lo), _bf16)


def _cast_kernel(job, x_ref, o_ref):
    _run_cast_job(job, pl.program_id(0), x_ref, o_ref)


def _cast(job, w):
    in_spec, out_spec = _job_specs(job, lambda i: i)
    return pl.pallas_call(
        functools.partial(_cast_kernel, job),
        grid=(job.n_out_slabs,),
        in_specs=[in_spec],
        out_specs=out_spec,
        out_shape=jax.ShapeDtypeStruct((job.n_out_slabs * job.slab, job.n_out_cols), _bf16),
        compiler_params=_params(("arbitrary",)),
        name="cast",
    )(w)


def _w_up_job(slab):
    segments, zero_ranges = [], []
    for c in range(N_FF_CHUNKS):
        width = min(TF, D_FF - c * TF)
        for half, src0 in enumerate((0, D_FF)):
            dst = (2 * c + half) * TF
            segments.append((src0 + c * TF, width, dst))
            if width < TF:
                zero_ranges.append((dst + width, dst + TF))
    return _cast_job(D_MODEL, 2 * D_FF, slab, segments=segments, zero_ranges=zero_ranges,
                     cols_out=2 * D_FF_PAD)


def _w_down_job(slab):
    return _cast_job(D_FF, D_MODEL, slab, rows_out=D_FF_PAD)


def _ffn_kernel(jobs, with_proj, with_norm_out, *refs):
    refs = list(refs)
    take = lambda n: [refs.pop(0) for _ in range(n)]
    xm_ref, xs_ref = take(2)
    if with_proj:
        tm_ref, ts_ref, wo_ref = take(3)
    g_ref, wgu_ref, wd_ref = take(3)
    if with_norm_out:
        gn_ref, = take(1)
    job_in = take(len(jobs))
    om_ref, os_ref = take(2)
    if with_norm_out:
        hn_ref, = take(1)
    job_out = take(len(jobs))
    h_ref = refs[0]
    i = pl.program_id(0)
    j = pl.program_id(1)

    @pl.when(j == 0)
    def _():
        g = g_ref[...]
        xm = xm_ref[...]
        xs = xs_ref[...]
        if with_proj:
            t_all = refs[1]
            t_all[0:TM, :] = tm_ref[...]
            t_all[TM:, :] = ts_ref[...]
            p = jnp.dot(t_all[...], wo_ref[...], preferred_element_type=_f32)
            xm = xm + p[0:TM, :]
            xs = xs + p[TM:, :]
        h_ref[0:TM, :] = _rms_rows(xm, g).astype(_bf16)
        om_ref[...] = xm
        h_ref[TM:, :] = _rms_rows(xs, g).astype(_bf16)
        os_ref[...] = xs

    for job, x_ref, o_ref in zip(jobs, job_in, job_out):
        _run_cast_job(job, i * N_FF_CHUNKS + j, x_ref, o_ref)

    h = h_ref[...]
    gu = jnp.dot(h, wgu_ref[...], preferred_element_type=_f32)
    gate, up = gu[:, 0:TF], gu[:, TF:2 * TF]
    a = (gate * jax.nn.sigmoid(gate) * up * 0.5).astype(_bf16)
    r = jnp.dot(a, wd_ref[...], preferred_element_type=_f32)
    om_ref[...] += r[0:TM, :]
    os_ref[...] += r[TM:, :]

    if with_norm_out:
        @pl.when(j == N_FF_CHUNKS - 1)
        def _():
            gn = gn_ref[...]
            hn_ref[0:TM, :] = _rms_rows(om_ref[...], gn).astype(_bf16)
            hn_ref[TM:, :] = _rms_rows(os_ref[...], gn).astype(_bf16)


def _ffn(xm, xs, g, wgu, wd, proj=None, next_gain=None, jobs=(), job_inputs=()):
    assert all(job.n_out_slabs <= FFN_STEPS for job in jobs)
    job_specs = [_job_specs(job, lambda i, j: i * N_FF_CHUNKS + j) for job in jobs]
    row = lambda i, j: (i, 0)
    fixed = lambda i, j: (0, 0)
    in_specs = [pl.BlockSpec((TM, D_MODEL), row), pl.BlockSpec((TS, D_MODEL), row)]
    out_specs = [pl.BlockSpec((TM, D_MODEL), row), pl.BlockSpec((TS, D_MODEL), row)]
    out_shape = [jax.ShapeDtypeStruct((N_MAIN, D_MODEL), _f32),
                 jax.ShapeDtypeStruct((N_SMALL, D_MODEL), _f32)]
    scratch = [pltpu.VMEM((TILE, D_MODEL), _bf16)]
    args = [xm, xs]
    if proj is not None:
        in_specs += [pl.BlockSpec((TM, D_MODEL), row), pl.BlockSpec((TS, D_MODEL), row),
                     pl.BlockSpec((D_MODEL, D_MODEL), fixed, pipeline_mode=pl.Buffered(1))]
        scratch += [pltpu.VMEM((TILE, D_MODEL), _bf16)]
        args += list(proj)
    in_specs += [
        pl.BlockSpec((1, D_MODEL), fixed),
        pl.BlockSpec((D_MODEL, 2 * TF), lambda i, j: (0, j)),
        pl.BlockSpec((TF, D_MODEL), lambda i, j: (j, 0)),
    ]
    args += [g, wgu, wd]
    if next_gain is not None:
        in_specs += [pl.BlockSpec((1, D_MODEL), fixed)]
        args += [next_gain]
        out_specs += [pl.BlockSpec((TILE, D_MODEL), row)]
        out_shape += [jax.ShapeDtypeStruct((N_ALL, D_MODEL), _bf16)]
    in_specs += [s[0] for s in job_specs]
    args += list(job_inputs)
    out_specs += [s[1] for s in job_specs]
    out_shape += [jax.ShapeDtypeStruct((job.n_out_slabs * job.slab, job.n_out_cols), _bf16)
                  for job in jobs]
    return pl.pallas_call(
        functools.partial(_ffn_kernel, tuple(jobs), proj is not None, next_gain is not None),
        grid=(N_TILES, N_FF_CHUNKS),
        in_specs=in_specs,
        out_specs=out_specs,
        out_shape=out_shape,
        scratch_shapes=scratch,
        compiler_params=_params(("arbitrary", "arbitrary")),
        name="ffn",
    )(*args)


def _ffn_stream_kernel(xm_ref, xs_ref, tm_ref, ts_ref, wo_ref, g_ref, wgu_hbm, wd_hbm,
                       om_ref, os_ref, h_ref, t_all, wgu_buf, wd_buf, sem):
    i = pl.program_id(0)

    def chunk_copies(chunk, slot):
        return (
            pltpu.make_async_copy(wgu_hbm.at[:, pl.ds(chunk * 2 * TF, 2 * TF)],
                                  wgu_buf.at[slot], sem.at[0, slot]),
            pltpu.make_async_copy(wd_hbm.at[pl.ds(chunk * TF, TF), :],
                                  wd_buf.at[slot], sem.at[1, slot]),
        )

    def start(chunk, slot):
        for c in chunk_copies(chunk, slot):
            c.start()

    @pl.when(i == 0)
    def _():
        start(0, 0)

    g = g_ref[...]
    t_all[0:TM, :] = tm_ref[...]
    t_all[TM:, :] = ts_ref[...]
    p = jnp.dot(t_all[...], wo_ref[...], preferred_element_type=_f32)
    xm = xm_ref[...] + p[0:TM, :]
    xs = xs_ref[...] + p[TM:, :]
    h_ref[0:TM, :] = _rms_rows(xm, g).astype(_bf16)
    om_ref[...] = xm
    h_ref[TM:, :] = _rms_rows(xs, g).astype(_bf16)
    os_ref[...] = xs

    h = h_ref[...]
    for j in range(N_FF_CHUNKS):
        slot = (i * N_FF_CHUNKS + j) % 2
        if j + 1 < N_FF_CHUNKS:
            start(j + 1, 1 - slot)
        else:
            @pl.when(i + 1 < N_TILES)
            def _():
                start(0, 1 - slot)
        for c in chunk_copies(j, slot):
            c.wait()
        gu = jnp.dot(h, wgu_buf[slot], preferred_element_type=_f32)
        gate, up = gu[:, 0:TF], gu[:, TF:2 * TF]
        a = (gate * jax.nn.sigmoid(gate) * up * 0.5).astype(_bf16)
        r = jnp.dot(a, wd_buf[slot], preferred_element_type=_f32)
        om_ref[...] += r[0:TM, :]
        os_ref[...] += r[TM:, :]


def _ffn_stream(xm, xs, g, wgu, wd, proj):
    row = lambda i: (i, 0)
    fixed = lambda i: (0, 0)
    return pl.pallas_call(
        _ffn_stream_kernel,
        grid=(N_TILES,),
        in_specs=[
            pl.BlockSpec((TM, D_MODEL), row),
            pl.BlockSpec((TS, D_MODEL), row),
            pl.BlockSpec((TM, D_MODEL), row),
            pl.BlockSpec((TS, D_MODEL), row),
            pl.BlockSpec((D_MODEL, D_MODEL), fixed, pipeline_mode=pl.Buffered(1)),
            pl.BlockSpec((1, D_MODEL), fixed),
            pl.BlockSpec(memory_space=pl.ANY),
            pl.BlockSpec(memory_space=pl.ANY),
        ],
        out_specs=[pl.BlockSpec((TM, D_MODEL), row), pl.BlockSpec((TS, D_MODEL), row)],
        out_shape=[jax.ShapeDtypeStruct((N_MAIN, D_MODEL), _f32),
                   jax.ShapeDtypeStruct((N_SMALL, D_MODEL), _f32)],
        scratch_shapes=[
            pltpu.VMEM((TILE, D_MODEL), _bf16),
            pltpu.VMEM((TILE, D_MODEL), _bf16),
            pltpu.VMEM((2, D_MODEL, 2 * TF), _bf16),
            pltpu.VMEM((2, TF, D_MODEL), _bf16),
            pltpu.SemaphoreType.DMA((2, 2)),
        ],
        compiler_params=_params(("arbitrary",)),
        name="ffn_stream",
    )(xm, xs, *proj, g, wgu, wd)


N_BRANCH_IN = OFF_GC
CONV_HALO = 8
U_BASE = 112
U_STATE0 = U_BASE + SAMPLE_ROW0 - (CONV_WIDTH - 1) * DEC_BATCH
U_ROWS = U_BASE + N_SMALL
U_LAST0 = U_BASE + SAMPLE_ROW0 + (DEC_SEQ - (CONV_WIDTH - 1)) * DEC_BATCH
assert U_BASE >= (CONV_WIDTH - 1) * DEC_BATCH and U_BASE % TS == 0 and TS == N_META


def _inproj_kernel(h_ref, w_ref, cw_ref, st_ref, qg_ref, kg_ref,
                   cb_ref, q_ref, k_ref, v_ref, sm_ref, ncp_ref, ncs_ref,
                   ubuf, meta_halo, utab):
    i = pl.program_id(0)

    @pl.when(i == 0)
    def _():
        utab[...] = jnp.zeros(utab.shape, _f32)
        utab[U_STATE0:U_STATE0 + (CONV_WIDTH - 1) * DEC_BATCH, :] = st_ref[...]
        meta_halo[...] = jnp.zeros(meta_halo.shape, _f32)
        ubuf[0:CONV_HALO, :] = jnp.zeros((CONV_HALO, D_CONV), _f32)

    h = h_ref[...]
    z_q = jnp.dot(h, w_ref[:, OFF_Q:OFF_K], preferred_element_type=_f32)
    z_kv = jnp.dot(h, w_ref[:, OFF_K:OFF_GC], preferred_element_type=_f32)
    z = jnp.dot(h, w_ref[:, 0:OFF_Q], preferred_element_type=_f32)
    cw = cw_ref[...]
    w0, w1, w2 = cw[0:1, :], cw[1:2, :], cw[2:3, :]
    qg = qg_ref[...]
    kg = kg_ref[...]
    seg = lambda a, off, width: a[:, off:off + width]

    def qkv(rows, q_out, k_out, v_out):
        for t in range(Q_DIM // PAIR):
            q_out(t, _pair_rms(z_q[rows, t * PAIR:(t + 1) * PAIR], qg) * Q_SCALE)
        for t in range(KV_DIM // PAIR):
            k_out(t, _pair_rms(z_kv[rows, t * PAIR:(t + 1) * PAIR], kg))
            v_out(t, z_kv[rows, KV_DIM + t * PAIR:KV_DIM + (t + 1) * PAIR])

    zs = z[TM:, :]
    u_s = seg(zs, OFF_CG, D_CONV) * seg(zs, OFF_XC, D_CONV)
    cur = pl.multiple_of(U_BASE + TS * i, TS)
    slot = pl.multiple_of(jnp.where(i == 0, U_ROWS, cur), TS)
    utab[pl.ds(slot, TS), :] = u_s
    y_s = (w0 * utab[pl.ds(cur - 2 * DEC_BATCH, TS), :]
           + w1 * utab[pl.ds(cur - DEC_BATCH, TS), :] + w2 * u_s)
    cb_ref[TM:, :] = (seg(zs, OFF_BG, D_CONV) * y_s).astype(_bf16)
    meta_halo[...] = jnp.where(i == 0, u_s[N_META - CONV_HALO:N_META, :], meta_halo[...])

    def sm_store(off):
        def store(t, val):
            sm_ref[:, off + t * PAIR:off + (t + 1) * PAIR] = val
        return store
    qkv(slice(TM, TILE), sm_store(0), sm_store(Q_DIM), sm_store(Q_DIM + KV_DIM))

    ncs_ref[...] = utab[U_LAST0:U_LAST0 + (CONV_WIDTH - 1) * DEC_BATCH, :]

    ubuf[0:CONV_HALO, :] = jnp.where(i % TILES_PER_SEQ == 0, meta_halo[...], ubuf[0:CONV_HALO, :])

    zm = z[0:TM, :]
    u = seg(zm, OFF_CG, D_CONV) * seg(zm, OFF_XC, D_CONV)
    ubuf[CONV_HALO:CONV_HALO + TM, :] = u
    u1 = ubuf[CONV_HALO - 1:CONV_HALO - 1 + TM, :]
    u2 = ubuf[CONV_HALO - 2:CONV_HALO - 2 + TM, :]
    cb_ref[0:TM, :] = (seg(zm, OFF_BG, D_CONV) * (w0 * u2 + w1 * u1 + w2 * u)).astype(_bf16)
    tail = ubuf[TM:TM + CONV_HALO, :]
    ubuf[0:CONV_HALO, :] = tail
    ncp_ref[0] = tail[CONV_HALO - (CONV_WIDTH - 1):CONV_HALO, :]

    def lane_store(ref):
        def store(t, val):
            ref[:, t * PAIR:(t + 1) * PAIR] = val
        return store
    qkv(slice(0, TM), lane_store(q_ref), lane_store(k_ref), lane_store(v_ref))


def _inproj(h_all, w_in_b, conv_w, state_rows, q_gain, k_gain):
    qg = jnp.tile(q_gain.reshape(1, HEAD_DIM), (1, 2))
    kg = jnp.tile(k_gain.reshape(1, HEAD_DIM), (1, 2))
    row = lambda i: (i, 0)
    fixed = lambda i: (0, 0)
    return pl.pallas_call(
        _inproj_kernel,
        grid=(N_TILES,),
        in_specs=[
            pl.BlockSpec((TILE, D_MODEL), row),
            pl.BlockSpec((D_MODEL, N_BRANCH_IN), fixed, pipeline_mode=pl.Buffered(1)),
            pl.BlockSpec((CONV_WIDTH, D_CONV), fixed),
            pl.BlockSpec(((CONV_WIDTH - 1) * DEC_BATCH, D_CONV), fixed),
            pl.BlockSpec((1, PAIR), fixed),
            pl.BlockSpec((1, PAIR), fixed),
        ],
        out_specs=[
            pl.BlockSpec((TILE, D_CONV), row),
            pl.BlockSpec((TM, Q_DIM), row),
            pl.BlockSpec((TM, KV_DIM), row),
            pl.BlockSpec((TM, KV_DIM), row),
            pl.BlockSpec((TS, N_QKV), row),
            pl.BlockSpec((1, CONV_WIDTH - 1, D_CONV), lambda i: (i // TILES_PER_SEQ, 0, 0)),
            pl.BlockSpec(((CONV_WIDTH - 1) * DEC_BATCH, D_CONV), fixed),
        ],
        out_shape=[
            jax.ShapeDtypeStruct((N_ALL, D_CONV), _bf16),
            jax.ShapeDtypeStruct((N_MAIN, Q_DIM), _f32),
            jax.ShapeDtypeStruct((N_MAIN, KV_DIM), _f32),
            jax.ShapeDtypeStruct((N_MAIN, KV_DIM), _f32),
            jax.ShapeDtypeStruct((N_SMALL, N_QKV), _f32),
            jax.ShapeDtypeStruct((BATCH, CONV_WIDTH - 1, D_CONV), _f32),
            jax.ShapeDtypeStruct(((CONV_WIDTH - 1) * DEC_BATCH, D_CONV), _f32),
        ],
        scratch_shapes=[
            pltpu.VMEM((CONV_HALO + TM, D_CONV), _f32),
            pltpu.VMEM((CONV_HALO, D_CONV), _f32),
            pltpu.VMEM((U_ROWS + TS, D_CONV), _f32),
        ],
        compiler_params=_params(("arbitrary",)),
        name="inproj",
    )(h_all, w_in_b, conv_w, state_rows, qg, kg)


QPAIR = 2 * GROUP * HEAD_DIM
HEADS_PER_STEP = 2 * GROUP
BLOCK_UNROLL = 1
assert (N_QBLK - 1) % BLOCK_UNROLL == 0


def _attn_prompt_kernel(sink_ref, q_ref, k_ref, v_ref, km_ref, vm_ref,
                        o_ref, nk_ref, nv_ref, kt_buf, vbuf, bias_ref):
    gp = pl.program_id(1)

    vbuf[0:KPAD, :] = jnp.zeros((KPAD, PAIR), _bf16)
    vbuf[KPAD:QBLK, :] = vm_ref[...].astype(_bf16)
    vbuf[QBLK:KBUF_ROWS, :] = v_ref[...].astype(_bf16)
    first_keys = jnp.concatenate([jnp.zeros((KPAD, PAIR), _f32), km_ref[...]], axis=0)
    kt_buf[0] = first_keys.T.astype(_bf16)
    for b in range(N_QBLK):
        kt_buf[b + 1] = k_ref[b * QBLK:(b + 1) * QBLK, :].T.astype(_bf16)
    nk_ref[0] = k_ref[SEQ - WINDOW:SEQ, :]
    nv_ref[0] = v_ref[SEQ - WINDOW:SEQ, :]

    row = lax.broadcasted_iota(jnp.int32, (QBLK, 2 * QBLK), 0)
    col = lax.broadcasted_iota(jnp.int32, (QBLK, 2 * QBLK), 1)
    dist = QBLK + row - col
    in_window = (dist >= 0) & (dist <= WINDOW)
    distf = dist.astype(_f32)
    for g in range(2):
        for hp in range(GROUP):
            head = gp * HEADS_PER_STEP + g * GROUP + hp
            slope = jnp.exp2(jnp.full((QBLK, 2 * QBLK), -8.0 / N_HEADS, _f32)
                             * (head + 1).astype(_f32))
            bias_ref[g, hp * QBLK:(hp + 1) * QBLK, :] = jnp.where(
                in_window, -slope * distf * LOG2E, NEG_INF)

    lane = lax.broadcasted_iota(jnp.int32, (QBLK, PAIR), 1)
    low_half = lane < HEAD_DIM

    def block(i, first):
        r0 = i * QBLK if first else pl.multiple_of(i * QBLK, QBLK)
        qn = q_ref[pl.ds(r0 - QBLK, QBLK), :]
        kt = jnp.concatenate([kt_buf[i - 1], kt_buf[i]], axis=1)
        vv = vbuf[pl.ds(r0 - QBLK, 2 * QBLK), :]
        for g in range(2):
            keep = low_half if g == 0 else jnp.logical_not(low_half)
            outs = []
            for hp in range(GROUP):
                hh = g * GROUP + hp
                tile = qn[:, (hh // 2) * PAIR:(hh // 2 + 1) * PAIR]
                if hp % 2 != g:
                    tile = pltpu.roll(tile, HEAD_DIM, axis=1)
                qp = jnp.where(keep, tile, 0.0).astype(_bf16)
                s = jnp.dot(qp, kt, preferred_element_type=_f32)
                s = s + bias_ref[g, hp * QBLK:(hp + 1) * QBLK, :]
                if first:
                    s = jnp.where(col < KPAD, NEG_INF, s)
                sink = sink_ref[gp * HEADS_PER_STEP + hh] * LOG2E
                m = jnp.maximum(jnp.max(s, axis=-1, keepdims=True), sink)
                e = jnp.exp2(s - m)
                denom = jnp.sum(e, axis=-1, keepdims=True) + jnp.exp2(sink - m)
                o = jnp.dot(e.astype(_bf16), vv, preferred_element_type=_f32)
                outs.append(o / denom)
            for pr in range(GROUP // 2):
                even, odd = outs[2 * pr], outs[2 * pr + 1]
                if g == 0:
                    odd = pltpu.roll(odd, HEAD_DIM, axis=1)
                else:
                    even = pltpu.roll(even, HEAD_DIM, axis=1)
                t = g * (GROUP // 2) + pr
                o_ref[pl.ds(r0 - QBLK, QBLK), t * PAIR:(t + 1) * PAIR] = (
                    jnp.where(low_half, even, odd).astype(_bf16))

    block(1, True)

    def later_block(i, carry):
        block(i, False)
        return carry

    lax.fori_loop(2, N_QBLK + 1, later_block, 0, unroll=BLOCK_UNROLL)


def _attn_prompt(qm, km, vm, qkv_s, sinks):
    grid_spec = pltpu.PrefetchScalarGridSpec(
        num_scalar_prefetch=1,
        grid=(BATCH, N_KV_HEADS // 2),
        in_specs=[
            pl.BlockSpec((SEQ, QPAIR), lambda b, p, s: (b, p)),
            pl.BlockSpec((SEQ, PAIR), lambda b, p, s: (b, p)),
            pl.BlockSpec((SEQ, PAIR), lambda b, p, s: (b, p)),
            pl.BlockSpec((N_META, PAIR), lambda b, p, s: (0, Q_DIM // PAIR + p)),
            pl.BlockSpec((N_META, PAIR), lambda b, p, s: (0, (Q_DIM + KV_DIM) // PAIR + p)),
        ],
        out_specs=[
            pl.BlockSpec((SEQ, QPAIR), lambda b, p, s: (b, p)),
            pl.BlockSpec((1, WINDOW, PAIR), lambda b, p, s: (b, 0, p)),
            pl.BlockSpec((1, WINDOW, PAIR), lambda b, p, s: (b, 0, p)),
        ],
        scratch_shapes=[
            pltpu.VMEM((N_QBLK + 1, PAIR, QBLK), _bf16),
            pltpu.VMEM((KBUF_ROWS, PAIR), _bf16),
            pltpu.VMEM((2, GROUP * QBLK, 2 * QBLK), _f32),
        ],
    )
    return pl.pallas_call(
        _attn_prompt_kernel,
        grid_spec=grid_spec,
        out_shape=[
            jax.ShapeDtypeStruct((N_MAIN, Q_DIM), _bf16),
            jax.ShapeDtypeStruct((BATCH, WINDOW, KV_DIM), _f32),
            jax.ShapeDtypeStruct((BATCH, WINDOW, KV_DIM), _f32),
        ],
        compiler_params=_params(("arbitrary", "arbitrary")),
        name="attn_prompt",
    )(sinks, qm, km, vm, qkv_s, qkv_s)


SEQ_CHUNK = 8
N_QROWS = N_HEADS * DEC_SEQ
N_KEYS = WINDOW + DEC_SEQ
N_KEYS_PAD = ((N_KEYS + 7) // 8) * 8


def _attn_sample_kernel(q_ref, kn_ref, vn_ref, ck_ref, cv_ref, sink_ref,
                        o_ref, nk_ref, nv_ref, kk_buf, vv_buf):
    q4 = q_ref[...]
    shp = (SEQ_CHUNK, N_QROWS, KV_DIM)
    row_kv = lax.broadcasted_iota(jnp.int32, shp, 1) // (GROUP * DEC_SEQ)
    lane_kv = lax.broadcasted_iota(jnp.int32, shp, 2) // HEAD_DIM
    own = row_kv == lane_kv
    qp = jnp.where(own, q4, 0.0)

    kk_buf[:, 0:WINDOW, :] = ck_ref[...]
    kk_buf[:, WINDOW:N_KEYS, :] = kn_ref[...]
    kk_buf[:, N_KEYS:N_KEYS_PAD, :] = jnp.zeros((SEQ_CHUNK, N_KEYS_PAD - N_KEYS, KV_DIM), _f32)
    vv_buf[:, 0:WINDOW, :] = cv_ref[...]
    vv_buf[:, WINDOW:N_KEYS, :] = vn_ref[...]
    vv_buf[:, N_KEYS:N_KEYS_PAD, :] = jnp.zeros((SEQ_CHUNK, N_KEYS_PAD - N_KEYS, KV_DIM), _f32)
    nk_ref[...] = kk_buf[:, DEC_SEQ:N_KEYS, :]
    nv_ref[...] = vv_buf[:, DEC_SEQ:N_KEYS, :]

    kk = kk_buf[...]
    vv = vv_buf[...]
    s = jnp.einsum('bqd,bkd->bqk', qp, kk, preferred_element_type=_f32)

    row = lax.broadcasted_iota(jnp.int32, (N_QROWS, N_KEYS_PAD), 0)
    col = lax.broadcasted_iota(jnp.int32, (N_QROWS, N_KEYS_PAD), 1)
    step = row % DEC_SEQ
    head = (row // DEC_SEQ + 1).astype(_f32)
    dist = step + WINDOW - col
    valid = (dist >= 0) & (dist <= WINDOW) & (col < N_KEYS)
    slope = jnp.exp2(head * (-8.0 / N_HEADS))
    bias = jnp.where(valid, -slope * dist.astype(_f32) * LOG2E, NEG_INF)
    s = jnp.where(valid[None], s + bias[None], NEG_INF)
    sink = sink_ref[...][None] * LOG2E
    m = jnp.maximum(jnp.max(s, axis=-1, keepdims=True), sink)
    e = jnp.exp2(s - m)
    denom = jnp.sum(e, axis=-1, keepdims=True) + jnp.exp2(sink - m)
    o = jnp.einsum('bqk,bkd->bqd', e, vv, preferred_element_type=_f32) / denom
    o = jnp.where(own, o, 0.0)
    acc = o
    for g in range(1, N_KV_HEADS):
        acc = acc + pltpu.roll(o, g * HEAD_DIM, axis=2)
    o_ref[...] = acc[:, :, 0:HEAD_DIM]


def _attn_sample(q_s, k_new, v_new, cache_k, cache_v, sink_rows):
    c = SEQ_CHUNK
    return pl.pallas_call(
        _attn_sample_kernel,
        grid=(DEC_BATCH // c,),
        in_specs=[
            pl.BlockSpec((c, N_QROWS, KV_DIM), lambda i: (i, 0, 0)),
            pl.BlockSpec((c, DEC_SEQ, KV_DIM), lambda i: (i, 0, 0)),
            pl.BlockSpec((c, DEC_SEQ, KV_DIM), lambda i: (i, 0, 0)),
            pl.BlockSpec((c, WINDOW, KV_DIM), lambda i: (i, 0, 0)),
            pl.BlockSpec((c, WINDOW, KV_DIM), lambda i: (i, 0, 0)),
            pl.BlockSpec((N_QROWS, 1), lambda i: (0, 0)),
        ],
        out_specs=[
            pl.BlockSpec((c, N_QROWS, HEAD_DIM), lambda i: (i, 0, 0)),
            pl.BlockSpec((c, WINDOW, KV_DIM), lambda i: (i, 0, 0)),
            pl.BlockSpec((c, WINDOW, KV_DIM), lambda i: (i, 0, 0)),
        ],
        out_shape=[
            jax.ShapeDtypeStruct((DEC_BATCH, N_QROWS, HEAD_DIM), _f32),
            jax.ShapeDtypeStruct((DEC_BATCH, WINDOW, KV_DIM), _f32),
            jax.ShapeDtypeStruct((DEC_BATCH, WINDOW, KV_DIM), _f32),
        ],
        scratch_shapes=[
            pltpu.VMEM((c, N_KEYS_PAD, KV_DIM), _f32),
            pltpu.VMEM((c, N_KEYS_PAD, KV_DIM), _f32),
        ],
        compiler_params=_params(("arbitrary",)),
        name="attn_sample",
    )(q_s, k_new, v_new, cache_k, cache_v, sink_rows)


def _gate_kernel(*refs):
    h_ref, cb_ref, atm_ref, ats_ref = refs[:4]
    w_refs = refs[4:4 + 4 * GATE_PIECES]
    tm_ref, ts_ref, at_all = refs[4 + 4 * GATE_PIECES:]
    at_all[0:TM, :] = atm_ref[...]
    at_all[TM:, :] = ats_ref[...]
    h = h_ref[...]
    cb = cb_ref[...]
    at = at_all[...]
    for c in range(GATE_PIECES):
        wgc_ref, wga_ref, wc_ref, wa_ref = w_refs[4 * c:4 * c + 4]
        cols = slice(c * GATE_PIECE, (c + 1) * GATE_PIECE)
        gc = jnp.dot(h, wgc_ref[...], preferred_element_type=_f32)
        ga = jnp.dot(h, wga_ref[...], preferred_element_type=_f32)
        yc = jnp.dot(cb, wc_ref[...], preferred_element_type=_f32)
        ya = jnp.dot(at, wa_ref[...], preferred_element_type=_f32)
        t = (jax.nn.sigmoid(gc) * yc + jax.nn.sigmoid(ga) * ya).astype(_bf16)
        tm_ref[:, cols] = t[0:TM, :]
        ts_ref[:, cols] = t[TM:, :]


def _gate(h_all, cb_all, atm, ats, w_in_b, wc, wa):
    row = lambda n, i: (i, 0)
    once = pl.Buffered(1)
    w_specs, w_args = [], []
    for c in range(GATE_PIECES):
        piece = lambda n, i, c=c, base=0: (0, base + n * GATE_PIECES + c)
        w_specs += [
            pl.BlockSpec((D_MODEL, GATE_PIECE),
                         functools.partial(piece, base=OFF_GC // GATE_PIECE), pipeline_mode=once),
            pl.BlockSpec((D_MODEL, GATE_PIECE),
                         functools.partial(piece, base=OFF_GA // GATE_PIECE), pipeline_mode=once),
            pl.BlockSpec((D_CONV, GATE_PIECE), piece, pipeline_mode=once),
            pl.BlockSpec((Q_DIM, GATE_PIECE), piece, pipeline_mode=once),
        ]
        w_args += [w_in_b, w_in_b, wc, wa]
    return pl.pallas_call(
        _gate_kernel,
        grid=(N_OUT_CHUNKS, N_TILES),
        in_specs=[
            pl.BlockSpec((TILE, D_MODEL), row),
            pl.BlockSpec((TILE, D_CONV), row),
            pl.BlockSpec((TM, Q_DIM), row),
            pl.BlockSpec((TS, Q_DIM), row),
        ] + w_specs,
        out_specs=[
            pl.BlockSpec((TM, TN_OUT), lambda n, i: (i, n)),
            pl.BlockSpec((TS, TN_OUT), lambda n, i: (i, n)),
        ],
        out_shape=[
            jax.ShapeDtypeStruct((N_MAIN, D_MODEL), _bf16),
            jax.ShapeDtypeStruct((N_SMALL, D_MODEL), _bf16),
        ],
        scratch_shapes=[pltpu.VMEM((TILE, Q_DIM), _bf16)],
        compiler_params=_params(("arbitrary", "arbitrary")),
        name="gate",
    )(h_all, cb_all, atm, ats, *w_args)


def _sample_rows(a):
    return a.transpose(1, 0, 2).reshape(N_SAMPLE, a.shape[-1])


def _to_small(sample_rows, dtype):
    w = sample_rows.shape[-1]
    return jnp.concatenate([
        jnp.zeros((SAMPLE_ROW0, w), dtype), sample_rows.astype(dtype),
        jnp.zeros((N_SMALL - SAMPLE_ROW0 - N_SAMPLE, w), dtype)], axis=0)


def kernel(x_prompt, x_sample, state_conv, cache_k_win, cache_v_win, meta_tokens, ffn1_norm, ffn1_w_up, ffn1_w_down, mix_norm, w_in, q_norm, k_norm, conv_w, w_conv_out, attn_sinks, w_attn_out, w_o, ffn2_norm, ffn2_w_up, ffn2_w_down):
    l = 0
    xm = x_prompt.reshape(N_MAIN, D_MODEL)
    xs = jnp.concatenate([
        meta_tokens.astype(_f32), _sample_rows(x_sample),
        jnp.zeros((N_SMALL - N_META - N_SAMPLE, D_MODEL), _f32)], axis=0)

    wgu1 = _cast(_w_up_job(256), ffn1_w_up[l])
    wd1 = _cast(_w_down_job(TF), ffn1_w_down[l])
    jobs = (_w_up_job(16), _w_down_job(32), _cast_job(D_MODEL, D_IN, 16),
            _cast_job(D_CONV, D_MODEL, 16), _cast_job(Q_DIM, D_MODEL, 16),
            _cast_job(D_MODEL, D_MODEL, 16))
    x1m, x1s, h_all, wgu2, wd2, w_in_b, wc_b, wa_b, wo_b = _ffn(
        xm, xs, ffn1_norm[l].reshape(1, D_MODEL), wgu1, wd1,
        next_gain=mix_norm[l].reshape(1, D_MODEL), jobs=jobs,
        job_inputs=(ffn2_w_up[l], ffn2_w_down[l], w_in[l], w_conv_out[l], w_attn_out[l], w_o[l]))

    state_rows = state_conv[l].transpose(1, 0, 2).reshape((CONV_WIDTH - 1) * DEC_BATCH, D_CONV)
    cb_all, qm, km, vm, qkv_s, new_conv_p, new_conv_s = _inproj(
        h_all, w_in_b, conv_w[l], state_rows, q_norm[l], k_norm[l])

    atm, new_k_p, new_v_p = _attn_prompt(qm, km, vm, qkv_s, attn_sinks[l])
    samp = qkv_s[SAMPLE_ROW0:SAMPLE_ROW0 + N_SAMPLE]
    q_s = (samp[:, 0:Q_DIM].reshape(DEC_SEQ, DEC_BATCH, N_HEADS, HEAD_DIM)
           .transpose(1, 2, 0, 3).reshape(DEC_BATCH, N_QROWS, HEAD_DIM))
    q_s = jnp.tile(q_s, (1, 1, N_KV_HEADS))
    k_new = samp[:, Q_DIM:Q_DIM + KV_DIM].reshape(DEC_SEQ, DEC_BATCH, KV_DIM).transpose(1, 0, 2)
    v_new = samp[:, Q_DIM + KV_DIM:N_QKV].reshape(DEC_SEQ, DEC_BATCH, KV_DIM).transpose(1, 0, 2)
    sink_rows = jnp.repeat(attn_sinks[l].astype(_f32), DEC_SEQ).reshape(N_QROWS, 1)
    o_s, new_k_s, new_v_s = _attn_sample(
        q_s, k_new, v_new,
        cache_k_win[l].reshape(DEC_BATCH, WINDOW, KV_DIM),
        cache_v_win[l].reshape(DEC_BATCH, WINDOW, KV_DIM), sink_rows)
    at_rows = (o_s.reshape(DEC_BATCH, N_HEADS, DEC_SEQ, HEAD_DIM)
               .transpose(2, 0, 1, 3).reshape(N_SAMPLE, Q_DIM))
    ats = _to_small(at_rows, _bf16)

    t_main, t_small = _gate(h_all, cb_all, atm, ats, w_in_b, wc_b, wa_b)
    ym, ys = _ffn_stream(x1m, x1s, ffn2_norm[l].reshape(1, D_MODEL), wgu2, wd2,
                         proj=(t_main, t_small, wo_b))

    y_prompt = ym.reshape(BATCH, SEQ, D_MODEL)
    y_sample = (ys[SAMPLE_ROW0:SAMPLE_ROW0 + N_SAMPLE]
                .reshape(DEC_SEQ, DEC_BATCH, D_MODEL).transpose(1, 0, 2))
    kv_shape_p = (1, BATCH, WINDOW, N_KV_HEADS, HEAD_DIM)
    kv_shape_s = (1, DEC_BATCH, WINDOW, N_KV_HEADS, HEAD_DIM)
    return (y_prompt, y_sample,
            new_conv_p[None],
            new_k_p.reshape(kv_shape_p), new_v_p.reshape(kv_shape_p),
            new_conv_s.reshape(CONV_WIDTH - 1, DEC_BATCH, D_CONV).transpose(1, 0, 2)[None],
            new_k_s.reshape(kv_shape_s), new_v_s.reshape(kv_shape_s))
```

```python
import functools
from typing import NamedTuple

import jax
import jax.numpy as jnp
from jax import lax
from jax.experimental import pallas as pl
from jax.experimental.pallas import tpu as pltpu

D_MODEL = 2048
BATCH = 4
SEQ = 2048
DEC_BATCH = 32
DEC_SEQ = 4
PAST_LEN = 16384
N_META = 16
D_CONV = D_MODEL // 2
CONV_WIDTH = 3
HEAD_DIM = 64
N_HEADS = (D_MODEL // 2) // HEAD_DIM
N_KV_HEADS = N_HEADS // 4
GROUP = N_HEADS // N_KV_HEADS
Q_DIM = N_HEADS * HEAD_DIM
KV_DIM = N_KV_HEADS * HEAD_DIM
PAIR = 2 * HEAD_DIM
WINDOW = 128
D_FF = ((8 * D_MODEL // 3 + 127) // 128) * 128
D_IN = 3 * D_CONV + Q_DIM + 2 * KV_DIM + 2 * D_MODEL
EPS = 1e-6
NEG_INF = -1e30
LOG2E = 1.4426950408889634
Q_SCALE = HEAD_DIM ** -0.5 * LOG2E

OFF_XC = 0
OFF_BG = D_CONV
OFF_CG = 2 * D_CONV
OFF_Q = 3 * D_CONV
OFF_K = OFF_Q + Q_DIM
OFF_V = OFF_K + KV_DIM
OFF_GC = OFF_V + KV_DIM
OFF_GA = OFF_GC + D_MODEL
N_QKV = Q_DIM + 2 * KV_DIM

N_MAIN = BATCH * SEQ
N_SAMPLE = DEC_BATCH * DEC_SEQ
N_SMALL = 256
SAMPLE_ROW0 = N_META

N_TILES = 16
TM = N_MAIN // N_TILES
TS = N_SMALL // N_TILES
TILE = TM + TS
N_ALL = N_TILES * TILE
TILES_PER_SEQ = SEQ // TM

TF = 512
D_FF_PAD = ((D_FF + TF - 1) // TF) * TF
N_FF_CHUNKS = D_FF_PAD // TF
FFN_STEPS = N_TILES * N_FF_CHUNKS
TN_OUT = 1024
N_OUT_CHUNKS = D_MODEL // TN_OUT
GATE_PIECE = 512
GATE_PIECES = TN_OUT // GATE_PIECE

QBLK = 128
KPAD = QBLK - N_META
N_QBLK = SEQ // QBLK
KBUF_ROWS = QBLK + SEQ

VMEM_LIMIT = 56 * 1024 * 1024

_bf16 = jnp.bfloat16
_f32 = jnp.float32


def _params(sem):
    return pltpu.CompilerParams(dimension_semantics=sem, vmem_limit_bytes=VMEM_LIMIT)


def _rms_rows(x, g):
    ms = jnp.mean(x * x, axis=-1, keepdims=True)
    return x * lax.rsqrt(ms + EPS) * g


def _pair_rms(x, g_pair):
    low = lax.broadcasted_iota(jnp.int32, x.shape, 1) < HEAD_DIM
    x2 = x * x
    s_lo = jnp.sum(jnp.where(low, x2, 0.0), axis=-1, keepdims=True)
    s_hi = jnp.sum(jnp.where(low, 0.0, x2), axis=-1, keepdims=True)
    ms = jnp.where(low, s_lo, s_hi) * (1.0 / HEAD_DIM)
    return x * lax.rsqrt(ms + EPS) * g_pair


class _CastJob(NamedTuple):
    slab: int
    rows_in: int
    n_in_slabs: int
    n_out_slabs: int
    n_in_cols: int
    n_out_cols: int
    segments: tuple
    zero_ranges: tuple


def _cast_job(rows_in, cols_in, slab, segments=None, zero_ranges=(), rows_out=None, cols_out=None):
    rows_out = rows_in if rows_out is None else rows_out
    cols_out = cols_in if cols_out is None else cols_out
    segments = ((0, cols_in, 0),) if segments is None else segments
    assert rows_out % slab == 0
    return _CastJob(slab, rows_in, -(-rows_in // slab), rows_out // slab, cols_in, cols_out,
                    tuple(segments), tuple(zero_ranges))


def _job_specs(job, step_of):
    in_idx = lambda *ids: (jnp.minimum(step_of(*ids), job.n_in_slabs - 1), 0)
    out_idx = lambda *ids: (jnp.minimum(step_of(*ids), job.n_out_slabs - 1), 0)
    return (pl.BlockSpec((job.slab, job.n_in_cols), in_idx),
            pl.BlockSpec((job.slab, job.n_out_cols), out_idx))


def _run_cast_job(job, step, x_ref, o_ref):
    out_slab = jnp.minimum(step, job.n_out_slabs - 1)
    padded = job.n_out_slabs * job.slab > job.rows_in
    for src, width, dst in job.segments:
        x = x_ref[:, src:src + width]
        if padded:
            row = out_slab * job.slab + lax.broadcasted_iota(jnp.int32, x.shape, 0)
            x = jnp.where(row < job.rows_in, x, 0.0)
        o_ref[:, dst:dst + width] = x.astype(_bf16)
    for lo, hi in job.zero_ranges:
        o_ref[:, lo:hi] = jnp.zeros((job.slab, hi - lo), _bf16)


def _cast_kernel(job, x_ref, o_ref):
    _run_cast_job(job, pl.program_id(0), x_ref, o_ref)


def _cast(job, w):
    in_spec, out_spec = _job_specs(job, lambda i: i)
    return pl.pallas_call(
        functools.partial(_cast_kernel, job),
        grid=(job.n_out_slabs,),
        in_specs=[in_spec],
        out_specs=out_spec,
        out_shape=jax.ShapeDtypeStruct((job.n_out_slabs * job.slab, job.n_out_cols), _bf16),
        compiler_params=_params(("arbitrary",)),
        name="cast",
    )(w)


def _w_up_job(slab):
    segments, zero_ranges = [], []
    for c in range(N_FF_CHUNKS):
        width = min(TF, D_FF - c * TF)
        for half, src0 in enumerate((0, D_FF)):
            dst = (2 * c + half) * TF
            segments.append((src0 + c * TF, width, dst))
            if width < TF:
                zero_ranges.append((dst + width, dst + TF))
    return _cast_job(D_MODEL, 2 * D_FF, slab, segments=segments, zero_ranges=zero_ranges,
                     cols_out=2 * D_FF_PAD)


def _w_down_job(slab):
    return _cast_job(D_FF, D_MODEL, slab, rows_out=D_FF_PAD)


def _ffn_kernel(jobs, with_proj, with_norm_out, *refs):
    refs = list(refs)
    take = lambda n: [refs.pop(0) for _ in range(n)]
    xm_ref, xs_ref = take(2)
    if with_proj:
        tm_ref, ts_ref, wo_ref = take(3)
    g_ref, wgu_ref, wd_ref = take(3)
    if with_norm_out:
        gn_ref, = take(1)
    job_in = take(len(jobs))
    om_ref, os_ref = take(2)
    if with_norm_out:
        hn_ref, = take(1)
    job_out = take(len(jobs))
    h_ref = refs[0]
    i = pl.program_id(0)
    j = pl.program_id(1)

    @pl.when(j == 0)
    def _():
        g = g_ref[...]
        xm = xm_ref[...]
        xs = xs_ref[...]
        if with_proj:
            t_all = refs[1]
            t_all[0:TM, :] = tm_ref[...]
            t_all[TM:, :] = ts_ref[...]
            p = jnp.dot(t_all[...], wo_ref[...], preferred_element_type=_f32)
            xm = xm + p[0:TM, :]
            xs = xs + p[TM:, :]
        h_ref[0:TM, :] = _rms_rows(xm, g).astype(_bf16)
        om_ref[...] = xm
        h_ref[TM:, :] = _rms_rows(xs, g).astype(_bf16)
        os_ref[...] = xs

    for job, x_ref, o_ref in zip(jobs, job_in, job_out):
        _run_cast_job(job, i * N_FF_CHUNKS + j, x_ref, o_ref)

    h = h_ref[...]
    gu = jnp.dot(h, wgu_ref[...], preferred_element_type=_f32)
    gate, up = gu[:, 0:TF], gu[:, TF:2 * TF]
    a = (gate * jax.nn.sigmoid(gate) * up * 0.5).astype(_bf16)
    r = jnp.dot(a, wd_ref[...], preferred_element_type=_f32)
    om_ref[...] += r[0:TM, :]
    os_ref[...] += r[TM:, :]

    if with_norm_out:
        @pl.when(j == N_FF_CHUNKS - 1)
        def _():
            gn = gn_ref[...]
            hn_ref[0:TM, :] = _rms_rows(om_ref[...], gn).astype(_bf16)
            hn_ref[TM:, :] = _rms_rows(os_ref[...], gn).astype(_bf16)


def _ffn(xm, xs, g, wgu, wd, proj=None, next_gain=None, jobs=(), job_inputs=()):
    assert all(job.n_out_slabs <= FFN_STEPS for job in jobs)
    job_specs = [_job_specs(job, lambda i, j: i * N_FF_CHUNKS + j) for job in jobs]
    row = lambda i, j: (i, 0)
    fixed = lambda i, j: (0, 0)
    in_specs = [pl.BlockSpec((TM, D_MODEL), row), pl.BlockSpec((TS, D_MODEL), row)]
    out_specs = [pl.BlockSpec((TM, D_MODEL), row), pl.BlockSpec((TS, D_MODEL), row)]
    out_shape = [jax.ShapeDtypeStruct((N_MAIN, D_MODEL), _f32),
                 jax.ShapeDtypeStruct((N_SMALL, D_MODEL), _f32)]
    scratch = [pltpu.VMEM((TILE, D_MODEL), _bf16)]
    args = [xm, xs]
    if proj is not None:
        in_specs += [pl.BlockSpec((TM, D_MODEL), row), pl.BlockSpec((TS, D_MODEL), row),
                     pl.BlockSpec((D_MODEL, D_MODEL), fixed, pipeline_mode=pl.Buffered(1))]
        scratch += [pltpu.VMEM((TILE, D_MODEL), _bf16)]
        args += list(proj)
    in_specs += [
        pl.BlockSpec((1, D_MODEL), fixed),
        pl.BlockSpec((D_MODEL, 2 * TF), lambda i, j: (0, j)),
        pl.BlockSpec((TF, D_MODEL), lambda i, j: (j, 0)),
    ]
    args += [g, wgu, wd]
    if next_gain is not None:
        in_specs += [pl.BlockSpec((1, D_MODEL), fixed)]
        args += [next_gain]
        out_specs += [pl.BlockSpec((TILE, D_MODEL), row)]
        out_shape += [jax.ShapeDtypeStruct((N_ALL, D_MODEL), _bf16)]
    in_specs += [s[0] for s in job_specs]
    args += list(job_inputs)
    out_specs += [s[1] for s in job_specs]
    out_shape += [jax.ShapeDtypeStruct((job.n_out_slabs * job.slab, job.n_out_cols), _bf16)
                  for job in jobs]
    return pl.pallas_call(
        functools.partial(_ffn_kernel, tuple(jobs), proj is not None, next_gain is not None),
        grid=(N_TILES, N_FF_CHUNKS),
        in_specs=in_specs,
        out_specs=out_specs,
        out_shape=out_shape,
        scratch_shapes=scratch,
        compiler_params=_params(("arbitrary", "arbitrary")),
        name="ffn",
    )(*args)


def _ffn_stream_kernel(xm_ref, xs_ref, tm_ref, ts_ref, wo_ref, g_ref, wgu_hbm, wd_hbm,
                       om_ref, os_ref, h_ref, t_all, wgu_buf, wd_buf, sem):
    i = pl.program_id(0)

    def chunk_copies(chunk, slot):
        return (
            pltpu.make_async_copy(wgu_hbm.at[:, pl.ds(chunk * 2 * TF, 2 * TF)],
                                  wgu_buf.at[slot], sem.at[0, slot]),
            pltpu.make_async_copy(wd_hbm.at[pl.ds(chunk * TF, TF), :],
                                  wd_buf.at[slot], sem.at[1, slot]),
        )

    def start(chunk, slot):
        for c in chunk_copies(chunk, slot):
            c.start()

    @pl.when(i == 0)
    def _():
        start(0, 0)

    g = g_ref[...]
    t_all[0:TM, :] = tm_ref[...]
    t_all[TM:, :] = ts_ref[...]
    p = jnp.dot(t_all[...], wo_ref[...], preferred_element_type=_f32)
    xm = xm_ref[...] + p[0:TM, :]
    xs = xs_ref[...] + p[TM:, :]
    h_ref[0:TM, :] = _rms_rows(xm, g).astype(_bf16)
    om_ref[...] = xm
    h_ref[TM:, :] = _rms_rows(xs, g).astype(_bf16)
    os_ref[...] = xs

    h = h_ref[...]
    for j in range(N_FF_CHUNKS):
        slot = (i * N_FF_CHUNKS + j) % 2
        if j + 1 < N_FF_CHUNKS:
            start(j + 1, 1 - slot)
        else:
            @pl.when(i + 1 < N_TILES)
            def _():
                start(0, 1 - slot)
        for c in chunk_copies(j, slot):
            c.wait()
        gu = jnp.dot(h, wgu_buf[slot], preferred_element_type=_f32)
        gate, up = gu[:, 0:TF], gu[:, TF:2 * TF]
        a = (gate * jax.nn.sigmoid(gate) * up * 0.5).astype(_bf16)
        r = jnp.dot(a, wd_buf[slot], preferred_element_type=_f32)
        om_ref[...] += r[0:TM, :]
        os_ref[...] += r[TM:, :]


def _ffn_stream(xm, xs, g, wgu, wd, proj):
    row = lambda i: (i, 0)
    fixed = lambda i: (0, 0)
    return pl.pallas_call(
        _ffn_stream_kernel,
        grid=(N_TILES,),
        in_specs=[
            pl.BlockSpec((TM, D_MODEL), row),
            pl.BlockSpec((TS, D_MODEL), row),
            pl.BlockSpec((TM, D_MODEL), row),
            pl.BlockSpec((TS, D_MODEL), row),
            pl.BlockSpec((D_MODEL, D_MODEL), fixed, pipeline_mode=pl.Buffered(1)),
            pl.BlockSpec((1, D_MODEL), fixed),
            pl.BlockSpec(memory_space=pl.ANY),
            pl.BlockSpec(memory_space=pl.ANY),
        ],
        out_specs=[pl.BlockSpec((TM, D_MODEL), row), pl.BlockSpec((TS, D_MODEL), row)],
        out_shape=[jax.ShapeDtypeStruct((N_MAIN, D_MODEL), _f32),
                   jax.ShapeDtypeStruct((N_SMALL, D_MODEL), _f32)],
        scratch_shapes=[
            pltpu.VMEM((TILE, D_MODEL), _bf16),
            pltpu.VMEM((TILE, D_MODEL), _bf16),
            pltpu.VMEM((2, D_MODEL, 2 * TF), _bf16),
            pltpu.VMEM((2, TF, D_MODEL), _bf16),
            pltpu.SemaphoreType.DMA((2, 2)),
        ],
        compiler_params=_params(("arbitrary",)),
        name="ffn_stream",
    )(xm, xs, *proj, g, wgu, wd)


N_BRANCH_IN = OFF_GC
CONV_HALO = 8
U_BASE = 112
U_STATE0 = U_BASE + SAMPLE_ROW0 - (CONV_WIDTH - 1) * DEC_BATCH
U_ROWS = U_BASE + N_SMALL
U_LAST0 = U_BASE + SAMPLE_ROW0 + (DEC_SEQ - (CONV_WIDTH - 1)) * DEC_BATCH
assert U_BASE >= (CONV_WIDTH - 1) * DEC_BATCH and U_BASE % TS == 0 and TS == N_META


def _inproj_kernel(h_ref, w_ref, cw_ref, st_ref, qg_ref, kg_ref,
                   cb_ref, q_ref, k_ref, v_ref, sm_ref, ncp_ref, ncs_ref,
                   ubuf, meta_halo, utab):
    i = pl.program_id(0)

    @pl.when(i == 0)
    def _():
        utab[...] = jnp.zeros(utab.shape, _f32)
        utab[U_STATE0:U_STATE0 + (CONV_WIDTH - 1) * DEC_BATCH, :] = st_ref[...]
        meta_halo[...] = jnp.zeros(meta_halo.shape, _f32)
        ubuf[0:CONV_HALO, :] = jnp.zeros((CONV_HALO, D_CONV), _f32)

    h = h_ref[...]
    z_q = jnp.dot(h, w_ref[:, OFF_Q:OFF_K], preferred_element_type=_f32)
    z_kv = jnp.dot(h, w_ref[:, OFF_K:OFF_GC], preferred_element_type=_f32)
    z = jnp.dot(h, w_ref[:, 0:OFF_Q], preferred_element_type=_f32)
    cw = cw_ref[...]
    w0, w1, w2 = cw[0:1, :], cw[1:2, :], cw[2:3, :]
    qg = qg_ref[...]
    kg = kg_ref[...]
    seg = lambda a, off, width: a[:, off:off + width]

    def qkv(rows, q_out, k_out, v_out):
        for t in range(Q_DIM // PAIR):
            q_out(t, _pair_rms(z_q[rows, t * PAIR:(t + 1) * PAIR], qg) * Q_SCALE)
        for t in range(KV_DIM // PAIR):
            k_out(t, _pair_rms(z_kv[rows, t * PAIR:(t + 1) * PAIR], kg))
            v_out(t, z_kv[rows, KV_DIM + t * PAIR:KV_DIM + (t + 1) * PAIR])

    zs = z[TM:, :]
    u_s = seg(zs, OFF_CG, D_CONV) * seg(zs, OFF_XC, D_CONV)
    cur = pl.multiple_of(U_BASE + TS * i, TS)
    slot = pl.multiple_of(jnp.where(i == 0, U_ROWS, cur), TS)
    utab[pl.ds(slot, TS), :] = u_s
    y_s = (w0 * utab[pl.ds(cur - 2 * DEC_BATCH, TS), :]
           + w1 * utab[pl.ds(cur - DEC_BATCH, TS), :] + w2 * u_s)
    cb_ref[TM:, :] = (seg(zs, OFF_BG, D_CONV) * y_s).astype(_bf16)
    meta_halo[...] = jnp.where(i == 0, u_s[N_META - CONV_HALO:N_META, :], meta_halo[...])

    def sm_store(off):
        def store(t, val):
            sm_ref[:, off + t * PAIR:off + (t + 1) * PAIR] = val
        return store
    qkv(slice(TM, TILE), sm_store(0), sm_store(Q_DIM), sm_store(Q_DIM + KV_DIM))

    ncs_ref[...] = utab[U_LAST0:U_LAST0 + (CONV_WIDTH - 1) * DEC_BATCH, :]

    ubuf[0:CONV_HALO, :] = jnp.where(i % TILES_PER_SEQ == 0, meta_halo[...], ubuf[0:CONV_HALO, :])

    zm = z[0:TM, :]
    u = seg(zm, OFF_CG, D_CONV) * seg(zm, OFF_XC, D_CONV)
    ubuf[CONV_HALO:CONV_HALO + TM, :] = u
    u1 = ubuf[CONV_HALO - 1:CONV_HALO - 1 + TM, :]
    u2 = ubuf[CONV_HALO - 2:CONV_HALO - 2 + TM, :]
    cb_ref[0:TM, :] = (seg(zm, OFF_BG, D_CONV) * (w0 * u2 + w1 * u1 + w2 * u)).astype(_bf16)
    tail = ubuf[TM:TM + CONV_HALO, :]
    ubuf[0:CONV_HALO, :] = tail
    ncp_ref[0] = tail[CONV_HALO - (CONV_WIDTH - 1):CONV_HALO, :]

    def lane_store(ref):
        def store(t, val):
            ref[:, t * PAIR:(t + 1) * PAIR] = val
        return store
    qkv(slice(0, TM), lane_store(q_ref), lane_store(k_ref), lane_store(v_ref))


def _inproj(h_all, w_in_b, conv_w, state_rows, q_gain, k_gain):
    qg = jnp.tile(q_gain.reshape(1, HEAD_DIM), (1, 2))
    kg = jnp.tile(k_gain.reshape(1, HEAD_DIM), (1, 2))
    row = lambda i: (i, 0)
    fixed = lambda i: (0, 0)
    return pl.pallas_call(
        _inproj_kernel,
        grid=(N_TILES,),
        in_specs=[
            pl.BlockSpec((TILE, D_MODEL), row),
            pl.BlockSpec((D_MODEL, N_BRANCH_IN), fixed, pipeline_mode=pl.Buffered(1)),
            pl.BlockSpec((CONV_WIDTH, D_CONV), fixed),
            pl.BlockSpec(((CONV_WIDTH - 1) * DEC_BATCH, D_CONV), fixed),
            pl.BlockSpec((1, PAIR), fixed),
            pl.BlockSpec((1, PAIR), fixed),
        ],
        out_specs=[
            pl.BlockSpec((TILE, D_CONV), row),
            pl.BlockSpec((TM, Q_DIM), row),
            pl.BlockSpec((TM, KV_DIM), row),
            pl.BlockSpec((TM, KV_DIM), row),
            pl.BlockSpec((TS, N_QKV), row),
            pl.BlockSpec((1, CONV_WIDTH - 1, D_CONV), lambda i: (i // TILES_PER_SEQ, 0, 0)),
            pl.BlockSpec(((CONV_WIDTH - 1) * DEC_BATCH, D_CONV), fixed),
        ],
        out_shape=[
            jax.ShapeDtypeStruct((N_ALL, D_CONV), _bf16),
            jax.ShapeDtypeStruct((N_MAIN, Q_DIM), _f32),
            jax.ShapeDtypeStruct((N_MAIN, KV_DIM), _f32),
            jax.ShapeDtypeStruct((N_MAIN, KV_DIM), _f32),
            jax.ShapeDtypeStruct((N_SMALL, N_QKV), _f32),
            jax.ShapeDtypeStruct((BATCH, CONV_WIDTH - 1, D_CONV), _f32),
            jax.ShapeDtypeStruct(((CONV_WIDTH - 1) * DEC_BATCH, D_CONV), _f32),
        ],
        scratch_shapes=[
            pltpu.VMEM((CONV_HALO + TM, D_CONV), _f32),
            pltpu.VMEM((CONV_HALO, D_CONV), _f32),
            pltpu.VMEM((U_ROWS + TS, D_CONV), _f32),
        ],
        compiler_params=_params(("arbitrary",)),
        name="inproj",
    )(h_all, w_in_b, conv_w, state_rows, qg, kg)


QPAIR = 2 * GROUP * HEAD_DIM
HEADS_PER_STEP = 2 * GROUP


def _attn_prompt_kernel(sink_ref, q_ref, k_ref, v_ref, km_ref, vm_ref,
                        o_ref, nk_ref, nv_ref, kt_buf, vbuf, bias_ref):
    gp = pl.program_id(1)

    vbuf[0:KPAD, :] = jnp.zeros((KPAD, PAIR), _bf16)
    vbuf[KPAD:QBLK, :] = vm_ref[...].astype(_bf16)
    vbuf[QBLK:KBUF_ROWS, :] = v_ref[...].astype(_bf16)
    first_keys = jnp.concatenate([jnp.zeros((KPAD, PAIR), _f32), km_ref[...]], axis=0)
    kt_buf[0] = first_keys.T.astype(_bf16)
    for b in range(N_QBLK):
        kt_buf[b + 1] = k_ref[b * QBLK:(b + 1) * QBLK, :].T.astype(_bf16)
    nk_ref[0] = k_ref[SEQ - WINDOW:SEQ, :]
    nv_ref[0] = v_ref[SEQ - WINDOW:SEQ, :]

    row = lax.broadcasted_iota(jnp.int32, (QBLK, 2 * QBLK), 0)
    col = lax.broadcasted_iota(jnp.int32, (QBLK, 2 * QBLK), 1)
    dist = QBLK + row - col
    in_window = (dist >= 0) & (dist <= WINDOW)
    distf = dist.astype(_f32)
    upper = lax.broadcasted_iota(jnp.int32, (2 * QBLK, 1), 0) < QBLK
    sink_cols = []
    for hp in range(GROUP):
        sinks = []
        for g in range(2):
            head = gp * HEADS_PER_STEP + g * GROUP + hp
            slope = jnp.exp2(jnp.full((QBLK, 2 * QBLK), -8.0 / N_HEADS, _f32)
                             * (head + 1).astype(_f32))
            bias_ref[hp, g * QBLK:(g + 1) * QBLK, :] = jnp.where(
                in_window, -slope * distf * LOG2E, NEG_INF)
            sinks.append(sink_ref[head] * LOG2E)
        sink_cols.append(jnp.where(upper, sinks[0], sinks[1]))

    lane = lax.broadcasted_iota(jnp.int32, (QBLK, PAIR), 1)
    low_half = lane < HEAD_DIM
    pad_col = lax.broadcasted_iota(jnp.int32, (2 * QBLK, 2 * QBLK), 1) < KPAD

    def block(i, first):
        r0 = i * QBLK if first else pl.multiple_of(i * QBLK, QBLK)
        qn = q_ref[pl.ds(r0 - QBLK, QBLK), :]
        kt = jnp.concatenate([kt_buf[i - 1], kt_buf[i]], axis=1)
        vv = vbuf[pl.ds(r0 - QBLK, 2 * QBLK), :]
        outs = [[None] * GROUP for _ in range(2)]
        for hp in range(GROUP):
            tiles = []
            for g in range(2):
                hh = g * GROUP + hp
                tile = qn[:, (hh // 2) * PAIR:(hh // 2 + 1) * PAIR]
                if hp % 2 != g:
                    tile = pltpu.roll(tile, HEAD_DIM, axis=1)
                keep = low_half if g == 0 else jnp.logical_not(low_half)
                tiles.append(jnp.where(keep, tile, 0.0).astype(_bf16))
            qst = jnp.concatenate(tiles, axis=0)
            s = jnp.dot(qst, kt, preferred_element_type=_f32) + bias_ref[hp]
            if first:
                s = jnp.where(pad_col, NEG_INF, s)
            sink = sink_cols[hp]
            m = jnp.maximum(jnp.max(s, axis=-1, keepdims=True), sink)
            e = jnp.exp2(s - m)
            denom = jnp.sum(e, axis=-1, keepdims=True) + jnp.exp2(sink - m)
            o = jnp.dot(e.astype(_bf16), vv, preferred_element_type=_f32) / denom
            for g in range(2):
                outs[g][hp] = o[g * QBLK:(g + 1) * QBLK, :]
        for g in range(2):
            for pr in range(GROUP // 2):
                even, odd = outs[g][2 * pr], outs[g][2 * pr + 1]
                if g == 0:
                    odd = pltpu.roll(odd, HEAD_DIM, axis=1)
                else:
                    even = pltpu.roll(even, HEAD_DIM, axis=1)
                t = g * (GROUP // 2) + pr
                o_ref[pl.ds(r0 - QBLK, QBLK), t * PAIR:(t + 1) * PAIR] = (
                    jnp.where(low_half, even, odd).astype(_bf16))

    block(1, True)

    def later_block(i, carry):
        block(i, False)
        return carry

    lax.fori_loop(2, N_QBLK + 1, later_block, 0)


def _attn_prompt(qm, km, vm, qkv_s, sinks):
    grid_spec = pltpu.PrefetchScalarGridSpec(
        num_scalar_prefetch=1,
        grid=(BATCH, N_KV_HEADS // 2),
        in_specs=[
            pl.BlockSpec((SEQ, QPAIR), lambda b, p, s: (b, p)),
            pl.BlockSpec((SEQ, PAIR), lambda b, p, s: (b, p)),
            pl.BlockSpec((SEQ, PAIR), lambda b, p, s: (b, p)),
            pl.BlockSpec((N_META, PAIR), lambda b, p, s: (0, Q_DIM // PAIR + p)),
            pl.BlockSpec((N_META, PAIR), lambda b, p, s: (0, (Q_DIM + KV_DIM) // PAIR + p)),
        ],
        out_specs=[
            pl.BlockSpec((SEQ, QPAIR), lambda b, p, s: (b, p)),
            pl.BlockSpec((1, WINDOW, PAIR), lambda b, p, s: (b, 0, p)),
            pl.BlockSpec((1, WINDOW, PAIR), lambda b, p, s: (b, 0, p)),
        ],
        scratch_shapes=[
            pltpu.VMEM((N_QBLK + 1, PAIR, QBLK), _bf16),
            pltpu.VMEM((KBUF_ROWS, PAIR), _bf16),
            pltpu.VMEM((GROUP, 2 * QBLK, 2 * QBLK), _f32),
        ],
    )
    return pl.pallas_call(
        _attn_prompt_kernel,
        grid_spec=grid_spec,
        out_shape=[
            jax.ShapeDtypeStruct((N_MAIN, Q_DIM), _bf16),
            jax.ShapeDtypeStruct((BATCH, WINDOW, KV_DIM), _f32),
            jax.ShapeDtypeStruct((BATCH, WINDOW, KV_DIM), _f32),
        ],
        compiler_params=_params(("arbitrary", "arbitrary")),
        name="attn_prompt",
    )(sinks, qm, km, vm, qkv_s, qkv_s)


SEQ_CHUNK = 8
N_QROWS = N_HEADS * DEC_SEQ
N_KEYS = WINDOW + DEC_SEQ
N_KEYS_PAD = ((N_KEYS + 7) // 8) * 8


def _attn_sample_kernel(q_ref, kn_ref, vn_ref, ck_ref, cv_ref, sink_ref,
                        o_ref, nk_ref, nv_ref, kk_buf, vv_buf):
    q4 = q_ref[...]
    shp = (SEQ_CHUNK, N_QROWS, KV_DIM)
    row_kv = lax.broadcasted_iota(jnp.int32, shp, 1) // (GROUP * DEC_SEQ)
    lane_kv = lax.broadcasted_iota(jnp.int32, shp, 2) // HEAD_DIM
    own = row_kv == lane_kv
    qp = jnp.where(own, q4, 0.0)

    kk_buf[:, 0:WINDOW, :] = ck_ref[...]
    kk_buf[:, WINDOW:N_KEYS, :] = kn_ref[...]
    kk_buf[:, N_KEYS:N_KEYS_PAD, :] = jnp.zeros((SEQ_CHUNK, N_KEYS_PAD - N_KEYS, KV_DIM), _f32)
    vv_buf[:, 0:WINDOW, :] = cv_ref[...]
    vv_buf[:, WINDOW:N_KEYS, :] = vn_ref[...]
    vv_buf[:, N_KEYS:N_KEYS_PAD, :] = jnp.zeros((SEQ_CHUNK, N_KEYS_PAD - N_KEYS, KV_DIM), _f32)
    nk_ref[...] = kk_buf[:, DEC_SEQ:N_KEYS, :]
    nv_ref[...] = vv_buf[:, DEC_SEQ:N_KEYS, :]

    kk = kk_buf[...]
    vv = vv_buf[...]
    s = jnp.einsum('bqd,bkd->bqk', qp, kk, preferred_element_type=_f32)

    row = lax.broadcasted_iota(jnp.int32, (N_QROWS, N_KEYS_PAD), 0)
    col = lax.broadcasted_iota(jnp.int32, (N_QROWS, N_KEYS_PAD), 1)
    step = row % DEC_SEQ
    head = (row // DEC_SEQ + 1).astype(_f32)
    dist = step + WINDOW - col
    valid = (dist >= 0) & (dist <= WINDOW) & (col < N_KEYS)
    slope = jnp.exp2(head * (-8.0 / N_HEADS))
    bias = jnp.where(valid, -slope * dist.astype(_f32) * LOG2E, NEG_INF)
    s = jnp.where(valid[None], s + bias[None], NEG_INF)
    sink = sink_ref[...][None] * LOG2E
    m = jnp.maximum(jnp.max(s, axis=-1, keepdims=True), sink)
    e = jnp.exp2(s - m)
    denom = jnp.sum(e, axis=-1, keepdims=True) + jnp.exp2(sink - m)
    o = jnp.einsum('bqk,bkd->bqd', e, vv, preferred_element_type=_f32) / denom
    o = jnp.where(own, o, 0.0)
    acc = o
    for g in range(1, N_KV_HEADS):
        acc = acc + pltpu.roll(o, g * HEAD_DIM, axis=2)
    o_ref[...] = acc[:, :, 0:HEAD_DIM]


def _attn_sample(q_s, k_new, v_new, cache_k, cache_v, sink_rows):
    c = SEQ_CHUNK
    return pl.pallas_call(
        _attn_sample_kernel,
        grid=(DEC_BATCH // c,),
        in_specs=[
            pl.BlockSpec((c, N_QROWS, KV_DIM), lambda i: (i, 0, 0)),
            pl.BlockSpec((c, DEC_SEQ, KV_DIM), lambda i: (i, 0, 0)),
            pl.BlockSpec((c, DEC_SEQ, KV_DIM), lambda i: (i, 0, 0)),
            pl.BlockSpec((c, WINDOW, KV_DIM), lambda i: (i, 0, 0)),
            pl.BlockSpec((c, WINDOW, KV_DIM), lambda i: (i, 0, 0)),
            pl.BlockSpec((N_QROWS, 1), lambda i: (0, 0)),
        ],
        out_specs=[
            pl.BlockSpec((c, N_QROWS, HEAD_DIM), lambda i: (i, 0, 0)),
            pl.BlockSpec((c, WINDOW, KV_DIM), lambda i: (i, 0, 0)),
            pl.BlockSpec((c, WINDOW, KV_DIM), lambda i: (i, 0, 0)),
        ],
        out_shape=[
            jax.ShapeDtypeStruct((DEC_BATCH, N_QROWS, HEAD_DIM), _f32),
            jax.ShapeDtypeStruct((DEC_BATCH, WINDOW, KV_DIM), _f32),
            jax.ShapeDtypeStruct((DEC_BATCH, WINDOW, KV_DIM), _f32),
        ],
        scratch_shapes=[
            pltpu.VMEM((c, N_KEYS_PAD, KV_DIM), _f32),
            pltpu.VMEM((c, N_KEYS_PAD, KV_DIM), _f32),
        ],
        compiler_params=_params(("arbitrary",)),
        name="attn_sample",
    )(q_s, k_new, v_new, cache_k, cache_v, sink_rows)


def _gate_kernel(*refs):
    h_ref, cb_ref, atm_ref, ats_ref = refs[:4]
    w_refs = refs[4:4 + 4 * GATE_PIECES]
    tm_ref, ts_ref, at_all = refs[4 + 4 * GATE_PIECES:]
    at_all[0:TM, :] = atm_ref[...]
    at_all[TM:, :] = ats_ref[...]
    h = h_ref[...]
    cb = cb_ref[...]
    at = at_all[...]
    for c in range(GATE_PIECES):
        wgc_ref, wga_ref, wc_ref, wa_ref = w_refs[4 * c:4 * c + 4]
        cols = slice(c * GATE_PIECE, (c + 1) * GATE_PIECE)
        gc = jnp.dot(h, wgc_ref[...], preferred_element_type=_f32)
        ga = jnp.dot(h, wga_ref[...], preferred_element_type=_f32)
        yc = jnp.dot(cb, wc_ref[...], preferred_element_type=_f32)
        ya = jnp.dot(at, wa_ref[...], preferred_element_type=_f32)
        t = (jax.nn.sigmoid(gc) * yc + jax.nn.sigmoid(ga) * ya).astype(_bf16)
        tm_ref[:, cols] = t[0:TM, :]
        ts_ref[:, cols] = t[TM:, :]


def _gate(h_all, cb_all, atm, ats, w_in_b, wc, wa):
    row = lambda n, i: (i, 0)
    once = pl.Buffered(1)
    w_specs, w_args = [], []
    for c in range(GATE_PIECES):
        piece = lambda n, i, c=c, base=0: (0, base + n * GATE_PIECES + c)
        w_specs += [
            pl.BlockSpec((D_MODEL, GATE_PIECE),
                         functools.partial(piece, base=OFF_GC // GATE_PIECE), pipeline_mode=once),
            pl.BlockSpec((D_MODEL, GATE_PIECE),
                         functools.partial(piece, base=OFF_GA // GATE_PIECE), pipeline_mode=once),
            pl.BlockSpec((D_CONV, GATE_PIECE), piece, pipeline_mode=once),
            pl.BlockSpec((Q_DIM, GATE_PIECE), piece, pipeline_mode=once),
        ]
        w_args += [w_in_b, w_in_b, wc, wa]
    return pl.pallas_call(
        _gate_kernel,
        grid=(N_OUT_CHUNKS, N_TILES),
        in_specs=[
            pl.BlockSpec((TILE, D_MODEL), row),
            pl.BlockSpec((TILE, D_CONV), row),
            pl.BlockSpec((TM, Q_DIM), row),
            pl.BlockSpec((TS, Q_DIM), row),
        ] + w_specs,
        out_specs=[
            pl.BlockSpec((TM, TN_OUT), lambda n, i: (i, n)),
            pl.BlockSpec((TS, TN_OUT), lambda n, i: (i, n)),
        ],
        out_shape=[
            jax.ShapeDtypeStruct((N_MAIN, D_MODEL), _bf16),
            jax.ShapeDtypeStruct((N_SMALL, D_MODEL), _bf16),
        ],
        scratch_shapes=[pltpu.VMEM((TILE, Q_DIM), _bf16)],
        compiler_params=_params(("arbitrary", "arbitrary")),
        name="gate",
    )(h_all, cb_all, atm, ats, *w_args)


def _sample_rows(a):
    return a.transpose(1, 0, 2).reshape(N_SAMPLE, a.shape[-1])


def _to_small(sample_rows, dtype):
    w = sample_rows.shape[-1]
    return jnp.concatenate([
        jnp.zeros((SAMPLE_ROW0, w), dtype), sample_rows.astype(dtype),
        jnp.zeros((N_SMALL - SAMPLE_ROW0 - N_SAMPLE, w), dtype)], axis=0)


def kernel(x_prompt, x_sample, state_conv, cache_k_win, cache_v_win, meta_tokens, ffn1_norm, ffn1_w_up, ffn1_w_down, mix_norm, w_in, q_norm, k_norm, conv_w, w_conv_out, attn_sinks, w_attn_out, w_o, ffn2_norm, ffn2_w_up, ffn2_w_down):
    l = 0
    xm = x_prompt.reshape(N_MAIN, D_MODEL)
    xs = jnp.concatenate([
        meta_tokens.astype(_f32), _sample_rows(x_sample),
        jnp.zeros((N_SMALL - N_META - N_SAMPLE, D_MODEL), _f32)], axis=0)

    wgu1 = _cast(_w_up_job(256), ffn1_w_up[l])
    wd1 = _cast(_w_down_job(TF), ffn1_w_down[l])
    jobs = (_w_up_job(16), _w_down_job(32), _cast_job(D_MODEL, D_IN, 16),
            _cast_job(D_CONV, D_MODEL, 16), _cast_job(Q_DIM, D_MODEL, 16),
            _cast_job(D_MODEL, D_MODEL, 16))
    x1m, x1s, h_all, wgu2, wd2, w_in_b, wc_b, wa_b, wo_b = _ffn(
        xm, xs, ffn1_norm[l].reshape(1, D_MODEL), wgu1, wd1,
        next_gain=mix_norm[l].reshape(1, D_MODEL), jobs=jobs,
        job_inputs=(ffn2_w_up[l], ffn2_w_down[l], w_in[l], w_conv_out[l], w_attn_out[l], w_o[l]))

    state_rows = state_conv[l].transpose(1, 0, 2).reshape((CONV_WIDTH - 1) * DEC_BATCH, D_CONV)
    cb_all, qm, km, vm, qkv_s, new_conv_p, new_conv_s = _inproj(
        h_all, w_in_b, conv_w[l], state_rows, q_norm[l], k_norm[l])

    atm, new_k_p, new_v_p = _attn_prompt(qm, km, vm, qkv_s, attn_sinks[l])
    samp = qkv_s[SAMPLE_ROW0:SAMPLE_ROW0 + N_SAMPLE]
    q_s = (samp[:, 0:Q_DIM].reshape(DEC_SEQ, DEC_BATCH, N_HEADS, HEAD_DIM)
           .transpose(1, 2, 0, 3).reshape(DEC_BATCH, N_QROWS, HEAD_DIM))
    q_s = jnp.tile(q_s, (1, 1, N_KV_HEADS))
    k_new = samp[:, Q_DIM:Q_DIM + KV_DIM].reshape(DEC_SEQ, DEC_BATCH, KV_DIM).transpose(1, 0, 2)
    v_new = samp[:, Q_DIM + KV_DIM:N_QKV].reshape(DEC_SEQ, DEC_BATCH, KV_DIM).transpose(1, 0, 2)
    sink_rows = jnp.repeat(attn_sinks[l].astype(_f32), DEC_SEQ).reshape(N_QROWS, 1)
    o_s, new_k_s, new_v_s = _attn_sample(
        q_s, k_new, v_new,
        cache_k_win[l].reshape(DEC_BATCH, WINDOW, KV_DIM),
        cache_v_win[l].reshape(DEC_BATCH, WINDOW, KV_DIM), sink_rows)
    at_rows = (o_s.reshape(DEC_BATCH, N_HEADS, DEC_SEQ, HEAD_DIM)
               .transpose(2, 0, 1, 3).reshape(N_SAMPLE, Q_DIM))
    ats = _to_small(at_rows, _bf16)

    t_main, t_small = _gate(h_all, cb_all, atm, ats, w_in_b, wc_b, wa_b)
    ym, ys = _ffn_stream(x1m, x1s, ffn2_norm[l].reshape(1, D_MODEL), wgu2, wd2,
                         proj=(t_main, t_small, wo_b))

    y_prompt = ym.reshape(BATCH, SEQ, D_MODEL)
    y_sample = (ys[SAMPLE_ROW0:SAMPLE_ROW0 + N_SAMPLE]
                .reshape(DEC_SEQ, DEC_BATCH, D_MODEL).transpose(1, 0, 2))
    kv_shape_p = (1, BATCH, WINDOW, N_KV_HEADS, HEAD_DIM)
    kv_shape_s = (1, DEC_BATCH, WINDOW, N_KV_HEADS, HEAD_DIM)
    return (y_prompt, y_sample,
            new_conv_p[None],
            new_k_p.reshape(kv_shape_p), new_v_p.reshape(kv_shape_p),
            new_conv_s.reshape(CONV_WIDTH - 1, DEC_BATCH, D_CONV).transpose(1, 0, 2)[None],
            new_k_s.reshape(kv_shape_s), new_v_s.reshape(kv_shape_s))
```

```python
import functools
from typing import NamedTuple

import jax
import jax.numpy as jnp
from jax import lax
from jax.experimental import pallas as pl
from jax.experimental.pallas import tpu as pltpu

D_MODEL = 2048
BATCH = 4
SEQ = 2048
DEC_BATCH = 32
DEC_SEQ = 4
PAST_LEN = 16384
N_META = 16
D_CONV = D_MODEL // 2
CONV_WIDTH = 3
HEAD_DIM = 64
N_HEADS = (D_MODEL // 2) // HEAD_DIM
N_KV_HEADS = N_HEADS // 4
GROUP = N_HEADS // N_KV_HEADS
Q_DIM = N_HEADS * HEAD_DIM
KV_DIM = N_KV_HEADS * HEAD_DIM
PAIR = 2 * HEAD_DIM
WINDOW = 128
D_FF = ((8 * D_MODEL // 3 + 127) // 128) * 128
D_IN = 3 * D_CONV + Q_DIM + 2 * KV_DIM + 2 * D_MODEL
EPS = 1e-6
NEG_INF = -1e30
LOG2E = 1.4426950408889634
Q_SCALE = HEAD_DIM ** -0.5 * LOG2E

OFF_XC = 0
OFF_BG = D_CONV
OFF_CG = 2 * D_CONV
OFF_Q = 3 * D_CONV
OFF_K = OFF_Q + Q_DIM
OFF_V = OFF_K + KV_DIM
OFF_GC = OFF_V + KV_DIM
OFF_GA = OFF_GC + D_MODEL
N_QKV = Q_DIM + 2 * KV_DIM

N_MAIN = BATCH * SEQ
N_SAMPLE = DEC_BATCH * DEC_SEQ
N_SMALL = 256
SAMPLE_ROW0 = N_META

N_TILES = 16
TM = N_MAIN // N_TILES
TS = N_SMALL // N_TILES
TILE = TM + TS
N_ALL = N_TILES * TILE
TILES_PER_SEQ = SEQ // TM

TF = 512
D_FF_PAD = ((D_FF + TF - 1) // TF) * TF
N_FF_CHUNKS = D_FF_PAD // TF
FFN_STEPS = N_TILES * N_FF_CHUNKS
TN_OUT = 1024
N_OUT_CHUNKS = D_MODEL // TN_OUT
GATE_PIECE = 512
GATE_PIECES = TN_OUT // GATE_PIECE

QBLK = 128
KPAD = QBLK - N_META
N_QBLK = SEQ // QBLK
KBUF_ROWS = QBLK + SEQ

VMEM_LIMIT = 56 * 1024 * 1024

_bf16 = jnp.bfloat16
_f32 = jnp.float32


def _params(sem):
    return pltpu.CompilerParams(dimension_semantics=sem, vmem_limit_bytes=VMEM_LIMIT)


def _rms_rows(x, g):
    ms = jnp.mean(x * x, axis=-1, keepdims=True)
    return x * lax.rsqrt(ms + EPS) * g


def _pair_rms(x, g_pair):
    low = lax.broadcasted_iota(jnp.int32, x.shape, 1) < HEAD_DIM
    x2 = x * x
    s_lo = jnp.sum(jnp.where(low, x2, 0.0), axis=-1, keepdims=True)
    s_hi = jnp.sum(jnp.where(low, 0.0, x2), axis=-1, keepdims=True)
    ms = jnp.where(low, s_lo, s_hi) * (1.0 / HEAD_DIM)
    return x * lax.rsqrt(ms + EPS) * g_pair


class _CastJob(NamedTuple):
    slab: int
    rows_in: int
    n_in_slabs: int
    n_out_slabs: int
    n_in_cols: int
    n_out_cols: int
    segments: tuple
    zero_ranges: tuple


def _cast_job(rows_in, cols_in, slab, segments=None, zero_ranges=(), rows_out=None, cols_out=None):
    rows_out = rows_in if rows_out is None else rows_out
    cols_out = cols_in if cols_out is None else cols_out
    segments = ((0, cols_in, 0),) if segments is None else segments
    assert rows_out % slab == 0
    return _CastJob(slab, rows_in, -(-rows_in // slab), rows_out // slab, cols_in, cols_out,
                    tuple(segments), tuple(zero_ranges))


def _job_specs(job, step_of):
    in_idx = lambda *ids: (jnp.minimum(step_of(*ids), job.n_in_slabs - 1), 0)
    out_idx = lambda *ids: (jnp.minimum(step_of(*ids), job.n_out_slabs - 1), 0)
    return (pl.BlockSpec((job.slab, job.n_in_cols), in_idx),
            pl.BlockSpec((job.slab, job.n_out_cols), out_idx))


def _run_cast_job(job, step, x_ref, o_ref):
    out_slab = jnp.minimum(step, job.n_out_slabs - 1)
    padded = job.n_out_slabs * job.slab > job.rows_in
    for src, width, dst in job.segments:
        x = x_ref[:, src:src + width]
        if padded:
            row = out_slab * job.slab + lax.broadcasted_iota(jnp.int32, x.shape, 0)
            x = jnp.where(row < job.rows_in, x, 0.0)
        o_ref[:, dst:dst + width] = x.astype(_bf16)
    for lo, hi in job.zero_ranges:
        o_ref[:, lo:hi] = jnp.zeros((job.slab, hi - lo), _bf16)


def _cast_kernel(job, x_ref, o_ref):
    _run_cast_job(job, pl.program_id(0), x_ref, o_ref)


def _cast(job, w):
    in_spec, out_spec = _job_specs(job, lambda i: i)
    return pl.pallas_call(
        functools.partial(_cast_kernel, job),
        grid=(job.n_out_slabs,),
        in_specs=[in_spec],
        out_specs=out_spec,
        out_shape=jax.ShapeDtypeStruct((job.n_out_slabs * job.slab, job.n_out_cols), _bf16),
        compiler_params=_params(("arbitrary",)),
        name="cast",
    )(w)


def _w_up_job(slab):
    segments, zero_ranges = [], []
    for c in range(N_FF_CHUNKS):
        width = min(TF, D_FF - c * TF)
        for half, src0 in enumerate((0, D_FF)):
            dst = (2 * c + half) * TF
            segments.append((src0 + c * TF, width, dst))
            if width < TF:
                zero_ranges.append((dst + width, dst + TF))
    return _cast_job(D_MODEL, 2 * D_FF, slab, segments=segments, zero_ranges=zero_ranges,
                     cols_out=2 * D_FF_PAD)


def _w_down_job(slab):
    return _cast_job(D_FF, D_MODEL, slab, rows_out=D_FF_PAD)


def _ffn_kernel(jobs, with_proj, with_norm_out, *refs):
    refs = list(refs)
    take = lambda n: [refs.pop(0) for _ in range(n)]
    xm_ref, xs_ref = take(2)
    if with_proj:
        tm_ref, ts_ref, wo_ref = take(3)
    g_ref, wgu_ref, wd_ref = take(3)
    if with_norm_out:
        gn_ref, = take(1)
    job_in = take(len(jobs))
    om_ref, os_ref = take(2)
    if with_norm_out:
        hn_ref, = take(1)
    job_out = take(len(jobs))
    h_ref = refs[0]
    i = pl.program_id(0)
    j = pl.program_id(1)

    @pl.when(j == 0)
    def _():
        g = g_ref[...]
        xm = xm_ref[...]
        xs = xs_ref[...]
        if with_proj:
            t_all = refs[1]
            t_all[0:TM, :] = tm_ref[...]
            t_all[TM:, :] = ts_ref[...]
            p = jnp.dot(t_all[...], wo_ref[...], preferred_element_type=_f32)
            xm = xm + p[0:TM, :]
            xs = xs + p[TM:, :]
        h_ref[0:TM, :] = _rms_rows(xm, g).astype(_bf16)
        om_ref[...] = xm
        h_ref[TM:, :] = _rms_rows(xs, g).astype(_bf16)
        os_ref[...] = xs

    for job, x_ref, o_ref in zip(jobs, job_in, job_out):
        _run_cast_job(job, i * N_FF_CHUNKS + j, x_ref, o_ref)

    h = h_ref[...]
    gu = jnp.dot(h, wgu_ref[...], preferred_element_type=_f32)
    gate, up = gu[:, 0:TF], gu[:, TF:2 * TF]
    a = (gate * jax.nn.sigmoid(gate) * up * 0.5).astype(_bf16)
    r = jnp.dot(a, wd_ref[...], preferred_element_type=_f32)
    om_ref[...] += r[0:TM, :]
    os_ref[...] += r[TM:, :]

    if with_norm_out:
        @pl.when(j == N_FF_CHUNKS - 1)
        def _():
            gn = gn_ref[...]
            hn_ref[0:TM, :] = _rms_rows(om_ref[...], gn).astype(_bf16)
            hn_ref[TM:, :] = _rms_rows(os_ref[...], gn).astype(_bf16)


def _ffn(xm, xs, g, wgu, wd, proj=None, next_gain=None, jobs=(), job_inputs=()):
    assert all(job.n_out_slabs <= FFN_STEPS for job in jobs)
    job_specs = [_job_specs(job, lambda i, j: i * N_FF_CHUNKS + j) for job in jobs]
    row = lambda i, j: (i, 0)
    fixed = lambda i, j: (0, 0)
    in_specs = [pl.BlockSpec((TM, D_MODEL), row), pl.BlockSpec((TS, D_MODEL), row)]
    out_specs = [pl.BlockSpec((TM, D_MODEL), row), pl.BlockSpec((TS, D_MODEL), row)]
    out_shape = [jax.ShapeDtypeStruct((N_MAIN, D_MODEL), _f32),
                 jax.ShapeDtypeStruct((N_SMALL, D_MODEL), _f32)]
    scratch = [pltpu.VMEM((TILE, D_MODEL), _bf16)]
    args = [xm, xs]
    if proj is not None:
        in_specs += [pl.BlockSpec((TM, D_MODEL), row), pl.BlockSpec((TS, D_MODEL), row),
                     pl.BlockSpec((D_MODEL, D_MODEL), fixed, pipeline_mode=pl.Buffered(1))]
        scratch += [pltpu.VMEM((TILE, D_MODEL), _bf16)]
        args += list(proj)
    in_specs += [
        pl.BlockSpec((1, D_MODEL), fixed),
        pl.BlockSpec((D_MODEL, 2 * TF), lambda i, j: (0, j)),
        pl.BlockSpec((TF, D_MODEL), lambda i, j: (j, 0)),
    ]
    args += [g, wgu, wd]
    if next_gain is not None:
        in_specs += [pl.BlockSpec((1, D_MODEL), fixed)]
        args += [next_gain]
        out_specs += [pl.BlockSpec((TILE, D_MODEL), row)]
        out_shape += [jax.ShapeDtypeStruct((N_ALL, D_MODEL), _bf16)]
    in_specs += [s[0] for s in job_specs]
    args += list(job_inputs)
    out_specs += [s[1] for s in job_specs]
    out_shape += [jax.ShapeDtypeStruct((job.n_out_slabs * job.slab, job.n_out_cols), _bf16)
                  for job in jobs]
    return pl.pallas_call(
        functools.partial(_ffn_kernel, tuple(jobs), proj is not None, next_gain is not None),
        grid=(N_TILES, N_FF_CHUNKS),
        in_specs=in_specs,
        out_specs=out_specs,
        out_shape=out_shape,
        scratch_shapes=scratch,
        compiler_params=_params(("arbitrary", "arbitrary")),
        name="ffn",
    )(*args)


def _ffn_stream_kernel(xm_ref, xs_ref, tm_ref, ts_ref, wo_ref, g_ref, wgu_hbm, wd_hbm,
                       om_ref, os_ref, h_ref, t_all, wgu_buf, wd_buf, sem):
    i = pl.program_id(0)

    def chunk_copies(chunk, slot):
        return (
            pltpu.make_async_copy(wgu_hbm.at[:, pl.ds(chunk * 2 * TF, 2 * TF)],
                                  wgu_buf.at[slot], sem.at[0, slot]),
            pltpu.make_async_copy(wd_hbm.at[pl.ds(chunk * TF, TF), :],
                                  wd_buf.at[slot], sem.at[1, slot]),
        )

    def start(chunk, slot):
        for c in chunk_copies(chunk, slot):
            c.start()

    @pl.when(i == 0)
    def _():
        start(0, 0)

    g = g_ref[...]
    t_all[0:TM, :] = tm_ref[...]
    t_all[TM:, :] = ts_ref[...]
    p = jnp.dot(t_all[...], wo_ref[...], preferred_element_type=_f32)
    xm = xm_ref[...] + p[0:TM, :]
    xs = xs_ref[...] + p[TM:, :]
    h_ref[0:TM, :] = _rms_rows(xm, g).astype(_bf16)
    om_ref[...] = xm
    h_ref[TM:, :] = _rms_rows(xs, g).astype(_bf16)
    os_ref[...] = xs

    h = h_ref[...]
    for j in range(N_FF_CHUNKS):
        slot = (i * N_FF_CHUNKS + j) % 2
        if j + 1 < N_FF_CHUNKS:
            start(j + 1, 1 - slot)
        else:
            @pl.when(i + 1 < N_TILES)
            def _():
                start(0, 1 - slot)
        for c in chunk_copies(j, slot):
            c.wait()
        gu = jnp.dot(h, wgu_buf[slot], preferred_element_type=_f32)
        gate, up = gu[:, 0:TF], gu[:, TF:2 * TF]
        a = (gate * jax.nn.sigmoid(gate) * up * 0.5).astype(_bf16)
        r = jnp.dot(a, wd_buf[slot], preferred_element_type=_f32)
        om_ref[...] += r[0:TM, :]
        os_ref[...] += r[TM:, :]


def _ffn_stream(xm, xs, g, wgu, wd, proj):
    row = lambda i: (i, 0)
    fixed = lambda i: (0, 0)
    return pl.pallas_call(
        _ffn_stream_kernel,
        grid=(N_TILES,),
        in_specs=[
            pl.BlockSpec((TM, D_MODEL), row),
            pl.BlockSpec((TS, D_MODEL), row),
            pl.BlockSpec((TM, D_MODEL), row),
            pl.BlockSpec((TS, D_MODEL), row),
            pl.BlockSpec((D_MODEL, D_MODEL), fixed, pipeline_mode=pl.Buffered(1)),
            pl.BlockSpec((1, D_MODEL), fixed),
            pl.BlockSpec(memory_space=pl.ANY),
            pl.BlockSpec(memory_space=pl.ANY),
        ],
        out_specs=[pl.BlockSpec((TM, D_MODEL), row), pl.BlockSpec((TS, D_MODEL), row)],
        out_shape=[jax.ShapeDtypeStruct((N_MAIN, D_MODEL), _f32),
                   jax.ShapeDtypeStruct((N_SMALL, D_MODEL), _f32)],
        scratch_shapes=[
            pltpu.VMEM((TILE, D_MODEL), _bf16),
            pltpu.VMEM((TILE, D_MODEL), _bf16),
            pltpu.VMEM((2, D_MODEL, 2 * TF), _bf16),
            pltpu.VMEM((2, TF, D_MODEL), _bf16),
            pltpu.SemaphoreType.DMA((2, 2)),
        ],
        compiler_params=_params(("arbitrary",)),
        name="ffn_stream",
    )(xm, xs, *proj, g, wgu, wd)


N_BRANCH_IN = OFF_GC
CONV_HALO = 8
U_BASE = 112
U_STATE0 = U_BASE + SAMPLE_ROW0 - (CONV_WIDTH - 1) * DEC_BATCH
U_ROWS = U_BASE + N_SMALL
U_LAST0 = U_BASE + SAMPLE_ROW0 + (DEC_SEQ - (CONV_WIDTH - 1)) * DEC_BATCH
assert U_BASE >= (CONV_WIDTH - 1) * DEC_BATCH and U_BASE % TS == 0 and TS == N_META


def _inproj_kernel(h_ref, w_ref, cw_ref, st_ref, qg_ref, kg_ref,
                   cb_ref, q_ref, k_ref, v_ref, sm_ref, ncp_ref, ncs_ref,
                   ubuf, meta_halo, utab):
    i = pl.program_id(0)

    @pl.when(i == 0)
    def _():
        utab[...] = jnp.zeros(utab.shape, _f32)
        utab[U_STATE0:U_STATE0 + (CONV_WIDTH - 1) * DEC_BATCH, :] = st_ref[...]
        meta_halo[...] = jnp.zeros(meta_halo.shape, _f32)
        ubuf[0:CONV_HALO, :] = jnp.zeros((CONV_HALO, D_CONV), _f32)

    h = h_ref[...]
    col_dot = lambda off, width: jnp.dot(h, w_ref[:, off:off + width],
                                         preferred_element_type=_f32)
    z_q = col_dot(OFF_Q, Q_DIM)
    z_kv = col_dot(OFF_K, 2 * KV_DIM)
    z_xc = col_dot(OFF_XC, D_CONV)
    z_cg = col_dot(OFF_CG, D_CONV)
    z_bg = col_dot(OFF_BG, D_CONV)
    cw = cw_ref[...]
    w0, w1, w2 = cw[0:1, :], cw[1:2, :], cw[2:3, :]
    qg = qg_ref[...]
    kg = kg_ref[...]

    def qkv(rows, q_out, k_out, v_out):
        for t in range(Q_DIM // PAIR):
            q_out(t, _pair_rms(z_q[rows, t * PAIR:(t + 1) * PAIR], qg) * Q_SCALE)
        for t in range(KV_DIM // PAIR):
            k_out(t, _pair_rms(z_kv[rows, t * PAIR:(t + 1) * PAIR], kg))
            v_out(t, z_kv[rows, KV_DIM + t * PAIR:KV_DIM + (t + 1) * PAIR])

    u_s = z_cg[TM:, :] * z_xc[TM:, :]
    cur = pl.multiple_of(U_BASE + TS * i, TS)
    slot = pl.multiple_of(jnp.where(i == 0, U_ROWS, cur), TS)
    utab[pl.ds(slot, TS), :] = u_s
    y_s = (w0 * utab[pl.ds(cur - 2 * DEC_BATCH, TS), :]
           + w1 * utab[pl.ds(cur - DEC_BATCH, TS), :] + w2 * u_s)
    cb_ref[TM:, :] = (z_bg[TM:, :] * y_s).astype(_bf16)
    meta_halo[...] = jnp.where(i == 0, u_s[N_META - CONV_HALO:N_META, :], meta_halo[...])

    def sm_store(off):
        def store(t, val):
            sm_ref[:, off + t * PAIR:off + (t + 1) * PAIR] = val
        return store
    qkv(slice(TM, TILE), sm_store(0), sm_store(Q_DIM), sm_store(Q_DIM + KV_DIM))

    ncs_ref[...] = utab[U_LAST0:U_LAST0 + (CONV_WIDTH - 1) * DEC_BATCH, :]

    ubuf[0:CONV_HALO, :] = jnp.where(i % TILES_PER_SEQ == 0, meta_halo[...], ubuf[0:CONV_HALO, :])

    u = z_cg[0:TM, :] * z_xc[0:TM, :]
    ubuf[CONV_HALO:CONV_HALO + TM, :] = u
    u1 = ubuf[CONV_HALO - 1:CONV_HALO - 1 + TM, :]
    u2 = ubuf[CONV_HALO - 2:CONV_HALO - 2 + TM, :]
    cb_ref[0:TM, :] = (z_bg[0:TM, :] * (w0 * u2 + w1 * u1 + w2 * u)).astype(_bf16)
    tail = ubuf[TM:TM + CONV_HALO, :]
    ubuf[0:CONV_HALO, :] = tail
    ncp_ref[0] = tail[CONV_HALO - (CONV_WIDTH - 1):CONV_HALO, :]

    def lane_store(ref):
        def store(t, val):
            ref[:, t * PAIR:(t + 1) * PAIR] = val
        return store
    qkv(slice(0, TM), lane_store(q_ref), lane_store(k_ref), lane_store(v_ref))


def _inproj(h_all, w_in_b, conv_w, state_rows, q_gain, k_gain):
    qg = jnp.tile(q_gain.reshape(1, HEAD_DIM), (1, 2))
    kg = jnp.tile(k_gain.reshape(1, HEAD_DIM), (1, 2))
    row = lambda i: (i, 0)
    fixed = lambda i: (0, 0)
    return pl.pallas_call(
        _inproj_kernel,
        grid=(N_TILES,),
        in_specs=[
            pl.BlockSpec((TILE, D_MODEL), row),
            pl.BlockSpec((D_MODEL, N_BRANCH_IN), fixed, pipeline_mode=pl.Buffered(1)),
            pl.BlockSpec((CONV_WIDTH, D_CONV), fixed),
            pl.BlockSpec(((CONV_WIDTH - 1) * DEC_BATCH, D_CONV), fixed),
            pl.BlockSpec((1, PAIR), fixed),
            pl.BlockSpec((1, PAIR), fixed),
        ],
        out_specs=[
            pl.BlockSpec((TILE, D_CONV), row),
            pl.BlockSpec((TM, Q_DIM), row),
            pl.BlockSpec((TM, KV_DIM), row),
            pl.BlockSpec((TM, KV_DIM), row),
            pl.BlockSpec((TS, N_QKV), row),
            pl.BlockSpec((1, CONV_WIDTH - 1, D_CONV), lambda i: (i // TILES_PER_SEQ, 0, 0)),
            pl.BlockSpec(((CONV_WIDTH - 1) * DEC_BATCH, D_CONV), fixed),
        ],
        out_shape=[
            jax.ShapeDtypeStruct((N_ALL, D_CONV), _bf16),
            jax.ShapeDtypeStruct((N_MAIN, Q_DIM), _f32),
            jax.ShapeDtypeStruct((N_MAIN, KV_DIM), _f32),
            jax.ShapeDtypeStruct((N_MAIN, KV_DIM), _f32),
            jax.ShapeDtypeStruct((N_SMALL, N_QKV), _f32),
            jax.ShapeDtypeStruct((BATCH, CONV_WIDTH - 1, D_CONV), _f32),
            jax.ShapeDtypeStruct(((CONV_WIDTH - 1) * DEC_BATCH, D_CONV), _f32),
        ],
        scratch_shapes=[
            pltpu.VMEM((CONV_HALO + TM, D_CONV), _f32),
            pltpu.VMEM((CONV_HALO, D_CONV), _f32),
            pltpu.VMEM((U_ROWS + TS, D_CONV), _f32),
        ],
        compiler_params=_params(("arbitrary",)),
        name="inproj",
    )(h_all, w_in_b, conv_w, state_rows, qg, kg)


QPAIR = 2 * GROUP * HEAD_DIM
HEADS_PER_STEP = 2 * GROUP


def _attn_prompt_kernel(sink_ref, q_ref, k_ref, v_ref, km_ref, vm_ref,
                        o_ref, nk_ref, nv_ref, kt_buf, vbuf, bias_ref):
    gp = pl.program_id(1)

    for order in range(2):
        swap = (lambda x: x) if order == 0 else (lambda x: pltpu.roll(x, HEAD_DIM, axis=1))
        vbuf[order, 0:KPAD, :] = jnp.zeros((KPAD, PAIR), _bf16)
        vbuf[order, KPAD:QBLK, :] = swap(vm_ref[...]).astype(_bf16)
        vbuf[order, QBLK:KBUF_ROWS, :] = swap(v_ref[...]).astype(_bf16)
        first_keys = jnp.concatenate([jnp.zeros((KPAD, PAIR), _f32), km_ref[...]], axis=0)
        kt_buf[order, 0] = swap(first_keys).T.astype(_bf16)
        for b in range(N_QBLK):
            kt_buf[order, b + 1] = swap(k_ref[b * QBLK:(b + 1) * QBLK, :]).T.astype(_bf16)
    nk_ref[0] = k_ref[SEQ - WINDOW:SEQ, :]
    nv_ref[0] = v_ref[SEQ - WINDOW:SEQ, :]

    partner = lambda p: p ^ 1
    row = lax.broadcasted_iota(jnp.int32, (QBLK, 2 * QBLK), 0)
    col = lax.broadcasted_iota(jnp.int32, (QBLK, 2 * QBLK), 1)
    dist = QBLK + row - col
    in_window = (dist >= 0) & (dist <= WINDOW)
    distf = dist.astype(_f32)
    upper = lax.broadcasted_iota(jnp.int32, (2 * QBLK, 1), 0) < QBLK
    sink_cols = []
    for p in range(GROUP):
        sinks = []
        for g, hp in ((0, p), (1, partner(p))):
            head = gp * HEADS_PER_STEP + g * GROUP + hp
            slope = jnp.exp2(jnp.full((QBLK, 2 * QBLK), -8.0 / N_HEADS, _f32)
                             * (head + 1).astype(_f32))
            bias_ref[p, g * QBLK:(g + 1) * QBLK, :] = jnp.where(
                in_window, -slope * distf * LOG2E, NEG_INF)
            sinks.append(sink_ref[head] * LOG2E)
        sink_cols.append(jnp.where(upper, sinks[0], sinks[1]))

    lane = lax.broadcasted_iota(jnp.int32, (QBLK, PAIR), 1)
    low_half = lane < HEAD_DIM
    pad_col = lax.broadcasted_iota(jnp.int32, (2 * QBLK, 2 * QBLK), 1) < KPAD

    def block(i, first):
        r0 = i * QBLK if first else pl.multiple_of(i * QBLK, QBLK)
        qn = q_ref[pl.ds(r0 - QBLK, QBLK), :]
        outs = [[None] * GROUP for _ in range(2)]
        for p in range(GROUP):
            order = p % 2
            kt = jnp.concatenate([kt_buf[order, i - 1], kt_buf[order, i]], axis=1)
            vv = vbuf[order, pl.ds(r0 - QBLK, 2 * QBLK), :]
            tiles = []
            for g, hp in ((0, p), (1, partner(p))):
                hh = g * GROUP + hp
                in_low = hp % 2 == 0
                tile = qn[:, (hh // 2) * PAIR:(hh // 2 + 1) * PAIR]
                tiles.append(jnp.where(low_half if in_low else jnp.logical_not(low_half),
                                       tile, 0.0).astype(_bf16))
            qst = jnp.concatenate(tiles, axis=0)
            s = jnp.dot(qst, kt, preferred_element_type=_f32) + bias_ref[p]
            if first:
                s = jnp.where(pad_col, NEG_INF, s)
            sink = sink_cols[p]
            m = jnp.maximum(jnp.max(s, axis=-1, keepdims=True), sink)
            e = jnp.exp2(s - m)
            denom = jnp.sum(e, axis=-1, keepdims=True) + jnp.exp2(sink - m)
            o = jnp.dot(e.astype(_bf16), vv, preferred_element_type=_f32) / denom
            outs[0][p] = o[0:QBLK, :]
            outs[1][partner(p)] = o[QBLK:2 * QBLK, :]
        for g in range(2):
            for pr in range(GROUP // 2):
                t = g * (GROUP // 2) + pr
                o_ref[pl.ds(r0 - QBLK, QBLK), t * PAIR:(t + 1) * PAIR] = jnp.where(
                    low_half, outs[g][2 * pr], outs[g][2 * pr + 1]).astype(_bf16)

    block(1, True)

    def later_block(i, carry):
        block(i, False)
        return carry

    lax.fori_loop(2, N_QBLK + 1, later_block, 0)


def _attn_prompt(qm, km, vm, qkv_s, sinks):
    grid_spec = pltpu.PrefetchScalarGridSpec(
        num_scalar_prefetch=1,
        grid=(BATCH, N_KV_HEADS // 2),
        in_specs=[
            pl.BlockSpec((SEQ, QPAIR), lambda b, p, s: (b, p)),
            pl.BlockSpec((SEQ, PAIR), lambda b, p, s: (b, p)),
            pl.BlockSpec((SEQ, PAIR), lambda b, p, s: (b, p)),
            pl.BlockSpec((N_META, PAIR), lambda b, p, s: (0, Q_DIM // PAIR + p)),
            pl.BlockSpec((N_META, PAIR), lambda b, p, s: (0, (Q_DIM + KV_DIM) // PAIR + p)),
        ],
        out_specs=[
            pl.BlockSpec((SEQ, QPAIR), lambda b, p, s: (b, p)),
            pl.BlockSpec((1, WINDOW, PAIR), lambda b, p, s: (b, 0, p)),
            pl.BlockSpec((1, WINDOW, PAIR), lambda b, p, s: (b, 0, p)),
        ],
        scratch_shapes=[
            pltpu.VMEM((2, N_QBLK + 1, PAIR, QBLK), _bf16),
            pltpu.VMEM((2, KBUF_ROWS, PAIR), _bf16),
            pltpu.VMEM((GROUP, 2 * QBLK, 2 * QBLK), _f32),
        ],
    )
    return pl.pallas_call(
        _attn_prompt_kernel,
        grid_spec=grid_spec,
        out_shape=[
            jax.ShapeDtypeStruct((N_MAIN, Q_DIM), _bf16),
            jax.ShapeDtypeStruct((BATCH, WINDOW, KV_DIM), _f32),
            jax.ShapeDtypeStruct((BATCH, WINDOW, KV_DIM), _f32),
        ],
        compiler_params=_params(("arbitrary", "arbitrary")),
        name="attn_prompt",
    )(sinks, qm, km, vm, qkv_s, qkv_s)


SEQ_CHUNK = 8
N_QROWS = N_HEADS * DEC_SEQ
N_KEYS = WINDOW + DEC_SEQ
N_KEYS_PAD = ((N_KEYS + 7) // 8) * 8


def _attn_sample_kernel(q_ref, kn_ref, vn_ref, ck_ref, cv_ref, sink_ref,
                        o_ref, nk_ref, nv_ref, kk_buf, vv_buf):
    q4 = q_ref[...]
    shp = (SEQ_CHUNK, N_QROWS, KV_DIM)
    row_kv = lax.broadcasted_iota(jnp.int32, shp, 1) // (GROUP * DEC_SEQ)
    lane_kv = lax.broadcasted_iota(jnp.int32, shp, 2) // HEAD_DIM
    own = row_kv == lane_kv
    qp = jnp.where(own, q4, 0.0)

    kk_buf[:, 0:WINDOW, :] = ck_ref[...]
    kk_buf[:, WINDOW:N_KEYS, :] = kn_ref[...]
    kk_buf[:, N_KEYS:N_KEYS_PAD, :] = jnp.zeros((SEQ_CHUNK, N_KEYS_PAD - N_KEYS, KV_DIM), _f32)
    vv_buf[:, 0:WINDOW, :] = cv_ref[...]
    vv_buf[:, WINDOW:N_KEYS, :] = vn_ref[...]
    vv_buf[:, N_KEYS:N_KEYS_PAD, :] = jnp.zeros((SEQ_CHUNK, N_KEYS_PAD - N_KEYS, KV_DIM), _f32)
    nk_ref[...] = kk_buf[:, DEC_SEQ:N_KEYS, :]
    nv_ref[...] = vv_buf[:, DEC_SEQ:N_KEYS, :]

    kk = kk_buf[...]
    vv = vv_buf[...]
    s = jnp.einsum('bqd,bkd->bqk', qp, kk, preferred_element_type=_f32)

    row = lax.broadcasted_iota(jnp.int32, (N_QROWS, N_KEYS_PAD), 0)
    col = lax.broadcasted_iota(jnp.int32, (N_QROWS, N_KEYS_PAD), 1)
    step = row % DEC_SEQ
    head = (row // DEC_SEQ + 1).astype(_f32)
    dist = step + WINDOW - col
    valid = (dist >= 0) & (dist <= WINDOW) & (col < N_KEYS)
    slope = jnp.exp2(head * (-8.0 / N_HEADS))
    bias = jnp.where(valid, -slope * dist.astype(_f32) * LOG2E, NEG_INF)
    s = jnp.where(valid[None], s + bias[None], NEG_INF)
    sink = sink_ref[...][None] * LOG2E
    m = jnp.maximum(jnp.max(s, axis=-1, keepdims=True), sink)
    e = jnp.exp2(s - m)
    denom = jnp.sum(e, axis=-1, keepdims=True) + jnp.exp2(sink - m)
    o = jnp.einsum('bqk,bkd->bqd', e, vv, preferred_element_type=_f32) / denom
    o = jnp.where(own, o, 0.0)
    acc = o
    for g in range(1, N_KV_HEADS):
        acc = acc + pltpu.roll(o, g * HEAD_DIM, axis=2)
    o_ref[...] = acc[:, :, 0:HEAD_DIM]


def _attn_sample(q_s, k_new, v_new, cache_k, cache_v, sink_rows):
    c = SEQ_CHUNK
    return pl.pallas_call(
        _attn_sample_kernel,
        grid=(DEC_BATCH // c,),
        in_specs=[
            pl.BlockSpec((c, N_QROWS, KV_DIM), lambda i: (i, 0, 0)),
            pl.BlockSpec((c, DEC_SEQ, KV_DIM), lambda i: (i, 0, 0)),
            pl.BlockSpec((c, DEC_SEQ, KV_DIM), lambda i: (i, 0, 0)),
            pl.BlockSpec((c, WINDOW, KV_DIM), lambda i: (i, 0, 0)),
            pl.BlockSpec((c, WINDOW, KV_DIM), lambda i: (i, 0, 0)),
            pl.BlockSpec((N_QROWS, 1), lambda i: (0, 0)),
        ],
        out_specs=[
            pl.BlockSpec((c, N_QROWS, HEAD_DIM), lambda i: (i, 0, 0)),
            pl.BlockSpec((c, WINDOW, KV_DIM), lambda i: (i, 0, 0)),
            pl.BlockSpec((c, WINDOW, KV_DIM), lambda i: (i, 0, 0)),
        ],
        out_shape=[
            jax.ShapeDtypeStruct((DEC_BATCH, N_QROWS, HEAD_DIM), _f32),
            jax.ShapeDtypeStruct((DEC_BATCH, WINDOW, KV_DIM), _f32),
            jax.ShapeDtypeStruct((DEC_BATCH, WINDOW, KV_DIM), _f32),
        ],
        scratch_shapes=[
            pltpu.VMEM((c, N_KEYS_PAD, KV_DIM), _f32),
            pltpu.VMEM((c, N_KEYS_PAD, KV_DIM), _f32),
        ],
        compiler_params=_params(("arbitrary",)),
        name="attn_sample",
    )(q_s, k_new, v_new, cache_k, cache_v, sink_rows)


def _gate_kernel(*refs):
    h_ref, cb_ref, atm_ref, ats_ref = refs[:4]
    w_refs = refs[4:4 + 4 * GATE_PIECES]
    tm_ref, ts_ref, at_all = refs[4 + 4 * GATE_PIECES:]
    at_all[0:TM, :] = atm_ref[...]
    at_all[TM:, :] = ats_ref[...]
    h = h_ref[...]
    cb = cb_ref[...]
    at = at_all[...]
    for c in range(GATE_PIECES):
        wgc_ref, wga_ref, wc_ref, wa_ref = w_refs[4 * c:4 * c + 4]
        cols = slice(c * GATE_PIECE, (c + 1) * GATE_PIECE)
        gc = jnp.dot(h, wgc_ref[...], preferred_element_type=_f32)
        ga = jnp.dot(h, wga_ref[...], preferred_element_type=_f32)
        yc = jnp.dot(cb, wc_ref[...], preferred_element_type=_f32)
        ya = jnp.dot(at, wa_ref[...], preferred_element_type=_f32)
        t = (jax.nn.sigmoid(gc) * yc + jax.nn.sigmoid(ga) * ya).astype(_bf16)
        tm_ref[:, cols] = t[0:TM, :]
        ts_ref[:, cols] = t[TM:, :]


def _gate(h_all, cb_all, atm, ats, w_in_b, wc, wa):
    row = lambda n, i: (i, 0)
    once = pl.Buffered(1)
    w_specs, w_args = [], []
    for c in range(GATE_PIECES):
        piece = lambda n, i, c=c, base=0: (0, base + n * GATE_PIECES + c)
        w_specs += [
            pl.BlockSpec((D_MODEL, GATE_PIECE),
                         functools.partial(piece, base=OFF_GC // GATE_PIECE), pipeline_mode=once),
            pl.BlockSpec((D_MODEL, GATE_PIECE),
                         functools.partial(piece, base=OFF_GA // GATE_PIECE), pipeline_mode=once),
            pl.BlockSpec((D_CONV, GATE_PIECE), piece, pipeline_mode=once),
            pl.BlockSpec((Q_DIM, GATE_PIECE), piece, pipeline_mode=once),
        ]
        w_args += [w_in_b, w_in_b, wc, wa]
    return pl.pallas_call(
        _gate_kernel,
        grid=(N_OUT_CHUNKS, N_TILES),
        in_specs=[
            pl.BlockSpec((TILE, D_MODEL), row),
            pl.BlockSpec((TILE, D_CONV), row),
            pl.BlockSpec((TM, Q_DIM), row),
            pl.BlockSpec((TS, Q_DIM), row),
        ] + w_specs,
        out_specs=[
            pl.BlockSpec((TM, TN_OUT), lambda n, i: (i, n)),
            pl.BlockSpec((TS, TN_OUT), lambda n, i: (i, n)),
        ],
        out_shape=[
            jax.ShapeDtypeStruct((N_MAIN, D_MODEL), _bf16),
            jax.ShapeDtypeStruct((N_SMALL, D_MODEL), _bf16),
        ],
        scratch_shapes=[pltpu.VMEM((TILE, Q_DIM), _bf16)],
        compiler_params=_params(("arbitrary", "arbitrary")),
        name="gate",
    )(h_all, cb_all, atm, ats, *w_args)


def _sample_rows(a):
    return a.transpose(1, 0, 2).reshape(N_SAMPLE, a.shape[-1])


def _to_small(sample_rows, dtype):
    w = sample_rows.shape[-1]
    return jnp.concatenate([
        jnp.zeros((SAMPLE_ROW0, w), dtype), sample_rows.astype(dtype),
        jnp.zeros((N_SMALL - SAMPLE_ROW0 - N_SAMPLE, w), dtype)], axis=0)


def kernel(x_prompt, x_sample, state_conv, cache_k_win, cache_v_win, meta_tokens, ffn1_norm, ffn1_w_up, ffn1_w_down, mix_norm, w_in, q_norm, k_norm, conv_w, w_conv_out, attn_sinks, w_attn_out, w_o, ffn2_norm, ffn2_w_up, ffn2_w_down):
    l = 0
    xm = x_prompt.reshape(N_MAIN, D_MODEL)
    xs = jnp.concatenate([
        meta_tokens.astype(_f32), _sample_rows(x_sample),
        jnp.zeros((N_SMALL - N_META - N_SAMPLE, D_MODEL), _f32)], axis=0)

    wgu1 = _cast(_w_up_job(256), ffn1_w_up[l])
    wd1 = _cast(_w_down_job(TF), ffn1_w_down[l])
    jobs = (_w_up_job(16), _w_down_job(32), _cast_job(D_MODEL, D_IN, 16),
            _cast_job(D_CONV, D_MODEL, 16), _cast_job(Q_DIM, D_MODEL, 16),
            _cast_job(D_MODEL, D_MODEL, 16))
    x1m, x1s, h_all, wgu2, wd2, w_in_b, wc_b, wa_b, wo_b = _ffn(
        xm, xs, ffn1_norm[l].reshape(1, D_MODEL), wgu1, wd1,
        next_gain=mix_norm[l].reshape(1, D_MODEL), jobs=jobs,
        job_inputs=(ffn2_w_up[l], ffn2_w_down[l], w_in[l], w_conv_out[l], w_attn_out[l], w_o[l]))

    state_rows = state_conv[l].transpose(1, 0, 2).reshape((CONV_WIDTH - 1) * DEC_BATCH, D_CONV)
    cb_all, qm, km, vm, qkv_s, new_conv_p, new_conv_s = _inproj(
        h_all, w_in_b, conv_w[l], state_rows, q_norm[l], k_norm[l])

    atm, new_k_p, new_v_p = _attn_prompt(qm, km, vm, qkv_s, attn_sinks[l])
    samp = qkv_s[SAMPLE_ROW0:SAMPLE_ROW0 + N_SAMPLE]
    q_s = (samp[:, 0:Q_DIM].reshape(DEC_SEQ, DEC_BATCH, N_HEADS, HEAD_DIM)
           .transpose(1, 2, 0, 3).reshape(DEC_BATCH, N_QROWS, HEAD_DIM))
    q_s = jnp.tile(q_s, (1, 1, N_KV_HEADS))
    k_new = samp[:, Q_DIM:Q_DIM + KV_DIM].reshape(DEC_SEQ, DEC_BATCH, KV_DIM).transpose(1, 0, 2)
    v_new = samp[:, Q_DIM + KV_DIM:N_QKV].reshape(DEC_SEQ, DEC_BATCH, KV_DIM).transpose(1, 0, 2)
    sink_rows = jnp.repeat(attn_sinks[l].astype(_f32), DEC_SEQ).reshape(N_QROWS, 1)
    o_s, new_k_s, new_v_s = _attn_sample(
        q_s, k_new, v_new,
        cache_k_win[l].reshape(DEC_BATCH, WINDOW, KV_DIM),
        cache_v_win[l].reshape(DEC_BATCH, WINDOW, KV_DIM), sink_rows)
    at_rows = (o_s.reshape(DEC_BATCH, N_HEADS, DEC_SEQ, HEAD_DIM)
               .transpose(2, 0, 1, 3).reshape(N_SAMPLE, Q_DIM))
    ats = _to_small(at_rows, _bf16)

    t_main, t_small = _gate(h_all, cb_all, atm, ats, w_in_b, wc_b, wa_b)
    ym, ys = _ffn_stream(x1m, x1s, ffn2_norm[l].reshape(1, D_MODEL), wgu2, wd2,
                         proj=(t_main, t_small, wo_b))

    y_prompt = ym.reshape(BATCH, SEQ, D_MODEL)
    y_sample = (ys[SAMPLE_ROW0:SAMPLE_ROW0 + N_SAMPLE]
                .reshape(DEC_SEQ, DEC_BATCH, D_MODEL).transpose(1, 0, 2))
    kv_shape_p = (1, BATCH, WINDOW, N_KV_HEADS, HEAD_DIM)
    kv_shape_s = (1, DEC_BATCH, WINDOW, N_KV_HEADS, HEAD_DIM)
    return (y_prompt, y_sample,
            new_conv_p[None],
            new_k_p.reshape(kv_shape_p), new_v_p.reshape(kv_shape_p),
            new_conv_s.reshape(CONV_WIDTH - 1, DEC_BATCH, D_CONV).transpose(1, 0, 2)[None],
            new_k_s.reshape(kv_shape_s), new_v_s.reshape(kv_shape_s))
```

```python
import functools
from typing import NamedTuple

import jax
import jax.numpy as jnp
from jax import lax
from jax.experimental import pallas as pl
from jax.experimental.pallas import tpu as pltpu

D_MODEL = 2048
BATCH = 4
SEQ = 2048
DEC_BATCH = 32
DEC_SEQ = 4
PAST_LEN = 16384
N_META = 16
D_CONV = D_MODEL // 2
CONV_WIDTH = 3
HEAD_DIM = 64
N_HEADS = (D_MODEL // 2) // HEAD_DIM
N_KV_HEADS = N_HEADS // 4
GROUP = N_HEADS // N_KV_HEADS
Q_DIM = N_HEADS * HEAD_DIM
KV_DIM = N_KV_HEADS * HEAD_DIM
PAIR = 2 * HEAD_DIM
WINDOW = 128
D_FF = ((8 * D_MODEL // 3 + 127) // 128) * 128
D_IN = 3 * D_CONV + Q_DIM + 2 * KV_DIM + 2 * D_MODEL
EPS = 1e-6
NEG_INF = -1e30
LOG2E = 1.4426950408889634
Q_SCALE = HEAD_DIM ** -0.5 * LOG2E

OFF_XC = 0
OFF_BG = D_CONV
OFF_CG = 2 * D_CONV
OFF_Q = 3 * D_CONV
OFF_K = OFF_Q + Q_DIM
OFF_V = OFF_K + KV_DIM
OFF_GC = OFF_V + KV_DIM
OFF_GA = OFF_GC + D_MODEL
N_QKV = Q_DIM + 2 * KV_DIM

N_MAIN = BATCH * SEQ
N_SAMPLE = DEC_BATCH * DEC_SEQ
N_SMALL = 256
SAMPLE_ROW0 = N_META

N_TILES = 16
TM = N_MAIN // N_TILES
TS = N_SMALL // N_TILES
TILE = TM + TS
N_ALL = N_TILES * TILE
TILES_PER_SEQ = SEQ // TM

TF = 512
D_FF_PAD = ((D_FF + TF - 1) // TF) * TF
N_FF_CHUNKS = D_FF_PAD // TF
FFN_STEPS = N_TILES * N_FF_CHUNKS
TN_OUT = 1024
N_OUT_CHUNKS = D_MODEL // TN_OUT
GATE_PIECE = 512
GATE_PIECES = TN_OUT // GATE_PIECE

QBLK = 128
KPAD = QBLK - N_META
N_QBLK = SEQ // QBLK
KBUF_ROWS = QBLK + SEQ

VMEM_LIMIT = 56 * 1024 * 1024

_bf16 = jnp.bfloat16
_f32 = jnp.float32


def _params(sem):
    return pltpu.CompilerParams(dimension_semantics=sem, vmem_limit_bytes=VMEM_LIMIT)


def _rms_rows(x, g):
    ms = jnp.mean(x * x, axis=-1, keepdims=True)
    return x * lax.rsqrt(ms + EPS) * g


def _pair_rms(x, g_pair):
    low = lax.broadcasted_iota(jnp.int32, x.shape, 1) < HEAD_DIM
    x2 = x * x
    s_lo = jnp.sum(jnp.where(low, x2, 0.0), axis=-1, keepdims=True)
    s_hi = jnp.sum(jnp.where(low, 0.0, x2), axis=-1, keepdims=True)
    ms = jnp.where(low, s_lo, s_hi) * (1.0 / HEAD_DIM)
    return x * lax.rsqrt(ms + EPS) * g_pair


class _CastJob(NamedTuple):
    slab: int
    rows_in: int
    n_in_slabs: int
    n_out_slabs: int
    n_in_cols: int
    n_out_cols: int
    segments: tuple
    zero_ranges: tuple


def _cast_job(rows_in, cols_in, slab, segments=None, zero_ranges=(), rows_out=None, cols_out=None):
    rows_out = rows_in if rows_out is None else rows_out
    cols_out = cols_in if cols_out is None else cols_out
    segments = ((0, cols_in, 0),) if segments is None else segments
    assert rows_out % slab == 0
    return _CastJob(slab, rows_in, -(-rows_in // slab), rows_out // slab, cols_in, cols_out,
                    tuple(segments), tuple(zero_ranges))


def _job_specs(job, step_of):
    in_idx = lambda *ids: (jnp.minimum(step_of(*ids), job.n_in_slabs - 1), 0)
    out_idx = lambda *ids: (jnp.minimum(step_of(*ids), job.n_out_slabs - 1), 0)
    return (pl.BlockSpec((job.slab, job.n_in_cols), in_idx),
            pl.BlockSpec((job.slab, job.n_out_cols), out_idx))


def _run_cast_job(job, step, x_ref, o_ref):
    out_slab = jnp.minimum(step, job.n_out_slabs - 1)
    padded = job.n_out_slabs * job.slab > job.rows_in
    for src, width, dst in job.segments:
        x = x_ref[:, src:src + width]
        if padded:
            row = out_slab * job.slab + lax.broadcasted_iota(jnp.int32, x.shape, 0)
            x = jnp.where(row < job.rows_in, x, 0.0)
        o_ref[:, dst:dst + width] = x.astype(_bf16)
    for lo, hi in job.zero_ranges:
        o_ref[:, lo:hi] = jnp.zeros((job.slab, hi - lo), _bf16)


def _w_up_job(slab):
    segments, zero_ranges = [], []
    for c in range(N_FF_CHUNKS):
        width = min(TF, D_FF - c * TF)
        for half, src0 in enumerate((0, D_FF)):
            dst = (2 * c + half) * TF
            segments.append((src0 + c * TF, width, dst))
            if width < TF:
                zero_ranges.append((dst + width, dst + TF))
    return _cast_job(D_MODEL, 2 * D_FF, slab, segments=segments, zero_ranges=zero_ranges,
                     cols_out=2 * D_FF_PAD)


def _w_down_job(slab):
    return _cast_job(D_FF, D_MODEL, slab, rows_out=D_FF_PAD)


FC = 256
N_FC = D_FF_PAD // FC
assert TF % FC == 0


def _swiglu_chunk(h, w_gate_up, w_down, n):
    gu = jnp.dot(h, w_gate_up, preferred_element_type=_f32)
    gate, up = gu[:, 0:n], gu[:, n:2 * n]
    a = (gate * jax.nn.sigmoid(gate) * up * 0.5).astype(_bf16)
    return jnp.dot(a, w_down, preferred_element_type=_f32)


def _ffn_first_kernel(xm_ref, xs_ref, g_ref, gn_ref, wup_hbm, wdn_hbm,
                      om_ref, os_ref, hn_ref, wgu_out, wd_out,
                      h_ref, stage_g, stage_u, stage_d, cast_gu, cast_d, sem_in, sem_out):
    def width(c):
        return min(FC, D_FF - c * FC)

    def in_copies(c):
        s, w = c % 2, width(c)
        return (
            pltpu.make_async_copy(wup_hbm.at[:, pl.ds(c * FC, w)],
                                  stage_g.at[s, :, pl.ds(0, w)], sem_in.at[0, s]),
            pltpu.make_async_copy(wup_hbm.at[:, pl.ds(D_FF + c * FC, w)],
                                  stage_u.at[s, :, pl.ds(0, w)], sem_in.at[1, s]),
            pltpu.make_async_copy(wdn_hbm.at[pl.ds(c * FC, w), :],
                                  stage_d.at[s, pl.ds(0, w), :], sem_in.at[2, s]),
        )

    def out_copies(c):
        s = c % 2
        col = (c * FC // TF) * 2 * TF + (c * FC) % TF
        return (
            pltpu.make_async_copy(cast_gu.at[s, :, pl.ds(0, FC)],
                                  wgu_out.at[:, pl.ds(col, FC)], sem_out.at[0, s]),
            pltpu.make_async_copy(cast_gu.at[s, :, pl.ds(FC, FC)],
                                  wgu_out.at[:, pl.ds(col + TF, FC)], sem_out.at[1, s]),
            pltpu.make_async_copy(cast_d.at[s], wd_out.at[pl.ds(c * FC, FC), :], sem_out.at[2, s]),
        )

    def start(copies):
        for cp in copies:
            cp.start()

    def wait(copies):
        for cp in copies:
            cp.wait()

    start(in_copies(0))
    g = g_ref[...]
    xm = xm_ref[...]
    xs = xs_ref[...]
    h_ref[0:TM, :] = _rms_rows(xm, g).astype(_bf16)
    om_ref[...] = xm
    h_ref[TM:, :] = _rms_rows(xs, g).astype(_bf16)
    os_ref[...] = xs
    h = h_ref[...]

    for c in range(N_FC):
        s, w = c % 2, width(c)
        if c + 1 < N_FC:
            start(in_copies(c + 1))
        wait(in_copies(c))
        if c >= 2:
            wait(out_copies(c - 2))
        if w == FC:
            cast_gu[s, :, 0:FC] = stage_g[s].astype(_bf16)
            cast_gu[s, :, FC:2 * FC] = stage_u[s].astype(_bf16)
            cast_d[s] = stage_d[s].astype(_bf16)
        else:
            lane = lax.broadcasted_iota(jnp.int32, (D_MODEL, FC), 1)
            sub = lax.broadcasted_iota(jnp.int32, (FC, D_MODEL), 0)
            cast_gu[s, :, 0:FC] = jnp.where(lane < w, stage_g[s], 0.0).astype(_bf16)
            cast_gu[s, :, FC:2 * FC] = jnp.where(lane < w, stage_u[s], 0.0).astype(_bf16)
            cast_d[s] = jnp.where(sub < w, stage_d[s], 0.0).astype(_bf16)
        start(out_copies(c))
        r = _swiglu_chunk(h, cast_gu[s], cast_d[s], FC)
        om_ref[...] += r[0:TM, :]
        os_ref[...] += r[TM:, :]
    wait(out_copies(N_FC - 2))
    wait(out_copies(N_FC - 1))

    gn = gn_ref[...]
    hn_ref[0:TM, :] = _rms_rows(om_ref[...], gn).astype(_bf16)
    hn_ref[TM:, :] = _rms_rows(os_ref[...], gn).astype(_bf16)


def _ffn_first(xm, xs, g, next_gain, w_up, w_down):
    first = lambda i: (0, 0)
    return pl.pallas_call(
        _ffn_first_kernel,
        grid=(1,),
        in_specs=[
            pl.BlockSpec((TM, D_MODEL), first),
            pl.BlockSpec((TS, D_MODEL), first),
            pl.BlockSpec((1, D_MODEL), first),
            pl.BlockSpec((1, D_MODEL), first),
            pl.BlockSpec(memory_space=pl.ANY),
            pl.BlockSpec(memory_space=pl.ANY),
        ],
        out_specs=[
            pl.BlockSpec((TM, D_MODEL), first),
            pl.BlockSpec((TS, D_MODEL), first),
            pl.BlockSpec((TILE, D_MODEL), first),
            pl.BlockSpec(memory_space=pl.ANY),
            pl.BlockSpec(memory_space=pl.ANY),
        ],
        out_shape=[
            jax.ShapeDtypeStruct((TM, D_MODEL), _f32),
            jax.ShapeDtypeStruct((TS, D_MODEL), _f32),
            jax.ShapeDtypeStruct((TILE, D_MODEL), _bf16),
            jax.ShapeDtypeStruct((D_MODEL, 2 * D_FF_PAD), _bf16),
            jax.ShapeDtypeStruct((D_FF_PAD, D_MODEL), _bf16),
        ],
        scratch_shapes=[
            pltpu.VMEM((TILE, D_MODEL), _bf16),
            pltpu.VMEM((2, D_MODEL, FC), _f32),
            pltpu.VMEM((2, D_MODEL, FC), _f32),
            pltpu.VMEM((2, FC, D_MODEL), _f32),
            pltpu.VMEM((2, D_MODEL, 2 * FC), _bf16),
            pltpu.VMEM((2, FC, D_MODEL), _bf16),
            pltpu.SemaphoreType.DMA((3, 2)),
            pltpu.SemaphoreType.DMA((3, 2)),
        ],
        compiler_params=_params(("arbitrary",)),
        name="ffn_first",
    )(xm, xs, g, next_gain, w_up, w_down)


def _ffn_main_kernel(jobs, *refs):
    refs = list(refs)
    take = lambda n: [refs.pop(0) for _ in range(n)]
    xm_ref, xs_ref, g_ref, wgu_ref, wd_ref, gn_ref, om0_hbm, os0_hbm, hn0_hbm = take(9)
    job_in = take(len(jobs))
    om_ref, os_ref, hn_ref = take(3)
    job_out = take(len(jobs))
    h_ref, sem = refs
    i = pl.program_id(0)
    j = pl.program_id(1)

    def run_jobs():
        for job, x_ref, o_ref in zip(jobs, job_in, job_out):
            _run_cast_job(job, i * N_FF_CHUNKS + j, x_ref, o_ref)

    @pl.when(i == 0)
    def _():
        @pl.when(j == 0)
        def _():
            copies = (pltpu.make_async_copy(om0_hbm, om_ref, sem.at[0]),
                      pltpu.make_async_copy(os0_hbm, os_ref, sem.at[1]),
                      pltpu.make_async_copy(hn0_hbm, hn_ref, sem.at[2]))
            for cp in copies:
                cp.start()
            for cp in copies:
                cp.wait()
        run_jobs()

    @pl.when(i > 0)
    def _():
        @pl.when(j == 0)
        def _():
            g = g_ref[...]
            xm = xm_ref[...]
            xs = xs_ref[...]
            h_ref[0:TM, :] = _rms_rows(xm, g).astype(_bf16)
            om_ref[...] = xm
            h_ref[TM:, :] = _rms_rows(xs, g).astype(_bf16)
            os_ref[...] = xs

        run_jobs()
        r = _swiglu_chunk(h_ref[...], wgu_ref[...], wd_ref[...], TF)
        om_ref[...] += r[0:TM, :]
        os_ref[...] += r[TM:, :]

        @pl.when(j == N_FF_CHUNKS - 1)
        def _():
            gn = gn_ref[...]
            hn_ref[0:TM, :] = _rms_rows(om_ref[...], gn).astype(_bf16)
            hn_ref[TM:, :] = _rms_rows(os_ref[...], gn).astype(_bf16)


def _ffn_main(xm, xs, g, wgu, wd, next_gain, first_tile, jobs, job_inputs):
    assert all(job.n_out_slabs <= FFN_STEPS for job in jobs)
    job_specs = [_job_specs(job, lambda i, j: i * N_FF_CHUNKS + j) for job in jobs]
    row = lambda i, j: (i, 0)
    fixed = lambda i, j: (0, 0)
    return pl.pallas_call(
        functools.partial(_ffn_main_kernel, tuple(jobs)),
        grid=(N_TILES, N_FF_CHUNKS),
        in_specs=[
            pl.BlockSpec((TM, D_MODEL), row),
            pl.BlockSpec((TS, D_MODEL), row),
            pl.BlockSpec((1, D_MODEL), fixed),
            pl.BlockSpec((D_MODEL, 2 * TF), lambda i, j: (0, j)),
            pl.BlockSpec((TF, D_MODEL), lambda i, j: (j, 0)),
            pl.BlockSpec((1, D_MODEL), fixed),
        ] + [pl.BlockSpec(memory_space=pl.ANY) for _ in first_tile] + [s[0] for s in job_specs],
        out_specs=[
            pl.BlockSpec((TM, D_MODEL), row),
            pl.BlockSpec((TS, D_MODEL), row),
            pl.BlockSpec((TILE, D_MODEL), row),
        ] + [s[1] for s in job_specs],
        out_shape=[
            jax.ShapeDtypeStruct((N_MAIN, D_MODEL), _f32),
            jax.ShapeDtypeStruct((N_SMALL, D_MODEL), _f32),
            jax.ShapeDtypeStruct((N_ALL, D_MODEL), _bf16),
        ] + [jax.ShapeDtypeStruct((job.n_out_slabs * job.slab, job.n_out_cols), _bf16)
             for job in jobs],
        scratch_shapes=[pltpu.VMEM((TILE, D_MODEL), _bf16), pltpu.SemaphoreType.DMA((3,))],
        compiler_params=_params(("arbitrary", "arbitrary")),
        name="ffn",
    )(xm, xs, g, wgu, wd, next_gain, *first_tile, *job_inputs)


def _ffn_stream_kernel(xm_ref, xs_ref, tm_ref, ts_ref, wo_ref, g_ref, wgu_hbm, wd_hbm,
                       om_ref, os_ref, h_ref, t_all, wgu_buf, wd_buf, sem):
    i = pl.program_id(0)

    def chunk_copies(chunk, slot):
        return (
            pltpu.make_async_copy(wgu_hbm.at[:, pl.ds(chunk * 2 * TF, 2 * TF)],
                                  wgu_buf.at[slot], sem.at[0, slot]),
            pltpu.make_async_copy(wd_hbm.at[pl.ds(chunk * TF, TF), :],
                                  wd_buf.at[slot], sem.at[1, slot]),
        )

    def start(chunk, slot):
        for c in chunk_copies(chunk, slot):
            c.start()

    @pl.when(i == 0)
    def _():
        start(0, 0)

    g = g_ref[...]
    t_all[0:TM, :] = tm_ref[...]
    t_all[TM:, :] = ts_ref[...]
    p = jnp.dot(t_all[...], wo_ref[...], preferred_element_type=_f32)
    xm = xm_ref[...] + p[0:TM, :]
    xs = xs_ref[...] + p[TM:, :]
    h_ref[0:TM, :] = _rms_rows(xm, g).astype(_bf16)
    om_ref[...] = xm
    h_ref[TM:, :] = _rms_rows(xs, g).astype(_bf16)
    os_ref[...] = xs

    h = h_ref[...]
    for j in range(N_FF_CHUNKS):
        slot = (i * N_FF_CHUNKS + j) % 2
        if j + 1 < N_FF_CHUNKS:
            start(j + 1, 1 - slot)
        else:
            @pl.when(i + 1 < N_TILES)
            def _():
                start(0, 1 - slot)
        for c in chunk_copies(j, slot):
            c.wait()
        r = _swiglu_chunk(h, wgu_buf[slot], wd_buf[slot], TF)
        om_ref[...] += r[0:TM, :]
        os_ref[...] += r[TM:, :]


def _ffn_stream(xm, xs, g, wgu, wd, proj):
    row = lambda i: (i, 0)
    fixed = lambda i: (0, 0)
    return pl.pallas_call(
        _ffn_stream_kernel,
        grid=(N_TILES,),
        in_specs=[
            pl.BlockSpec((TM, D_MODEL), row),
            pl.BlockSpec((TS, D_MODEL), row),
            pl.BlockSpec((TM, D_MODEL), row),
            pl.BlockSpec((TS, D_MODEL), row),
            pl.BlockSpec((D_MODEL, D_MODEL), fixed, pipeline_mode=pl.Buffered(1)),
            pl.BlockSpec((1, D_MODEL), fixed),
            pl.BlockSpec(memory_space=pl.ANY),
            pl.BlockSpec(memory_space=pl.ANY),
        ],
        out_specs=[pl.BlockSpec((TM, D_MODEL), row), pl.BlockSpec((TS, D_MODEL), row)],
        out_shape=[jax.ShapeDtypeStruct((N_MAIN, D_MODEL), _f32),
                   jax.ShapeDtypeStruct((N_SMALL, D_MODEL), _f32)],
        scratch_shapes=[
            pltpu.VMEM((TILE, D_MODEL), _bf16),
            pltpu.VMEM((TILE, D_MODEL), _bf16),
            pltpu.VMEM((2, D_MODEL, 2 * TF), _bf16),
            pltpu.VMEM((2, TF, D_MODEL), _bf16),
            pltpu.SemaphoreType.DMA((2, 2)),
        ],
        compiler_params=_params(("arbitrary",)),
        name="ffn_stream",
    )(xm, xs, *proj, g, wgu, wd)


N_BRANCH_IN = OFF_GC
CONV_HALO = 8
U_BASE = 112
U_STATE0 = U_BASE + SAMPLE_ROW0 - (CONV_WIDTH - 1) * DEC_BATCH
U_ROWS = U_BASE + N_SMALL
U_LAST0 = U_BASE + SAMPLE_ROW0 + (DEC_SEQ - (CONV_WIDTH - 1)) * DEC_BATCH
assert U_BASE >= (CONV_WIDTH - 1) * DEC_BATCH and U_BASE % TS == 0 and TS == N_META


def _inproj_kernel(h_ref, w_ref, cw_ref, st_ref, qg_ref, kg_ref,
                   cb_ref, q_ref, k_ref, v_ref, sm_ref, ncp_ref, ncs_ref,
                   ubuf, meta_halo, utab):
    i = pl.program_id(0)

    @pl.when(i == 0)
    def _():
        utab[...] = jnp.zeros(utab.shape, _f32)
        utab[U_STATE0:U_STATE0 + (CONV_WIDTH - 1) * DEC_BATCH, :] = st_ref[...]
        meta_halo[...] = jnp.zeros(meta_halo.shape, _f32)
        ubuf[0:CONV_HALO, :] = jnp.zeros((CONV_HALO, D_CONV), _f32)

    h = h_ref[...]
    col_dot = lambda off, width: jnp.dot(h, w_ref[:, off:off + width],
                                         preferred_element_type=_f32)
    z_q = col_dot(OFF_Q, Q_DIM)
    z_kv = col_dot(OFF_K, 2 * KV_DIM)
    z_xc = col_dot(OFF_XC, D_CONV)
    z_cg = col_dot(OFF_CG, D_CONV)
    z_bg = col_dot(OFF_BG, D_CONV)
    cw = cw_ref[...]
    w0, w1, w2 = cw[0:1, :], cw[1:2, :], cw[2:3, :]
    qg = qg_ref[...]
    kg = kg_ref[...]

    def qkv(rows, q_out, k_out, v_out):
        for t in range(Q_DIM // PAIR):
            q_out(t, _pair_rms(z_q[rows, t * PAIR:(t + 1) * PAIR], qg) * Q_SCALE)
        for t in range(KV_DIM // PAIR):
            k_out(t, _pair_rms(z_kv[rows, t * PAIR:(t + 1) * PAIR], kg))
            v_out(t, z_kv[rows, KV_DIM + t * PAIR:KV_DIM + (t + 1) * PAIR])

    u_s = z_cg[TM:, :] * z_xc[TM:, :]
    cur = pl.multiple_of(U_BASE + TS * i, TS)
    slot = pl.multiple_of(jnp.where(i == 0, U_ROWS, cur), TS)
    utab[pl.ds(slot, TS), :] = u_s
    y_s = (w0 * utab[pl.ds(cur - 2 * DEC_BATCH, TS), :]
           + w1 * utab[pl.ds(cur - DEC_BATCH, TS), :] + w2 * u_s)
    cb_ref[TM:, :] = (z_bg[TM:, :] * y_s).astype(_bf16)
    meta_halo[...] = jnp.where(i == 0, u_s[N_META - CONV_HALO:N_META, :], meta_halo[...])

    def sm_store(off):
        def store(t, val):
            sm_ref[:, off + t * PAIR:off + (t + 1) * PAIR] = val
        return store
    qkv(slice(TM, TILE), sm_store(0), sm_store(Q_DIM), sm_store(Q_DIM + KV_DIM))

    ncs_ref[...] = utab[U_LAST0:U_LAST0 + (CONV_WIDTH - 1) * DEC_BATCH, :]

    ubuf[0:CONV_HALO, :] = jnp.where(i % TILES_PER_SEQ == 0, meta_halo[...], ubuf[0:CONV_HALO, :])

    u = z_cg[0:TM, :] * z_xc[0:TM, :]
    ubuf[CONV_HALO:CONV_HALO + TM, :] = u
    u1 = ubuf[CONV_HALO - 1:CONV_HALO - 1 + TM, :]
    u2 = ubuf[CONV_HALO - 2:CONV_HALO - 2 + TM, :]
    cb_ref[0:TM, :] = (z_bg[0:TM, :] * (w0 * u2 + w1 * u1 + w2 * u)).astype(_bf16)
    tail = ubuf[TM:TM + CONV_HALO, :]
    ubuf[0:CONV_HALO, :] = tail
    ncp_ref[0] = tail[CONV_HALO - (CONV_WIDTH - 1):CONV_HALO, :]

    def lane_store(ref):
        def store(t, val):
            ref[:, t * PAIR:(t + 1) * PAIR] = val
        return store
    qkv(slice(0, TM), lane_store(q_ref), lane_store(k_ref), lane_store(v_ref))


def _inproj(h_all, w_in_b, conv_w, state_rows, q_gain, k_gain):
    qg = jnp.tile(q_gain.reshape(1, HEAD_DIM), (1, 2))
    kg = jnp.tile(k_gain.reshape(1, HEAD_DIM), (1, 2))
    row = lambda i: (i, 0)
    fixed = lambda i: (0, 0)
    return pl.pallas_call(
        _inproj_kernel,
        grid=(N_TILES,),
        in_specs=[
            pl.BlockSpec((TILE, D_MODEL), row),
            pl.BlockSpec((D_MODEL, N_BRANCH_IN), fixed, pipeline_mode=pl.Buffered(1)),
            pl.BlockSpec((CONV_WIDTH, D_CONV), fixed),
            pl.BlockSpec(((CONV_WIDTH - 1) * DEC_BATCH, D_CONV), fixed),
            pl.BlockSpec((1, PAIR), fixed),
            pl.BlockSpec((1, PAIR), fixed),
        ],
        out_specs=[
            pl.BlockSpec((TILE, D_CONV), row),
            pl.BlockSpec((TM, Q_DIM), row),
            pl.BlockSpec((TM, KV_DIM), row),
            pl.BlockSpec((TM, KV_DIM), row),
            pl.BlockSpec((TS, N_QKV), row),
            pl.BlockSpec((1, CONV_WIDTH - 1, D_CONV), lambda i: (i // TILES_PER_SEQ, 0, 0)),
            pl.BlockSpec(((CONV_WIDTH - 1) * DEC_BATCH, D_CONV), fixed),
        ],
        out_shape=[
            jax.ShapeDtypeStruct((N_ALL, D_CONV), _bf16),
            jax.ShapeDtypeStruct((N_MAIN, Q_DIM), _f32),
            jax.ShapeDtypeStruct((N_MAIN, KV_DIM), _f32),
            jax.ShapeDtypeStruct((N_MAIN, KV_DIM), _f32),
            jax.ShapeDtypeStruct((N_SMALL, N_QKV), _f32),
            jax.ShapeDtypeStruct((BATCH, CONV_WIDTH - 1, D_CONV), _f32),
            jax.ShapeDtypeStruct(((CONV_WIDTH - 1) * DEC_BATCH, D_CONV), _f32),
        ],
        scratch_shapes=[
            pltpu.VMEM((CONV_HALO + TM, D_CONV), _f32),
            pltpu.VMEM((CONV_HALO, D_CONV), _f32),
            pltpu.VMEM((U_ROWS + TS, D_CONV), _f32),
        ],
        compiler_params=_params(("arbitrary",)),
        name="inproj",
    )(h_all, w_in_b, conv_w, state_rows, qg, kg)


QPAIR = 2 * GROUP * HEAD_DIM
HEADS_PER_STEP = 2 * GROUP


def _attn_prompt_kernel(sink_ref, q_ref, k_ref, v_ref, km_ref, vm_ref,
                        o_ref, nk_ref, nv_ref, kt_buf, vbuf, bias_ref):
    gp = pl.program_id(1)

    for order in range(2):
        swap = (lambda x: x) if order == 0 else (lambda x: pltpu.roll(x, HEAD_DIM, axis=1))
        vbuf[order, 0:KPAD, :] = jnp.zeros((KPAD, PAIR), _bf16)
        vbuf[order, KPAD:QBLK, :] = swap(vm_ref[...]).astype(_bf16)
        vbuf[order, QBLK:KBUF_ROWS, :] = swap(v_ref[...]).astype(_bf16)
        first_keys = jnp.concatenate([jnp.zeros((KPAD, PAIR), _f32), km_ref[...]], axis=0)
        kt_buf[order, 0] = swap(first_keys).T.astype(_bf16)
        for b in range(N_QBLK):
            kt_buf[order, b + 1] = swap(k_ref[b * QBLK:(b + 1) * QBLK, :]).T.astype(_bf16)
    nk_ref[0] = k_ref[SEQ - WINDOW:SEQ, :]
    nv_ref[0] = v_ref[SEQ - WINDOW:SEQ, :]

    partner = lambda p: p ^ 1
    row = lax.broadcasted_iota(jnp.int32, (QBLK, 2 * QBLK), 0)
    col = lax.broadcasted_iota(jnp.int32, (QBLK, 2 * QBLK), 1)
    dist = QBLK + row - col
    in_window = (dist >= 0) & (dist <= WINDOW)
    distf = dist.astype(_f32)
    upper = lax.broadcasted_iota(jnp.int32, (2 * QBLK, 1), 0) < QBLK
    sink_cols = []
    for p in range(GROUP):
        sinks = []
        for g, hp in ((0, p), (1, partner(p))):
            head = gp * HEADS_PER_STEP + g * GROUP + hp
            slope = jnp.exp2(jnp.full((QBLK, 2 * QBLK), -8.0 / N_HEADS, _f32)
                             * (head + 1).astype(_f32))
            bias_ref[p, g * QBLK:(g + 1) * QBLK, :] = jnp.where(
                in_window, -slope * distf * LOG2E, NEG_INF)
            sinks.append(sink_ref[head] * LOG2E)
        sink_cols.append(jnp.where(upper, sinks[0], sinks[1]))

    lane = lax.broadcasted_iota(jnp.int32, (QBLK, PAIR), 1)
    low_half = lane < HEAD_DIM
    pad_col = lax.broadcasted_iota(jnp.int32, (2 * QBLK, 2 * QBLK), 1) < KPAD

    def block(i, first):
        r0 = i * QBLK if first else pl.multiple_of(i * QBLK, QBLK)
        qn = q_ref[pl.ds(r0 - QBLK, QBLK), :]
        outs = [[None] * GROUP for _ in range(2)]
        for p in range(GROUP):
            order = p % 2
            kt = jnp.concatenate([kt_buf[order, i - 1], kt_buf[order, i]], axis=1)
            vv = vbuf[order, pl.ds(r0 - QBLK, 2 * QBLK), :]
            tiles = []
            for g, hp in ((0, p), (1, partner(p))):
                hh = g * GROUP + hp
                in_low = hp % 2 == 0
                tile = qn[:, (hh // 2) * PAIR:(hh // 2 + 1) * PAIR]
                tiles.append(jnp.where(low_half if in_low else jnp.logical_not(low_half),
                                       tile, 0.0).astype(_bf16))
            qst = jnp.concatenate(tiles, axis=0)
            s = jnp.dot(qst, kt, preferred_element_type=_f32) + bias_ref[p]
            if first:
                s = jnp.where(pad_col, NEG_INF, s)
            sink = sink_cols[p]
            m = jnp.maximum(jnp.max(s, axis=-1, keepdims=True), sink)
            e = jnp.exp2(s - m)
            denom = jnp.sum(e, axis=-1, keepdims=True) + jnp.exp2(sink - m)
            o = jnp.dot(e.astype(_bf16), vv, preferred_element_type=_f32) / denom
            outs[0][p] = o[0:QBLK, :]
            outs[1][partner(p)] = o[QBLK:2 * QBLK, :]
        for g in range(2):
            for pr in range(GROUP // 2):
                t = g * (GROUP // 2) + pr
                o_ref[pl.ds(r0 - QBLK, QBLK), t * PAIR:(t + 1) * PAIR] = jnp.where(
                    low_half, outs[g][2 * pr], outs[g][2 * pr + 1]).astype(_bf16)

    block(1, True)

    def later_block(i, carry):
        block(i, False)
        return carry

    lax.fori_loop(2, N_QBLK + 1, later_block, 0)


def _attn_prompt(qm, km, vm, qkv_s, sinks):
    grid_spec = pltpu.PrefetchScalarGridSpec(
        num_scalar_prefetch=1,
        grid=(BATCH, N_KV_HEADS // 2),
        in_specs=[
            pl.BlockSpec((SEQ, QPAIR), lambda b, p, s: (b, p)),
            pl.BlockSpec((SEQ, PAIR), lambda b, p, s: (b, p)),
            pl.BlockSpec((SEQ, PAIR), lambda b, p, s: (b, p)),
            pl.BlockSpec((N_META, PAIR), lambda b, p, s: (0, Q_DIM // PAIR + p)),
            pl.BlockSpec((N_META, PAIR), lambda b, p, s: (0, (Q_DIM + KV_DIM) // PAIR + p)),
        ],
        out_specs=[
            pl.BlockSpec((SEQ, QPAIR), lambda b, p, s: (b, p)),
            pl.BlockSpec((1, WINDOW, PAIR), lambda b, p, s: (b, 0, p)),
            pl.BlockSpec((1, WINDOW, PAIR), lambda b, p, s: (b, 0, p)),
        ],
        scratch_shapes=[
            pltpu.VMEM((2, N_QBLK + 1, PAIR, QBLK), _bf16),
            pltpu.VMEM((2, KBUF_ROWS, PAIR), _bf16),
            pltpu.VMEM((GROUP, 2 * QBLK, 2 * QBLK), _f32),
        ],
    )
    return pl.pallas_call(
        _attn_prompt_kernel,
        grid_spec=grid_spec,
        out_shape=[
            jax.ShapeDtypeStruct((N_MAIN, Q_DIM), _bf16),
            jax.ShapeDtypeStruct((BATCH, WINDOW, KV_DIM), _f32),
            jax.ShapeDtypeStruct((BATCH, WINDOW, KV_DIM), _f32),
        ],
        compiler_params=_params(("arbitrary", "arbitrary")),
        name="attn_prompt",
    )(sinks, qm, km, vm, qkv_s, qkv_s)


SEQ_CHUNK = 8
N_QROWS = N_HEADS * DEC_SEQ
N_KEYS = WINDOW + DEC_SEQ
N_KEYS_PAD = ((N_KEYS + 7) // 8) * 8


def _attn_sample_kernel(q_ref, kn_ref, vn_ref, ck_ref, cv_ref, sink_ref,
                        o_ref, nk_ref, nv_ref, kk_buf, vv_buf):
    q4 = q_ref[...]
    shp = (SEQ_CHUNK, N_QROWS, KV_DIM)
    row_kv = lax.broadcasted_iota(jnp.int32, shp, 1) // (GROUP * DEC_SEQ)
    lane_kv = lax.broadcasted_iota(jnp.int32, shp, 2) // HEAD_DIM
    own = row_kv == lane_kv
    qp = jnp.where(own, q4, 0.0)

    kk_buf[:, 0:WINDOW, :] = ck_ref[...]
    kk_buf[:, WINDOW:N_KEYS, :] = kn_ref[...]
    kk_buf[:, N_KEYS:N_KEYS_PAD, :] = jnp.zeros((SEQ_CHUNK, N_KEYS_PAD - N_KEYS, KV_DIM), _f32)
    vv_buf[:, 0:WINDOW, :] = cv_ref[...]
    vv_buf[:, WINDOW:N_KEYS, :] = vn_ref[...]
    vv_buf[:, N_KEYS:N_KEYS_PAD, :] = jnp.zeros((SEQ_CHUNK, N_KEYS_PAD - N_KEYS, KV_DIM), _f32)
    nk_ref[...] = kk_buf[:, DEC_SEQ:N_KEYS, :]
    nv_ref[...] = vv_buf[:, DEC_SEQ:N_KEYS, :]

    kk = kk_buf[...]
    vv = vv_buf[...]
    s = jnp.einsum('bqd,bkd->bqk', qp, kk, preferred_element_type=_f32)

    row = lax.broadcasted_iota(jnp.int32, (N_QROWS, N_KEYS_PAD), 0)
    col = lax.broadcasted_iota(jnp.int32, (N_QROWS, N_KEYS_PAD), 1)
    step = row % DEC_SEQ
    head = (row // DEC_SEQ + 1).astype(_f32)
    dist = step + WINDOW - col
    valid = (dist >= 0) & (dist <= WINDOW) & (col < N_KEYS)
    slope = jnp.exp2(head * (-8.0 / N_HEADS))
    bias = jnp.where(valid, -slope * dist.astype(_f32) * LOG2E, NEG_INF)
    s = jnp.where(valid[None], s + bias[None], NEG_INF)
    sink = sink_ref[...][None] * LOG2E
    m = jnp.maximum(jnp.max(s, axis=-1, keepdims=True), sink)
    e = jnp.exp2(s - m)
    denom = jnp.sum(e, axis=-1, keepdims=True) + jnp.exp2(sink - m)
    o = jnp.einsum('bqk,bkd->bqd', e, vv, preferred_element_type=_f32) / denom
    o = jnp.where(own, o, 0.0)
    acc = o
    for g in range(1, N_KV_HEADS):
        acc = acc + pltpu.roll(o, g * HEAD_DIM, axis=2)
    o_ref[...] = acc[:, :, 0:HEAD_DIM]


def _attn_sample(q_s, k_new, v_new, cache_k, cache_v, sink_rows):
    c = SEQ_CHUNK
    return pl.pallas_call(
        _attn_sample_kernel,
        grid=(DEC_BATCH // c,),
        in_specs=[
            pl.BlockSpec((c, N_QROWS, KV_DIM), lambda i: (i, 0, 0)),
            pl.BlockSpec((c, DEC_SEQ, KV_DIM), lambda i: (i, 0, 0)),
            pl.BlockSpec((c, DEC_SEQ, KV_DIM), lambda i: (i, 0, 0)),
            pl.BlockSpec((c, WINDOW, KV_DIM), lambda i: (i, 0, 0)),
            pl.BlockSpec((c, WINDOW, KV_DIM), lambda i: (i, 0, 0)),
            pl.BlockSpec((N_QROWS, 1), lambda i: (0, 0)),
        ],
        out_specs=[
            pl.BlockSpec((c, N_QROWS, HEAD_DIM), lambda i: (i, 0, 0)),
            pl.BlockSpec((c, WINDOW, KV_DIM), lambda i: (i, 0, 0)),
            pl.BlockSpec((c, WINDOW, KV_DIM), lambda i: (i, 0, 0)),
        ],
        out_shape=[
            jax.ShapeDtypeStruct((DEC_BATCH, N_QROWS, HEAD_DIM), _f32),
            jax.ShapeDtypeStruct((DEC_BATCH, WINDOW, KV_DIM), _f32),
            jax.ShapeDtypeStruct((DEC_BATCH, WINDOW, KV_DIM), _f32),
        ],
        scratch_shapes=[
            pltpu.VMEM((c, N_KEYS_PAD, KV_DIM), _f32),
            pltpu.VMEM((c, N_KEYS_PAD, KV_DIM), _f32),
        ],
        compiler_params=_params(("arbitrary",)),
        name="attn_sample",
    )(q_s, k_new, v_new, cache_k, cache_v, sink_rows)


def _gate_kernel(*refs):
    h_ref, cb_ref, atm_ref, ats_ref = refs[:4]
    w_refs = refs[4:4 + 4 * GATE_PIECES]
    tm_ref, ts_ref, at_all = refs[4 + 4 * GATE_PIECES:]
    at_all[0:TM, :] = atm_ref[...]
    at_all[TM:, :] = ats_ref[...]
    h = h_ref[...]
    cb = cb_ref[...]
    at = at_all[...]
    for c in range(GATE_PIECES):
        wgc_ref, wga_ref, wc_ref, wa_ref = w_refs[4 * c:4 * c + 4]
        cols = slice(c * GATE_PIECE, (c + 1) * GATE_PIECE)
        gc = jnp.dot(h, wgc_ref[...], preferred_element_type=_f32)
        ga = jnp.dot(h, wga_ref[...], preferred_element_type=_f32)
        yc = jnp.dot(cb, wc_ref[...], preferred_element_type=_f32)
        ya = jnp.dot(at, wa_ref[...], preferred_element_type=_f32)
        t = (jax.nn.sigmoid(gc) * yc + jax.nn.sigmoid(ga) * ya).astype(_bf16)
        tm_ref[:, cols] = t[0:TM, :]
        ts_ref[:, cols] = t[TM:, :]


def _gate(h_all, cb_all, atm, ats, w_in_b, wc, wa):
    row = lambda n, i: (i, 0)
    once = pl.Buffered(1)
    w_specs, w_args = [], []
    for c in range(GATE_PIECES):
        piece = lambda n, i, c=c, base=0: (0, base + n * GATE_PIECES + c)
        w_specs += [
            pl.BlockSpec((D_MODEL, GATE_PIECE),
                         functools.partial(piece, base=OFF_GC // GATE_PIECE), pipeline_mode=once),
            pl.BlockSpec((D_MODEL, GATE_PIECE),
                         functools.partial(piece, base=OFF_GA // GATE_PIECE), pipeline_mode=once),
            pl.BlockSpec((D_CONV, GATE_PIECE), piece, pipeline_mode=once),
            pl.BlockSpec((Q_DIM, GATE_PIECE), piece, pipeline_mode=once),
        ]
        w_args += [w_in_b, w_in_b, wc, wa]
    return pl.pallas_call(
        _gate_kernel,
        grid=(N_OUT_CHUNKS, N_TILES),
        in_specs=[
            pl.BlockSpec((TILE, D_MODEL), row),
            pl.BlockSpec((TILE, D_CONV), row),
            pl.BlockSpec((TM, Q_DIM), row),
            pl.BlockSpec((TS, Q_DIM), row),
        ] + w_specs,
        out_specs=[
            pl.BlockSpec((TM, TN_OUT), lambda n, i: (i, n)),
            pl.BlockSpec((TS, TN_OUT), lambda n, i: (i, n)),
        ],
        out_shape=[
            jax.ShapeDtypeStruct((N_MAIN, D_MODEL), _bf16),
            jax.ShapeDtypeStruct((N_SMALL, D_MODEL), _bf16),
        ],
        scratch_shapes=[pltpu.VMEM((TILE, Q_DIM), _bf16)],
        compiler_params=_params(("arbitrary", "arbitrary")),
        name="gate",
    )(h_all, cb_all, atm, ats, *w_args)


def _sample_rows(a):
    return a.transpose(1, 0, 2).reshape(N_SAMPLE, a.shape[-1])


def _to_small(sample_rows, dtype):
    w = sample_rows.shape[-1]
    return jnp.concatenate([
        jnp.zeros((SAMPLE_ROW0, w), dtype), sample_rows.astype(dtype),
        jnp.zeros((N_SMALL - SAMPLE_ROW0 - N_SAMPLE, w), dtype)], axis=0)


def kernel(x_prompt, x_sample, state_conv, cache_k_win, cache_v_win, meta_tokens, ffn1_norm, ffn1_w_up, ffn1_w_down, mix_norm, w_in, q_norm, k_norm, conv_w, w_conv_out, attn_sinks, w_attn_out, w_o, ffn2_norm, ffn2_w_up, ffn2_w_down):
    l = 0
    xm = x_prompt.reshape(N_MAIN, D_MODEL)
    xs = jnp.concatenate([
        meta_tokens.astype(_f32), _sample_rows(x_sample),
        jnp.zeros((N_SMALL - N_META - N_SAMPLE, D_MODEL), _f32)], axis=0)

    g1 = ffn1_norm[l].reshape(1, D_MODEL)
    g_mix = mix_norm[l].reshape(1, D_MODEL)
    *first_tile, wgu1, wd1 = _ffn_first(xm, xs, g1, g_mix, ffn1_w_up[l], ffn1_w_down[l])
    jobs = (_w_up_job(16), _w_down_job(32), _cast_job(D_MODEL, D_IN, 16),
            _cast_job(D_CONV, D_MODEL, 16), _cast_job(Q_DIM, D_MODEL, 16),
            _cast_job(D_MODEL, D_MODEL, 16))
    x1m, x1s, h_all, wgu2, wd2, w_in_b, wc_b, wa_b, wo_b = _ffn_main(
        xm, xs, g1, wgu1, wd1, g_mix, first_tile, jobs,
        (ffn2_w_up[l], ffn2_w_down[l], w_in[l], w_conv_out[l], w_attn_out[l], w_o[l]))

    state_rows = state_conv[l].transpose(1, 0, 2).reshape((CONV_WIDTH - 1) * DEC_BATCH, D_CONV)
    cb_all, qm, km, vm, qkv_s, new_conv_p, new_conv_s = _inproj(
        h_all, w_in_b, conv_w[l], state_rows, q_norm[l], k_norm[l])

    atm, new_k_p, new_v_p = _attn_prompt(qm, km, vm, qkv_s, attn_sinks[l])
    samp = qkv_s[SAMPLE_ROW0:SAMPLE_ROW0 + N_SAMPLE]
    q_s = (samp[:, 0:Q_DIM].reshape(DEC_SEQ, DEC_BATCH, N_HEADS, HEAD_DIM)
           .transpose(1, 2, 0, 3).reshape(DEC_BATCH, N_QROWS, HEAD_DIM))
    q_s = jnp.tile(q_s, (1, 1, N_KV_HEADS))
    k_new = samp[:, Q_DIM:Q_DIM + KV_DIM].reshape(DEC_SEQ, DEC_BATCH, KV_DIM).transpose(1, 0, 2)
    v_new = samp[:, Q_DIM + KV_DIM:N_QKV].reshape(DEC_SEQ, DEC_BATCH, KV_DIM).transpose(1, 0, 2)
    sink_rows = jnp.repeat(attn_sinks[l].astype(_f32), DEC_SEQ).reshape(N_QROWS, 1)
    o_s, new_k_s, new_v_s = _attn_sample(
        q_s, k_new, v_new,
        cache_k_win[l].reshape(DEC_BATCH, WINDOW, KV_DIM),
        cache_v_win[l].reshape(DEC_BATCH, WINDOW, KV_DIM), sink_rows)
    at_rows = (o_s.reshape(DEC_BATCH, N_HEADS, DEC_SEQ, HEAD_DIM)
               .transpose(2, 0, 1, 3).reshape(N_SAMPLE, Q_DIM))
    ats = _to_small(at_rows, _bf16)

    t_main, t_small = _gate(h_all, cb_all, atm, ats, w_in_b, wc_b, wa_b)
    ym, ys = _ffn_stream(x1m, x1s, ffn2_norm[l].reshape(1, D_MODEL), wgu2, wd2,
                         proj=(t_main, t_small, wo_b))

    y_prompt = ym.reshape(BATCH, SEQ, D_MODEL)
    y_sample = (ys[SAMPLE_ROW0:SAMPLE_ROW0 + N_SAMPLE]
                .reshape(DEC_SEQ, DEC_BATCH, D_MODEL).transpose(1, 0, 2))
    kv_shape_p = (1, BATCH, WINDOW, N_KV_HEADS, HEAD_DIM)
    kv_shape_s = (1, DEC_BATCH, WINDOW, N_KV_HEADS, HEAD_DIM)
    return (y_prompt, y_sample,
            new_conv_p[None],
            new_k_p.reshape(kv_shape_p), new_v_p.reshape(kv_shape_p),
            new_conv_s.reshape(CONV_WIDTH - 1, DEC_BATCH, D_CONV).transpose(1, 0, 2)[None],
            new_k_s.reshape(kv_shape_s), new_v_s.reshape(kv_shape_s))
```

```python
import functools
from typing import NamedTuple

import jax
import jax.numpy as jnp
from jax import lax
from jax.experimental import pallas as pl
from jax.experimental.pallas import tpu as pltpu

D_MODEL = 2048
BATCH = 4
SEQ = 2048
DEC_BATCH = 32
DEC_SEQ = 4
PAST_LEN = 16384
N_META = 16
D_CONV = D_MODEL // 2
CONV_WIDTH = 3
HEAD_DIM = 64
N_HEADS = (D_MODEL // 2) // HEAD_DIM
N_KV_HEADS = N_HEADS // 4
GROUP = N_HEADS // N_KV_HEADS
Q_DIM = N_HEADS * HEAD_DIM
KV_DIM = N_KV_HEADS * HEAD_DIM
PAIR = 2 * HEAD_DIM
WINDOW = 128
D_FF = ((8 * D_MODEL // 3 + 127) // 128) * 128
D_IN = 3 * D_CONV + Q_DIM + 2 * KV_DIM + 2 * D_MODEL
EPS = 1e-6
NEG_INF = -1e30
LOG2E = 1.4426950408889634
Q_SCALE = HEAD_DIM ** -0.5 * LOG2E

OFF_XC = 0
OFF_BG = D_CONV
OFF_CG = 2 * D_CONV
OFF_Q = 3 * D_CONV
OFF_K = OFF_Q + Q_DIM
OFF_V = OFF_K + KV_DIM
OFF_GC = OFF_V + KV_DIM
OFF_GA = OFF_GC + D_MODEL
N_QKV = Q_DIM + 2 * KV_DIM

N_MAIN = BATCH * SEQ
N_SAMPLE = DEC_BATCH * DEC_SEQ
N_SMALL = 256
SAMPLE_ROW0 = N_META

N_TILES = 16
TM = N_MAIN // N_TILES
TS = N_SMALL // N_TILES
TILE = TM + TS
N_ALL = N_TILES * TILE
TILES_PER_SEQ = SEQ // TM

TF = 512
D_FF_PAD = ((D_FF + TF - 1) // TF) * TF
N_FF_CHUNKS = D_FF_PAD // TF
FFN_STEPS = N_TILES * N_FF_CHUNKS
TN_OUT = 1024
N_OUT_CHUNKS = D_MODEL // TN_OUT
GATE_PIECE = 512
GATE_PIECES = TN_OUT // GATE_PIECE

QBLK = 128
KPAD = QBLK - N_META
N_QBLK = SEQ // QBLK
KBUF_ROWS = QBLK + SEQ

VMEM_LIMIT = 56 * 1024 * 1024

_bf16 = jnp.bfloat16
_f32 = jnp.float32


def _params(sem):
    return pltpu.CompilerParams(dimension_semantics=sem, vmem_limit_bytes=VMEM_LIMIT)


def _rms_rows(x, g):
    ms = jnp.mean(x * x, axis=-1, keepdims=True)
    return x * lax.rsqrt(ms + EPS) * g


def _pair_rms(x, g_pair):
    low = lax.broadcasted_iota(jnp.int32, x.shape, 1) < HEAD_DIM
    x2 = x * x
    s_lo = jnp.sum(jnp.where(low, x2, 0.0), axis=-1, keepdims=True)
    s_hi = jnp.sum(jnp.where(low, 0.0, x2), axis=-1, keepdims=True)
    ms = jnp.where(low, s_lo, s_hi) * (1.0 / HEAD_DIM)
    return x * lax.rsqrt(ms + EPS) * g_pair


class _CastJob(NamedTuple):
    slab: int
    rows_in: int
    n_in_slabs: int
    n_out_slabs: int
    n_in_cols: int
    n_out_cols: int
    segments: tuple
    zero_ranges: tuple


def _cast_job(rows_in, cols_in, slab, segments=None, zero_ranges=(), rows_out=None, cols_out=None):
    rows_out = rows_in if rows_out is None else rows_out
    cols_out = cols_in if cols_out is None else cols_out
    segments = ((0, cols_in, 0),) if segments is None else segments
    assert rows_out % slab == 0
    return _CastJob(slab, rows_in, -(-rows_in // slab), rows_out // slab, cols_in, cols_out,
                    tuple(segments), tuple(zero_ranges))


def _job_specs(job, step_of):
    in_idx = lambda *ids: (jnp.minimum(step_of(*ids), job.n_in_slabs - 1), 0)
    out_idx = lambda *ids: (jnp.minimum(step_of(*ids), job.n_out_slabs - 1), 0)
    return (pl.BlockSpec((job.slab, job.n_in_cols), in_idx),
            pl.BlockSpec((job.slab, job.n_out_cols), out_idx))


def _run_cast_job(job, step, x_ref, o_ref):
    out_slab = jnp.minimum(step, job.n_out_slabs - 1)
    padded = job.n_out_slabs * job.slab > job.rows_in
    for src, width, dst in job.segments:
        x = x_ref[:, src:src + width]
        if padded:
            row = out_slab * job.slab + lax.broadcasted_iota(jnp.int32, x.shape, 0)
            x = jnp.where(row < job.rows_in, x, 0.0)
        o_ref[:, dst:dst + width] = x.astype(_bf16)
    for lo, hi in job.zero_ranges:
        o_ref[:, lo:hi] = jnp.zeros((job.slab, hi - lo), _bf16)


def _w_up_job(slab):
    segments, zero_ranges = [], []
    for c in range(N_FF_CHUNKS):
        width = min(TF, D_FF - c * TF)
        for half, src0 in enumerate((0, D_FF)):
            dst = (2 * c + half) * TF
            segments.append((src0 + c * TF, width, dst))
            if width < TF:
                zero_ranges.append((dst + width, dst + TF))
    return _cast_job(D_MODEL, 2 * D_FF, slab, segments=segments, zero_ranges=zero_ranges,
                     cols_out=2 * D_FF_PAD)


def _w_down_job(slab):
    return _cast_job(D_FF, D_MODEL, slab, rows_out=D_FF_PAD)


FC = 256
N_FC = D_FF_PAD // FC
assert TF % FC == 0


def _swiglu_chunk(h, w_gate_up, w_down, n):
    gu = jnp.dot(h, w_gate_up, preferred_element_type=_f32)
    gate, up = gu[:, 0:n], gu[:, n:2 * n]
    a = (gate * jax.nn.sigmoid(gate) * up * 0.5).astype(_bf16)
    return jnp.dot(a, w_down, preferred_element_type=_f32)


def _ffn_first_kernel(xm_ref, xs_ref, g_ref, gn_ref, wup_hbm, wdn_hbm,
                      om_ref, os_ref, hn_ref, wgu_out, wd_out,
                      h_ref, stage_g, stage_u, stage_d, cast_gu, cast_d, sem_in, sem_out):
    def width(c):
        return min(FC, D_FF - c * FC)

    def in_copies(c):
        s, w = c % 2, width(c)
        return (
            pltpu.make_async_copy(wup_hbm.at[:, pl.ds(c * FC, w)],
                                  stage_g.at[s, :, pl.ds(0, w)], sem_in.at[0, s]),
            pltpu.make_async_copy(wup_hbm.at[:, pl.ds(D_FF + c * FC, w)],
                                  stage_u.at[s, :, pl.ds(0, w)], sem_in.at[1, s]),
            pltpu.make_async_copy(wdn_hbm.at[pl.ds(c * FC, w), :],
                                  stage_d.at[s, pl.ds(0, w), :], sem_in.at[2, s]),
        )

    def out_copies(c):
        s = c % 2
        col = (c * FC // TF) * 2 * TF + (c * FC) % TF
        return (
            pltpu.make_async_copy(cast_gu.at[s, :, pl.ds(0, FC)],
                                  wgu_out.at[:, pl.ds(col, FC)], sem_out.at[0, s]),
            pltpu.make_async_copy(cast_gu.at[s, :, pl.ds(FC, FC)],
                                  wgu_out.at[:, pl.ds(col + TF, FC)], sem_out.at[1, s]),
            pltpu.make_async_copy(cast_d.at[s], wd_out.at[pl.ds(c * FC, FC), :], sem_out.at[2, s]),
        )

    def start(copies):
        for cp in copies:
            cp.start()

    def wait(copies):
        for cp in copies:
            cp.wait()

    start(in_copies(0))
    g = g_ref[...]
    xm = xm_ref[...]
    xs = xs_ref[...]
    h_ref[0:TM, :] = _rms_rows(xm, g).astype(_bf16)
    om_ref[...] = xm
    h_ref[TM:, :] = _rms_rows(xs, g).astype(_bf16)
    os_ref[...] = xs
    h = h_ref[...]

    for c in range(N_FC):
        s, w = c % 2, width(c)
        if c + 1 < N_FC:
            start(in_copies(c + 1))
        wait(in_copies(c))
        if c >= 2:
            wait(out_copies(c - 2))
        if w == FC:
            cast_gu[s, :, 0:FC] = stage_g[s].astype(_bf16)
            cast_gu[s, :, FC:2 * FC] = stage_u[s].astype(_bf16)
            cast_d[s] = stage_d[s].astype(_bf16)
        else:
            lane = lax.broadcasted_iota(jnp.int32, (D_MODEL, FC), 1)
            sub = lax.broadcasted_iota(jnp.int32, (FC, D_MODEL), 0)
            cast_gu[s, :, 0:FC] = jnp.where(lane < w, stage_g[s], 0.0).astype(_bf16)
            cast_gu[s, :, FC:2 * FC] = jnp.where(lane < w, stage_u[s], 0.0).astype(_bf16)
            cast_d[s] = jnp.where(sub < w, stage_d[s], 0.0).astype(_bf16)
        start(out_copies(c))
        r = _swiglu_chunk(h, cast_gu[s], cast_d[s], FC)
        om_ref[...] += r[0:TM, :]
        os_ref[...] += r[TM:, :]
    wait(out_copies(N_FC - 2))
    wait(out_copies(N_FC - 1))

    gn = gn_ref[...]
    hn_ref[0:TM, :] = _rms_rows(om_ref[...], gn).astype(_bf16)
    hn_ref[TM:, :] = _rms_rows(os_ref[...], gn).astype(_bf16)


def _ffn_first(xm, xs, g, next_gain, w_up, w_down):
    first = lambda i: (0, 0)
    return pl.pallas_call(
        _ffn_first_kernel,
        grid=(1,),
        in_specs=[
            pl.BlockSpec((TM, D_MODEL), first),
            pl.BlockSpec((TS, D_MODEL), first),
            pl.BlockSpec((1, D_MODEL), first),
            pl.BlockSpec((1, D_MODEL), first),
            pl.BlockSpec(memory_space=pl.ANY),
            pl.BlockSpec(memory_space=pl.ANY),
        ],
        out_specs=[
            pl.BlockSpec((TM, D_MODEL), first),
            pl.BlockSpec((TS, D_MODEL), first),
            pl.BlockSpec((TILE, D_MODEL), first),
            pl.BlockSpec(memory_space=pl.ANY),
            pl.BlockSpec(memory_space=pl.ANY),
        ],
        out_shape=[
            jax.ShapeDtypeStruct((TM, D_MODEL), _f32),
            jax.ShapeDtypeStruct((TS, D_MODEL), _f32),
            jax.ShapeDtypeStruct((TILE, D_MODEL), _bf16),
            jax.ShapeDtypeStruct((D_MODEL, 2 * D_FF_PAD), _bf16),
            jax.ShapeDtypeStruct((D_FF_PAD, D_MODEL), _bf16),
        ],
        scratch_shapes=[
            pltpu.VMEM((TILE, D_MODEL), _bf16),
            pltpu.VMEM((2, D_MODEL, FC), _f32),
            pltpu.VMEM((2, D_MODEL, FC), _f32),
            pltpu.VMEM((2, FC, D_MODEL), _f32),
            pltpu.VMEM((2, D_MODEL, 2 * FC), _bf16),
            pltpu.VMEM((2, FC, D_MODEL), _bf16),
            pltpu.SemaphoreType.DMA((3, 2)),
            pltpu.SemaphoreType.DMA((3, 2)),
        ],
        compiler_params=_params(("arbitrary",)),
        name="ffn_first",
    )(xm, xs, g, next_gain, w_up, w_down)


def _ffn_main_kernel(jobs, *refs):
    refs = list(refs)
    take = lambda n: [refs.pop(0) for _ in range(n)]
    xm_ref, xs_ref, g_ref, wgu_ref, wd_ref, gn_ref, om0_hbm, os0_hbm, hn0_hbm = take(9)
    job_in = take(len(jobs))
    om_ref, os_ref, hn_ref = take(3)
    job_out = take(len(jobs))
    h_ref, sem = refs
    i = pl.program_id(0)
    j = pl.program_id(1)

    def run_jobs():
        for job, x_ref, o_ref in zip(jobs, job_in, job_out):
            _run_cast_job(job, i * N_FF_CHUNKS + j, x_ref, o_ref)

    @pl.when(i == 0)
    def _():
        @pl.when(j == 0)
        def _():
            copies = (pltpu.make_async_copy(om0_hbm, om_ref, sem.at[0]),
                      pltpu.make_async_copy(os0_hbm, os_ref, sem.at[1]),
                      pltpu.make_async_copy(hn0_hbm, hn_ref, sem.at[2]))
            for cp in copies:
                cp.start()
            for cp in copies:
                cp.wait()
        run_jobs()

    @pl.when(i > 0)
    def _():
        @pl.when(j == 0)
        def _():
            g = g_ref[...]
            xm = xm_ref[...]
            xs = xs_ref[...]
            h_ref[0:TM, :] = _rms_rows(xm, g).astype(_bf16)
            om_ref[...] = xm
            h_ref[TM:, :] = _rms_rows(xs, g).astype(_bf16)
            os_ref[...] = xs

        run_jobs()
        r = _swiglu_chunk(h_ref[...], wgu_ref[...], wd_ref[...], TF)
        om_ref[...] += r[0:TM, :]
        os_ref[...] += r[TM:, :]

        @pl.when(j == N_FF_CHUNKS - 1)
        def _():
            gn = gn_ref[...]
            hn_ref[0:TM, :] = _rms_rows(om_ref[...], gn).astype(_bf16)
            hn_ref[TM:, :] = _rms_rows(os_ref[...], gn).astype(_bf16)


def _ffn_main(xm, xs, g, wgu, wd, next_gain, first_tile, jobs, job_inputs):
    assert all(job.n_out_slabs <= FFN_STEPS for job in jobs)
    job_specs = [_job_specs(job, lambda i, j: i * N_FF_CHUNKS + j) for job in jobs]
    row = lambda i, j: (i, 0)
    fixed = lambda i, j: (0, 0)
    chunk = lambda i, j: jnp.where(i == 0, 0, j)
    return pl.pallas_call(
        functools.partial(_ffn_main_kernel, tuple(jobs)),
        grid=(N_TILES, N_FF_CHUNKS),
        in_specs=[
            pl.BlockSpec((TM, D_MODEL), row),
            pl.BlockSpec((TS, D_MODEL), row),
            pl.BlockSpec((1, D_MODEL), fixed),
            pl.BlockSpec((D_MODEL, 2 * TF), lambda i, j: (0, chunk(i, j))),
            pl.BlockSpec((TF, D_MODEL), lambda i, j: (chunk(i, j), 0)),
            pl.BlockSpec((1, D_MODEL), fixed),
        ] + [pl.BlockSpec(memory_space=pl.ANY) for _ in first_tile] + [s[0] for s in job_specs],
        out_specs=[
            pl.BlockSpec((TM, D_MODEL), row),
            pl.BlockSpec((TS, D_MODEL), row),
            pl.BlockSpec((TILE, D_MODEL), row),
        ] + [s[1] for s in job_specs],
        out_shape=[
            jax.ShapeDtypeStruct((N_MAIN, D_MODEL), _f32),
            jax.ShapeDtypeStruct((N_SMALL, D_MODEL), _f32),
            jax.ShapeDtypeStruct((N_ALL, D_MODEL), _bf16),
        ] + [jax.ShapeDtypeStruct((job.n_out_slabs * job.slab, job.n_out_cols), _bf16)
             for job in jobs],
        scratch_shapes=[pltpu.VMEM((TILE, D_MODEL), _bf16), pltpu.SemaphoreType.DMA((3,))],
        compiler_params=_params(("arbitrary", "arbitrary")),
        name="ffn",
    )(xm, xs, g, wgu, wd, next_gain, *first_tile, *job_inputs)


def _ffn_stream_kernel(xm_ref, xs_ref, tm_ref, ts_ref, wo_ref, g_ref, wgu_hbm, wd_hbm,
                       om_ref, os_ref, h_ref, t_all, wgu_buf, wd_buf, sem):
    i = pl.program_id(0)

    def chunk_copies(chunk, slot):
        return (
            pltpu.make_async_copy(wgu_hbm.at[:, pl.ds(chunk * 2 * TF, 2 * TF)],
                                  wgu_buf.at[slot], sem.at[0, slot]),
            pltpu.make_async_copy(wd_hbm.at[pl.ds(chunk * TF, TF), :],
                                  wd_buf.at[slot], sem.at[1, slot]),
        )

    def start(chunk, slot):
        for c in chunk_copies(chunk, slot):
            c.start()

    @pl.when(i == 0)
    def _():
        start(0, 0)

    g = g_ref[...]
    t_all[0:TM, :] = tm_ref[...]
    t_all[TM:, :] = ts_ref[...]
    p = jnp.dot(t_all[...], wo_ref[...], preferred_element_type=_f32)
    xm = xm_ref[...] + p[0:TM, :]
    xs = xs_ref[...] + p[TM:, :]
    h_ref[0:TM, :] = _rms_rows(xm, g).astype(_bf16)
    om_ref[...] = xm
    h_ref[TM:, :] = _rms_rows(xs, g).astype(_bf16)
    os_ref[...] = xs

    h = h_ref[...]
    for j in range(N_FF_CHUNKS):
        slot = (i * N_FF_CHUNKS + j) % 2
        if j + 1 < N_FF_CHUNKS:
            start(j + 1, 1 - slot)
        else:
            @pl.when(i + 1 < N_TILES)
            def _():
                start(0, 1 - slot)
        for c in chunk_copies(j, slot):
            c.wait()
        r = _swiglu_chunk(h, wgu_buf[slot], wd_buf[slot], TF)
        om_ref[...] += r[0:TM, :]
        os_ref[...] += r[TM:, :]


def _ffn_stream(xm, xs, g, wgu, wd, proj):
    row = lambda i: (i, 0)
    fixed = lambda i: (0, 0)
    return pl.pallas_call(
        _ffn_stream_kernel,
        grid=(N_TILES,),
        in_specs=[
            pl.BlockSpec((TM, D_MODEL), row),
            pl.BlockSpec((TS, D_MODEL), row),
            pl.BlockSpec((TM, D_MODEL), row),
            pl.BlockSpec((TS, D_MODEL), row),
            pl.BlockSpec((D_MODEL, D_MODEL), fixed, pipeline_mode=pl.Buffered(1)),
            pl.BlockSpec((1, D_MODEL), fixed),
            pl.BlockSpec(memory_space=pl.ANY),
            pl.BlockSpec(memory_space=pl.ANY),
        ],
        out_specs=[pl.BlockSpec((TM, D_MODEL), row), pl.BlockSpec((TS, D_MODEL), row)],
        out_shape=[jax.ShapeDtypeStruct((N_MAIN, D_MODEL), _f32),
                   jax.ShapeDtypeStruct((N_SMALL, D_MODEL), _f32)],
        scratch_shapes=[
            pltpu.VMEM((TILE, D_MODEL), _bf16),
            pltpu.VMEM((TILE, D_MODEL), _bf16),
            pltpu.VMEM((2, D_MODEL, 2 * TF), _bf16),
            pltpu.VMEM((2, TF, D_MODEL), _bf16),
            pltpu.SemaphoreType.DMA((2, 2)),
        ],
        compiler_params=_params(("arbitrary",)),
        name="ffn_stream",
    )(xm, xs, *proj, g, wgu, wd)


N_BRANCH_IN = OFF_GC
CONV_HALO = 8
U_BASE = 112
U_STATE0 = U_BASE + SAMPLE_ROW0 - (CONV_WIDTH - 1) * DEC_BATCH
U_ROWS = U_BASE + N_SMALL
U_LAST0 = U_BASE + SAMPLE_ROW0 + (DEC_SEQ - (CONV_WIDTH - 1)) * DEC_BATCH
assert U_BASE >= (CONV_WIDTH - 1) * DEC_BATCH and U_BASE % TS == 0 and TS == N_META


def _inproj_kernel(h_ref, w_ref, cw_ref, st_ref, qg_ref, kg_ref,
                   cb_ref, q_ref, k_ref, v_ref, sm_ref, ncp_ref, ncs_ref,
                   ubuf, meta_halo, utab):
    i = pl.program_id(0)

    @pl.when(i == 0)
    def _():
        utab[...] = jnp.zeros(utab.shape, _f32)
        utab[U_STATE0:U_STATE0 + (CONV_WIDTH - 1) * DEC_BATCH, :] = st_ref[...]
        meta_halo[...] = jnp.zeros(meta_halo.shape, _f32)
        ubuf[0:CONV_HALO, :] = jnp.zeros((CONV_HALO, D_CONV), _f32)

    h = h_ref[...]
    col_dot = lambda off, width: jnp.dot(h, w_ref[:, off:off + width],
                                         preferred_element_type=_f32)
    z_q = col_dot(OFF_Q, Q_DIM)
    z_kv = col_dot(OFF_K, 2 * KV_DIM)
    z_xc = col_dot(OFF_XC, D_CONV)
    z_cg = col_dot(OFF_CG, D_CONV)
    z_bg = col_dot(OFF_BG, D_CONV)
    cw = cw_ref[...]
    w0, w1, w2 = cw[0:1, :], cw[1:2, :], cw[2:3, :]
    qg = qg_ref[...]
    kg = kg_ref[...]

    def qkv(rows, q_out, k_out, v_out):
        for t in range(Q_DIM // PAIR):
            q_out(t, _pair_rms(z_q[rows, t * PAIR:(t + 1) * PAIR], qg) * Q_SCALE)
        for t in range(KV_DIM // PAIR):
            k_out(t, _pair_rms(z_kv[rows, t * PAIR:(t + 1) * PAIR], kg))
            v_out(t, z_kv[rows, KV_DIM + t * PAIR:KV_DIM + (t + 1) * PAIR])

    u_s = z_cg[TM:, :] * z_xc[TM:, :]
    cur = pl.multiple_of(U_BASE + TS * i, TS)
    slot = pl.multiple_of(jnp.where(i == 0, U_ROWS, cur), TS)
    utab[pl.ds(slot, TS), :] = u_s
    y_s = (w0 * utab[pl.ds(cur - 2 * DEC_BATCH, TS), :]
           + w1 * utab[pl.ds(cur - DEC_BATCH, TS), :] + w2 * u_s)
    cb_ref[TM:, :] = (z_bg[TM:, :] * y_s).astype(_bf16)
    meta_halo[...] = jnp.where(i == 0, u_s[N_META - CONV_HALO:N_META, :], meta_halo[...])

    def sm_store(off):
        def store(t, val):
            sm_ref[:, off + t * PAIR:off + (t + 1) * PAIR] = val
        return store
    qkv(slice(TM, TILE), sm_store(0), sm_store(Q_DIM), sm_store(Q_DIM + KV_DIM))

    ncs_ref[...] = utab[U_LAST0:U_LAST0 + (CONV_WIDTH - 1) * DEC_BATCH, :]

    ubuf[0:CONV_HALO, :] = jnp.where(i % TILES_PER_SEQ == 0, meta_halo[...], ubuf[0:CONV_HALO, :])

    u = z_cg[0:TM, :] * z_xc[0:TM, :]
    ubuf[CONV_HALO:CONV_HALO + TM, :] = u
    u1 = ubuf[CONV_HALO - 1:CONV_HALO - 1 + TM, :]
    u2 = ubuf[CONV_HALO - 2:CONV_HALO - 2 + TM, :]
    cb_ref[0:TM, :] = (z_bg[0:TM, :] * (w0 * u2 + w1 * u1 + w2 * u)).astype(_bf16)
    tail = ubuf[TM:TM + CONV_HALO, :]
    ubuf[0:CONV_HALO, :] = tail
    ncp_ref[0] = tail[CONV_HALO - (CONV_WIDTH - 1):CONV_HALO, :]

    def lane_store(ref):
        def store(t, val):
            ref[:, t * PAIR:(t + 1) * PAIR] = val
        return store
    qkv(slice(0, TM), lane_store(q_ref), lane_store(k_ref), lane_store(v_ref))


def _inproj(h_all, w_in_b, conv_w, state_rows, q_gain, k_gain):
    qg = jnp.tile(q_gain.reshape(1, HEAD_DIM), (1, 2))
    kg = jnp.tile(k_gain.reshape(1, HEAD_DIM), (1, 2))
    row = lambda i: (i, 0)
    fixed = lambda i: (0, 0)
    return pl.pallas_call(
        _inproj_kernel,
        grid=(N_TILES,),
        in_specs=[
            pl.BlockSpec((TILE, D_MODEL), row),
            pl.BlockSpec((D_MODEL, N_BRANCH_IN), fixed, pipeline_mode=pl.Buffered(1)),
            pl.BlockSpec((CONV_WIDTH, D_CONV), fixed),
            pl.BlockSpec(((CONV_WIDTH - 1) * DEC_BATCH, D_CONV), fixed),
            pl.BlockSpec((1, PAIR), fixed),
            pl.BlockSpec((1, PAIR), fixed),
        ],
        out_specs=[
            pl.BlockSpec((TILE, D_CONV), row),
            pl.BlockSpec((TM, Q_DIM), row),
            pl.BlockSpec((TM, KV_DIM), row),
            pl.BlockSpec((TM, KV_DIM), row),
            pl.BlockSpec((TS, N_QKV), row),
            pl.BlockSpec((1, CONV_WIDTH - 1, D_CONV), lambda i: (i // TILES_PER_SEQ, 0, 0)),
            pl.BlockSpec(((CONV_WIDTH - 1) * DEC_BATCH, D_CONV), fixed),
        ],
        out_shape=[
            jax.ShapeDtypeStruct((N_ALL, D_CONV), _bf16),
            jax.ShapeDtypeStruct((N_MAIN, Q_DIM), _f32),
            jax.ShapeDtypeStruct((N_MAIN, KV_DIM), _f32),
            jax.ShapeDtypeStruct((N_MAIN, KV_DIM), _f32),
            jax.ShapeDtypeStruct((N_SMALL, N_QKV), _f32),
            jax.ShapeDtypeStruct((BATCH, CONV_WIDTH - 1, D_CONV), _f32),
            jax.ShapeDtypeStruct(((CONV_WIDTH - 1) * DEC_BATCH, D_CONV), _f32),
        ],
        scratch_shapes=[
            pltpu.VMEM((CONV_HALO + TM, D_CONV), _f32),
            pltpu.VMEM((CONV_HALO, D_CONV), _f32),
            pltpu.VMEM((U_ROWS + TS, D_CONV), _f32),
        ],
        compiler_params=_params(("arbitrary",)),
        name="inproj",
    )(h_all, w_in_b, conv_w, state_rows, qg, kg)


QPAIR = 2 * GROUP * HEAD_DIM
HEADS_PER_STEP = 2 * GROUP


def _attn_prompt_kernel(sink_ref, q_ref, k_ref, v_ref, km_ref, vm_ref,
                        o_ref, nk_ref, nv_ref, kt_buf, vbuf, bias_ref):
    gp = pl.program_id(1)

    for order in range(2):
        swap = (lambda x: x) if order == 0 else (lambda x: pltpu.roll(x, HEAD_DIM, axis=1))
        vbuf[order, 0:KPAD, :] = jnp.zeros((KPAD, PAIR), _bf16)
        vbuf[order, KPAD:QBLK, :] = swap(vm_ref[...]).astype(_bf16)
        vbuf[order, QBLK:KBUF_ROWS, :] = swap(v_ref[...]).astype(_bf16)
        first_keys = jnp.concatenate([jnp.zeros((KPAD, PAIR), _f32), km_ref[...]], axis=0)
        kt_buf[order, 0] = swap(first_keys).T.astype(_bf16)
        for b in range(N_QBLK):
            kt_buf[order, b + 1] = swap(k_ref[b * QBLK:(b + 1) * QBLK, :]).T.astype(_bf16)
    nk_ref[0] = k_ref[SEQ - WINDOW:SEQ, :]
    nv_ref[0] = v_ref[SEQ - WINDOW:SEQ, :]

    partner = lambda p: p ^ 1
    row = lax.broadcasted_iota(jnp.int32, (QBLK, 2 * QBLK), 0)
    col = lax.broadcasted_iota(jnp.int32, (QBLK, 2 * QBLK), 1)
    dist = QBLK + row - col
    in_window = (dist >= 0) & (dist <= WINDOW)
    distf = dist.astype(_f32)
    upper = lax.broadcasted_iota(jnp.int32, (2 * QBLK, 1), 0) < QBLK
    sink_cols = []
    for p in range(GROUP):
        sinks = []
        for g, hp in ((0, p), (1, partner(p))):
            head = gp * HEADS_PER_STEP + g * GROUP + hp
            slope = jnp.exp2(jnp.full((QBLK, 2 * QBLK), -8.0 / N_HEADS, _f32)
                             * (head + 1).astype(_f32))
            bias_ref[p, g * QBLK:(g + 1) * QBLK, :] = jnp.where(
                in_window, -slope * distf * LOG2E, NEG_INF)
            sinks.append(sink_ref[head] * LOG2E)
        sink_cols.append(jnp.where(upper, sinks[0], sinks[1]))

    lane = lax.broadcasted_iota(jnp.int32, (QBLK, PAIR), 1)
    low_half = lane < HEAD_DIM
    pad_col = lax.broadcasted_iota(jnp.int32, (2 * QBLK, 2 * QBLK), 1) < KPAD

    def block(i, first):
        r0 = i * QBLK if first else pl.multiple_of(i * QBLK, QBLK)
        qn = q_ref[pl.ds(r0 - QBLK, QBLK), :]
        outs = [[None] * GROUP for _ in range(2)]
        for p in range(GROUP):
            order = p % 2
            kt = jnp.concatenate([kt_buf[order, i - 1], kt_buf[order, i]], axis=1)
            vv = vbuf[order, pl.ds(r0 - QBLK, 2 * QBLK), :]
            tiles = []
            for g, hp in ((0, p), (1, partner(p))):
                hh = g * GROUP + hp
                in_low = hp % 2 == 0
                tile = qn[:, (hh // 2) * PAIR:(hh // 2 + 1) * PAIR]
                tiles.append(jnp.where(low_half if in_low else jnp.logical_not(low_half),
                                       tile, 0.0).astype(_bf16))
            qst = jnp.concatenate(tiles, axis=0)
            s = jnp.dot(qst, kt, preferred_element_type=_f32) + bias_ref[p]
            if first:
                s = jnp.where(pad_col, NEG_INF, s)
            sink = sink_cols[p]
            m = jnp.maximum(jnp.max(s, axis=-1, keepdims=True), sink)
            e = jnp.exp2(s - m)
            denom = jnp.sum(e, axis=-1, keepdims=True) + jnp.exp2(sink - m)
            o = jnp.dot(e.astype(_bf16), vv, preferred_element_type=_f32) / denom
            outs[0][p] = o[0:QBLK, :]
            outs[1][partner(p)] = o[QBLK:2 * QBLK, :]
        for g in range(2):
            for pr in range(GROUP // 2):
                t = g * (GROUP // 2) + pr
                o_ref[pl.ds(r0 - QBLK, QBLK), t * PAIR:(t + 1) * PAIR] = jnp.where(
                    low_half, outs[g][2 * pr], outs[g][2 * pr + 1]).astype(_bf16)

    block(1, True)

    def later_block(i, carry):
        block(i, False)
        return carry

    lax.fori_loop(2, N_QBLK + 1, later_block, 0)


def _attn_prompt(qm, km, vm, qkv_s, sinks):
    grid_spec = pltpu.PrefetchScalarGridSpec(
        num_scalar_prefetch=1,
        grid=(BATCH, N_KV_HEADS // 2),
        in_specs=[
            pl.BlockSpec((SEQ, QPAIR), lambda b, p, s: (b, p)),
            pl.BlockSpec((SEQ, PAIR), lambda b, p, s: (b, p)),
            pl.BlockSpec((SEQ, PAIR), lambda b, p, s: (b, p)),
            pl.BlockSpec((N_META, PAIR), lambda b, p, s: (0, Q_DIM // PAIR + p)),
            pl.BlockSpec((N_META, PAIR), lambda b, p, s: (0, (Q_DIM + KV_DIM) // PAIR + p)),
        ],
        out_specs=[
            pl.BlockSpec((SEQ, QPAIR), lambda b, p, s: (b, p)),
            pl.BlockSpec((1, WINDOW, PAIR), lambda b, p, s: (b, 0, p)),
            pl.BlockSpec((1, WINDOW, PAIR), lambda b, p, s: (b, 0, p)),
        ],
        scratch_shapes=[
            pltpu.VMEM((2, N_QBLK + 1, PAIR, QBLK), _bf16),
            pltpu.VMEM((2, KBUF_ROWS, PAIR), _bf16),
            pltpu.VMEM((GROUP, 2 * QBLK, 2 * QBLK), _f32),
        ],
    )
    return pl.pallas_call(
        _attn_prompt_kernel,
        grid_spec=grid_spec,
        out_shape=[
            jax.ShapeDtypeStruct((N_MAIN, Q_DIM), _bf16),
            jax.ShapeDtypeStruct((BATCH, WINDOW, KV_DIM), _f32),
            jax.ShapeDtypeStruct((BATCH, WINDOW, KV_DIM), _f32),
        ],
        compiler_params=_params(("arbitrary", "arbitrary")),
        name="attn_prompt",
    )(sinks, qm, km, vm, qkv_s, qkv_s)


SEQ_CHUNK = 8
N_QROWS = N_HEADS * DEC_SEQ
N_KEYS = WINDOW + DEC_SEQ
N_KEYS_PAD = ((N_KEYS + 7) // 8) * 8


def _attn_sample_kernel(q_ref, kn_ref, vn_ref, ck_ref, cv_ref, sink_ref,
                        o_ref, nk_ref, nv_ref, kk_buf, vv_buf):
    q4 = q_ref[...]
    shp = (SEQ_CHUNK, N_QROWS, KV_DIM)
    row_kv = lax.broadcasted_iota(jnp.int32, shp, 1) // (GROUP * DEC_SEQ)
    lane_kv = lax.broadcasted_iota(jnp.int32, shp, 2) // HEAD_DIM
    own = row_kv == lane_kv
    qp = jnp.where(own, q4, 0.0)

    kk_buf[:, 0:WINDOW, :] = ck_ref[...]
    kk_buf[:, WINDOW:N_KEYS, :] = kn_ref[...]
    kk_buf[:, N_KEYS:N_KEYS_PAD, :] = jnp.zeros((SEQ_CHUNK, N_KEYS_PAD - N_KEYS, KV_DIM), _f32)
    vv_buf[:, 0:WINDOW, :] = cv_ref[...]
    vv_buf[:, WINDOW:N_KEYS, :] = vn_ref[...]
    vv_buf[:, N_KEYS:N_KEYS_PAD, :] = jnp.zeros((SEQ_CHUNK, N_KEYS_PAD - N_KEYS, KV_DIM), _f32)
    nk_ref[...] = kk_buf[:, DEC_SEQ:N_KEYS, :]
    nv_ref[...] = vv_buf[:, DEC_SEQ:N_KEYS, :]

    kk = kk_buf[...]
    vv = vv_buf[...]
    s = jnp.einsum('bqd,bkd->bqk', qp, kk, preferred_element_type=_f32)

    row = lax.broadcasted_iota(jnp.int32, (N_QROWS, N_KEYS_PAD), 0)
    col = lax.broadcasted_iota(jnp.int32, (N_QROWS, N_KEYS_PAD), 1)
    step = row % DEC_SEQ
    head = (row // DEC_SEQ + 1).astype(_f32)
    dist = step + WINDOW - col
    valid = (dist >= 0) & (dist <= WINDOW) & (col < N_KEYS)
    slope = jnp.exp2(head * (-8.0 / N_HEADS))
    bias = jnp.where(valid, -slope * dist.astype(_f32) * LOG2E, NEG_INF)
    s = jnp.where(valid[None], s + bias[None], NEG_INF)
    sink = sink_ref[...][None] * LOG2E
    m = jnp.maximum(jnp.max(s, axis=-1, keepdims=True), sink)
    e = jnp.exp2(s - m)
    denom = jnp.sum(e, axis=-1, keepdims=True) + jnp.exp2(sink - m)
    o = jnp.einsum('bqk,bkd->bqd', e, vv, preferred_element_type=_f32) / denom
    o = jnp.where(own, o, 0.0)
    acc = o
    for g in range(1, N_KV_HEADS):
        acc = acc + pltpu.roll(o, g * HEAD_DIM, axis=2)
    o_ref[...] = acc[:, :, 0:HEAD_DIM]


def _attn_sample(q_s, k_new, v_new, cache_k, cache_v, sink_rows):
    c = SEQ_CHUNK
    return pl.pallas_call(
        _attn_sample_kernel,
        grid=(DEC_BATCH // c,),
        in_specs=[
            pl.BlockSpec((c, N_QROWS, KV_DIM), lambda i: (i, 0, 0)),
            pl.BlockSpec((c, DEC_SEQ, KV_DIM), lambda i: (i, 0, 0)),
            pl.BlockSpec((c, DEC_SEQ, KV_DIM), lambda i: (i, 0, 0)),
            pl.BlockSpec((c, WINDOW, KV_DIM), lambda i: (i, 0, 0)),
            pl.BlockSpec((c, WINDOW, KV_DIM), lambda i: (i, 0, 0)),
            pl.BlockSpec((N_QROWS, 1), lambda i: (0, 0)),
        ],
        out_specs=[
            pl.BlockSpec((c, N_QROWS, HEAD_DIM), lambda i: (i, 0, 0)),
            pl.BlockSpec((c, WINDOW, KV_DIM), lambda i: (i, 0, 0)),
            pl.BlockSpec((c, WINDOW, KV_DIM), lambda i: (i, 0, 0)),
        ],
        out_shape=[
            jax.ShapeDtypeStruct((DEC_BATCH, N_QROWS, HEAD_DIM), _f32),
            jax.ShapeDtypeStruct((DEC_BATCH, WINDOW, KV_DIM), _f32),
            jax.ShapeDtypeStruct((DEC_BATCH, WINDOW, KV_DIM), _f32),
        ],
        scratch_shapes=[
            pltpu.VMEM((c, N_KEYS_PAD, KV_DIM), _f32),
            pltpu.VMEM((c, N_KEYS_PAD, KV_DIM), _f32),
        ],
        compiler_params=_params(("arbitrary",)),
        name="attn_sample",
    )(q_s, k_new, v_new, cache_k, cache_v, sink_rows)


def _gate_kernel(*refs):
    h_ref, cb_ref, atm_ref, ats_ref = refs[:4]
    w_refs = refs[4:4 + 4 * GATE_PIECES]
    tm_ref, ts_ref, at_all = refs[4 + 4 * GATE_PIECES:]
    at_all[0:TM, :] = atm_ref[...]
    at_all[TM:, :] = ats_ref[...]
    h = h_ref[...]
    cb = cb_ref[...]
    at = at_all[...]
    for c in range(GATE_PIECES):
        wgc_ref, wga_ref, wc_ref, wa_ref = w_refs[4 * c:4 * c + 4]
        cols = slice(c * GATE_PIECE, (c + 1) * GATE_PIECE)
        gc = jnp.dot(h, wgc_ref[...], preferred_element_type=_f32)
        ga = jnp.dot(h, wga_ref[...], preferred_element_type=_f32)
        yc = jnp.dot(cb, wc_ref[...], preferred_element_type=_f32)
        ya = jnp.dot(at, wa_ref[...], preferred_element_type=_f32)
        t = (jax.nn.sigmoid(gc) * yc + jax.nn.sigmoid(ga) * ya).astype(_bf16)
        tm_ref[:, cols] = t[0:TM, :]
        ts_ref[:, cols] = t[TM:, :]


def _gate(h_all, cb_all, atm, ats, w_in_b, wc, wa):
    row = lambda n, i: (i, 0)
    once = pl.Buffered(1)
    w_specs, w_args = [], []
    for c in range(GATE_PIECES):
        piece = lambda n, i, c=c, base=0: (0, base + n * GATE_PIECES + c)
        w_specs += [
            pl.BlockSpec((D_MODEL, GATE_PIECE),
                         functools.partial(piece, base=OFF_GC // GATE_PIECE), pipeline_mode=once),
            pl.BlockSpec((D_MODEL, GATE_PIECE),
                         functools.partial(piece, base=OFF_GA // GATE_PIECE), pipeline_mode=once),
            pl.BlockSpec((D_CONV, GATE_PIECE), piece, pipeline_mode=once),
            pl.BlockSpec((Q_DIM, GATE_PIECE), piece, pipeline_mode=once),
        ]
        w_args += [w_in_b, w_in_b, wc, wa]
    return pl.pallas_call(
        _gate_kernel,
        grid=(N_OUT_CHUNKS, N_TILES),
        in_specs=[
            pl.BlockSpec((TILE, D_MODEL), row),
            pl.BlockSpec((TILE, D_CONV), row),
            pl.BlockSpec((TM, Q_DIM), row),
            pl.BlockSpec((TS, Q_DIM), row),
        ] + w_specs,
        out_specs=[
            pl.BlockSpec((TM, TN_OUT), lambda n, i: (i, n)),
            pl.BlockSpec((TS, TN_OUT), lambda n, i: (i, n)),
        ],
        out_shape=[
            jax.ShapeDtypeStruct((N_MAIN, D_MODEL), _bf16),
            jax.ShapeDtypeStruct((N_SMALL, D_MODEL), _bf16),
        ],
        scratch_shapes=[pltpu.VMEM((TILE, Q_DIM), _bf16)],
        compiler_params=_params(("arbitrary", "arbitrary")),
        name="gate",
    )(h_all, cb_all, atm, ats, *w_args)


def _sample_rows(a):
    return a.transpose(1, 0, 2).reshape(N_SAMPLE, a.shape[-1])


def _to_small(sample_rows, dtype):
    w = sample_rows.shape[-1]
    return jnp.concatenate([
        jnp.zeros((SAMPLE_ROW0, w), dtype), sample_rows.astype(dtype),
        jnp.zeros((N_SMALL - SAMPLE_ROW0 - N_SAMPLE, w), dtype)], axis=0)


def kernel(x_prompt, x_sample, state_conv, cache_k_win, cache_v_win, meta_tokens, ffn1_norm, ffn1_w_up, ffn1_w_down, mix_norm, w_in, q_norm, k_norm, conv_w, w_conv_out, attn_sinks, w_attn_out, w_o, ffn2_norm, ffn2_w_up, ffn2_w_down):
    l = 0
    xm = x_prompt.reshape(N_MAIN, D_MODEL)
    xs = jnp.concatenate([
        meta_tokens.astype(_f32), _sample_rows(x_sample),
        jnp.zeros((N_SMALL - N_META - N_SAMPLE, D_MODEL), _f32)], axis=0)

    g1 = ffn1_norm[l].reshape(1, D_MODEL)
    g_mix = mix_norm[l].reshape(1, D_MODEL)
    *first_tile, wgu1, wd1 = _ffn_first(xm, xs, g1, g_mix, ffn1_w_up[l], ffn1_w_down[l])
    jobs = (_w_up_job(16), _w_down_job(32), _cast_job(D_MODEL, D_IN, 16),
            _cast_job(D_CONV, D_MODEL, 16), _cast_job(Q_DIM, D_MODEL, 16),
            _cast_job(D_MODEL, D_MODEL, 16))
    x1m, x1s, h_all, wgu2, wd2, w_in_b, wc_b, wa_b, wo_b = _ffn_main(
        xm, xs, g1, wgu1, wd1, g_mix, first_tile, jobs,
        (ffn2_w_up[l], ffn2_w_down[l], w_in[l], w_conv_out[l], w_attn_out[l], w_o[l]))

    state_rows = state_conv[l].transpose(1, 0, 2).reshape((CONV_WIDTH - 1) * DEC_BATCH, D_CONV)
    cb_all, qm, km, vm, qkv_s, new_conv_p, new_conv_s = _inproj(
        h_all, w_in_b, conv_w[l], state_rows, q_norm[l], k_norm[l])

    atm, new_k_p, new_v_p = _attn_prompt(qm, km, vm, qkv_s, attn_sinks[l])
    samp = qkv_s[SAMPLE_ROW0:SAMPLE_ROW0 + N_SAMPLE]
    q_s = (samp[:, 0:Q_DIM].reshape(DEC_SEQ, DEC_BATCH, N_HEADS, HEAD_DIM)
           .transpose(1, 2, 0, 3).reshape(DEC_BATCH, N_QROWS, HEAD_DIM))
    q_s = jnp.tile(q_s, (1, 1, N_KV_HEADS))
    k_new = samp[:, Q_DIM:Q_DIM + KV_DIM].reshape(DEC_SEQ, DEC_BATCH, KV_DIM).transpose(1, 0, 2)
    v_new = samp[:, Q_DIM + KV_DIM:N_QKV].reshape(DEC_SEQ, DEC_BATCH, KV_DIM).transpose(1, 0, 2)
    sink_rows = jnp.repeat(attn_sinks[l].astype(_f32), DEC_SEQ).reshape(N_QROWS, 1)
    o_s, new_k_s, new_v_s = _attn_sample(
        q_s, k_new, v_new,
        cache_k_win[l].reshape(DEC_BATCH, WINDOW, KV_DIM),
        cache_v_win[l].reshape(DEC_BATCH, WINDOW, KV_DIM), sink_rows)
    at_rows = (o_s.reshape(DEC_BATCH, N_HEADS, DEC_SEQ, HEAD_DIM)
               .transpose(2, 0, 1, 3).reshape(N_SAMPLE, Q_DIM))
    ats = _to_small(at_rows, _bf16)

    t_main, t_small = _gate(h_all, cb_all, atm, ats, w_in_b, wc_b, wa_b)
    ym, ys = _ffn_stream(x1m, x1s, ffn2_norm[l].reshape(1, D_MODEL), wgu2, wd2,
                         proj=(t_main, t_small, wo_b))

    y_prompt = ym.reshape(BATCH, SEQ, D_MODEL)
    y_sample = (ys[SAMPLE_ROW0:SAMPLE_ROW0 + N_SAMPLE]
                .reshape(DEC_SEQ, DEC_BATCH, D_MODEL).transpose(1, 0, 2))
    kv_shape_p = (1, BATCH, WINDOW, N_KV_HEADS, HEAD_DIM)
    kv_shape_s = (1, DEC_BATCH, WINDOW, N_KV_HEADS, HEAD_DIM)
    return (y_prompt, y_sample,
            new_conv_p[None],
            new_k_p.reshape(kv_shape_p), new_v_p.reshape(kv_shape_p),
            new_conv_s.reshape(CONV_WIDTH - 1, DEC_BATCH, D_CONV).transpose(1, 0, 2)[None],
            new_k_s.reshape(kv_shape_s), new_v_s.reshape(kv_shape_s))
```

```python
import functools
from typing import NamedTuple

import jax
import jax.numpy as jnp
from jax import lax
from jax.experimental import pallas as pl
from jax.experimental.pallas import tpu as pltpu

D_MODEL = 2048
BATCH = 4
SEQ = 2048
DEC_BATCH = 32
DEC_SEQ = 4
PAST_LEN = 16384
N_META = 16
D_CONV = D_MODEL // 2
CONV_WIDTH = 3
HEAD_DIM = 64
N_HEADS = (D_MODEL // 2) // HEAD_DIM
N_KV_HEADS = N_HEADS // 4
GROUP = N_HEADS // N_KV_HEADS
Q_DIM = N_HEADS * HEAD_DIM
KV_DIM = N_KV_HEADS * HEAD_DIM
PAIR = 2 * HEAD_DIM
WINDOW = 128
D_FF = ((8 * D_MODEL // 3 + 127) // 128) * 128
D_IN = 3 * D_CONV + Q_DIM + 2 * KV_DIM + 2 * D_MODEL
EPS = 1e-6
NEG_INF = -1e30
LOG2E = 1.4426950408889634
Q_SCALE = HEAD_DIM ** -0.5 * LOG2E

OFF_XC = 0
OFF_BG = D_CONV
OFF_CG = 2 * D_CONV
OFF_Q = 3 * D_CONV
OFF_K = OFF_Q + Q_DIM
OFF_V = OFF_K + KV_DIM
OFF_GC = OFF_V + KV_DIM
OFF_GA = OFF_GC + D_MODEL
N_QKV = Q_DIM + 2 * KV_DIM

N_MAIN = BATCH * SEQ
N_SAMPLE = DEC_BATCH * DEC_SEQ
N_SMALL = 256
SAMPLE_ROW0 = N_META

N_TILES = 16
TM = N_MAIN // N_TILES
TS = N_SMALL // N_TILES
TILE = TM + TS
N_ALL = N_TILES * TILE
TILES_PER_SEQ = SEQ // TM

TF = 512
D_FF_PAD = ((D_FF + TF - 1) // TF) * TF
N_FF_CHUNKS = D_FF_PAD // TF
FFN_STEPS = N_TILES * N_FF_CHUNKS
TN_OUT = 1024
N_OUT_CHUNKS = D_MODEL // TN_OUT
GATE_PIECE = 512
GATE_PIECES = TN_OUT // GATE_PIECE

QBLK = 128
KPAD = QBLK - N_META
N_QBLK = SEQ // QBLK
KBUF_ROWS = QBLK + SEQ

VMEM_LIMIT = 56 * 1024 * 1024

_bf16 = jnp.bfloat16
_f32 = jnp.float32


def _params(sem):
    return pltpu.CompilerParams(dimension_semantics=sem, vmem_limit_bytes=VMEM_LIMIT)


def _rms_rows(x, g):
    ms = jnp.mean(x * x, axis=-1, keepdims=True)
    return x * lax.rsqrt(ms + EPS) * g


def _pair_rms(x, g_pair):
    low = lax.broadcasted_iota(jnp.int32, x.shape, 1) < HEAD_DIM
    x2 = x * x
    s_lo = jnp.sum(jnp.where(low, x2, 0.0), axis=-1, keepdims=True)
    s_hi = jnp.sum(jnp.where(low, 0.0, x2), axis=-1, keepdims=True)
    ms = jnp.where(low, s_lo, s_hi) * (1.0 / HEAD_DIM)
    return x * lax.rsqrt(ms + EPS) * g_pair


class _CastJob(NamedTuple):
    slab: int
    rows_in: int
    n_in_slabs: int
    n_out_slabs: int
    n_in_cols: int
    n_out_cols: int
    segments: tuple
    zero_ranges: tuple


def _cast_job(rows_in, cols_in, slab, segments=None, zero_ranges=(), rows_out=None, cols_out=None):
    rows_out = rows_in if rows_out is None else rows_out
    cols_out = cols_in if cols_out is None else cols_out
    segments = ((0, cols_in, 0),) if segments is None else segments
    assert rows_out % slab == 0
    return _CastJob(slab, rows_in, -(-rows_in // slab), rows_out // slab, cols_in, cols_out,
                    tuple(segments), tuple(zero_ranges))


def _job_specs(job, step_of):
    in_idx = lambda *ids: (jnp.minimum(step_of(*ids), job.n_in_slabs - 1), 0)
    out_idx = lambda *ids: (jnp.minimum(step_of(*ids), job.n_out_slabs - 1), 0)
    return (pl.BlockSpec((job.slab, job.n_in_cols), in_idx),
            pl.BlockSpec((job.slab, job.n_out_cols), out_idx))


def _run_cast_job(job, step, x_ref, o_ref):
    out_slab = jnp.minimum(step, job.n_out_slabs - 1)
    padded = job.n_out_slabs * job.slab > job.rows_in
    for src, width, dst in job.segments:
        x = x_ref[:, src:src + width]
        if padded:
            row = out_slab * job.slab + lax.broadcasted_iota(jnp.int32, x.shape, 0)
            x = jnp.where(row < job.rows_in, x, 0.0)
        o_ref[:, dst:dst + width] = x.astype(_bf16)
    for lo, hi in job.zero_ranges:
        o_ref[:, lo:hi] = jnp.zeros((job.slab, hi - lo), _bf16)


def _w_up_job(slab):
    segments, zero_ranges = [], []
    for c in range(N_FF_CHUNKS):
        width = min(TF, D_FF - c * TF)
        for half, src0 in enumerate((0, D_FF)):
            dst = (2 * c + half) * TF
            segments.append((src0 + c * TF, width, dst))
            if width < TF:
                zero_ranges.append((dst + width, dst + TF))
    return _cast_job(D_MODEL, 2 * D_FF, slab, segments=segments, zero_ranges=zero_ranges,
                     cols_out=2 * D_FF_PAD)


def _w_down_job(slab):
    return _cast_job(D_FF, D_MODEL, slab, rows_out=D_FF_PAD)


FC = 256
N_FC = D_FF_PAD // FC
assert TF % FC == 0
NF = 2


def _swiglu_chunk(h, w_gate_up, w_down, n):
    gu = jnp.dot(h, w_gate_up, preferred_element_type=_f32)
    gate, up = gu[:, 0:n], gu[:, n:2 * n]
    a = (gate * jax.nn.sigmoid(gate) * up * 0.5).astype(_bf16)
    return jnp.dot(a, w_down, preferred_element_type=_f32)


def _ffn_first_kernel(xm_ref, xs_ref, g_ref, gn_ref, wup_hbm, wdn_hbm,
                      om_ref, os_ref, hn_ref, wgu_out, wd_out,
                      h_ref, stage_g, stage_u, stage_d, cast_gu, cast_d, sem_in, sem_out):
    def width(c):
        return min(FC, D_FF - c * FC)

    def in_copies(c):
        s, w = c % 2, width(c)
        return (
            pltpu.make_async_copy(wup_hbm.at[:, pl.ds(c * FC, w)],
                                  stage_g.at[s, :, pl.ds(0, w)], sem_in.at[0, s]),
            pltpu.make_async_copy(wup_hbm.at[:, pl.ds(D_FF + c * FC, w)],
                                  stage_u.at[s, :, pl.ds(0, w)], sem_in.at[1, s]),
            pltpu.make_async_copy(wdn_hbm.at[pl.ds(c * FC, w), :],
                                  stage_d.at[s, pl.ds(0, w), :], sem_in.at[2, s]),
        )

    def out_copies(c):
        s = c % 2
        col = (c * FC // TF) * 2 * TF + (c * FC) % TF
        return (
            pltpu.make_async_copy(cast_gu.at[s, :, pl.ds(0, FC)],
                                  wgu_out.at[:, pl.ds(col, FC)], sem_out.at[0, s]),
            pltpu.make_async_copy(cast_gu.at[s, :, pl.ds(FC, FC)],
                                  wgu_out.at[:, pl.ds(col + TF, FC)], sem_out.at[1, s]),
            pltpu.make_async_copy(cast_d.at[s], wd_out.at[pl.ds(c * FC, FC), :], sem_out.at[2, s]),
        )

    def start(copies):
        for cp in copies:
            cp.start()

    def wait(copies):
        for cp in copies:
            cp.wait()

    start(in_copies(0))
    g = g_ref[...]
    om_ref[...] = xm_ref[...]
    os_ref[...] = xs_ref[...]
    for t in range(NF):
        h_ref[t * TILE:t * TILE + TM, :] = _rms_rows(
            xm_ref[t * TM:(t + 1) * TM, :], g).astype(_bf16)
        h_ref[t * TILE + TM:(t + 1) * TILE, :] = _rms_rows(
            xs_ref[t * TS:(t + 1) * TS, :], g).astype(_bf16)

    for c in range(N_FC):
        s, w = c % 2, width(c)
        if c + 1 < N_FC:
            start(in_copies(c + 1))
        wait(in_copies(c))
        if c >= 2:
            wait(out_copies(c - 2))
        if w == FC:
            cast_gu[s, :, 0:FC] = stage_g[s].astype(_bf16)
            cast_gu[s, :, FC:2 * FC] = stage_u[s].astype(_bf16)
            cast_d[s] = stage_d[s].astype(_bf16)
        else:
            lane = lax.broadcasted_iota(jnp.int32, (D_MODEL, FC), 1)
            sub = lax.broadcasted_iota(jnp.int32, (FC, D_MODEL), 0)
            cast_gu[s, :, 0:FC] = jnp.where(lane < w, stage_g[s], 0.0).astype(_bf16)
            cast_gu[s, :, FC:2 * FC] = jnp.where(lane < w, stage_u[s], 0.0).astype(_bf16)
            cast_d[s] = jnp.where(sub < w, stage_d[s], 0.0).astype(_bf16)
        start(out_copies(c))
        for t in range(NF):
            r = _swiglu_chunk(h_ref[t * TILE:(t + 1) * TILE, :], cast_gu[s], cast_d[s], FC)
            om_ref[t * TM:(t + 1) * TM, :] += r[0:TM, :]
            os_ref[t * TS:(t + 1) * TS, :] += r[TM:, :]
    wait(out_copies(N_FC - 2))
    wait(out_copies(N_FC - 1))

    gn = gn_ref[...]
    for t in range(NF):
        hn_ref[t * TILE:t * TILE + TM, :] = _rms_rows(
            om_ref[t * TM:(t + 1) * TM, :], gn).astype(_bf16)
        hn_ref[t * TILE + TM:(t + 1) * TILE, :] = _rms_rows(
            os_ref[t * TS:(t + 1) * TS, :], gn).astype(_bf16)


def _ffn_first(xm, xs, g, next_gain, w_up, w_down):
    first = lambda i: (0, 0)
    once = pl.Buffered(1)
    return pl.pallas_call(
        _ffn_first_kernel,
        grid=(1,),
        in_specs=[
            pl.BlockSpec((NF * TM, D_MODEL), first, pipeline_mode=once),
            pl.BlockSpec((NF * TS, D_MODEL), first, pipeline_mode=once),
            pl.BlockSpec((1, D_MODEL), first),
            pl.BlockSpec((1, D_MODEL), first),
            pl.BlockSpec(memory_space=pl.ANY),
            pl.BlockSpec(memory_space=pl.ANY),
        ],
        out_specs=[
            pl.BlockSpec((NF * TM, D_MODEL), first, pipeline_mode=once),
            pl.BlockSpec((NF * TS, D_MODEL), first, pipeline_mode=once),
            pl.BlockSpec((NF * TILE, D_MODEL), first, pipeline_mode=once),
            pl.BlockSpec(memory_space=pl.ANY),
            pl.BlockSpec(memory_space=pl.ANY),
        ],
        out_shape=[
            jax.ShapeDtypeStruct((NF * TM, D_MODEL), _f32),
            jax.ShapeDtypeStruct((NF * TS, D_MODEL), _f32),
            jax.ShapeDtypeStruct((NF * TILE, D_MODEL), _bf16),
            jax.ShapeDtypeStruct((D_MODEL, 2 * D_FF_PAD), _bf16),
            jax.ShapeDtypeStruct((D_FF_PAD, D_MODEL), _bf16),
        ],
        scratch_shapes=[
            pltpu.VMEM((NF * TILE, D_MODEL), _bf16),
            pltpu.VMEM((2, D_MODEL, FC), _f32),
            pltpu.VMEM((2, D_MODEL, FC), _f32),
            pltpu.VMEM((2, FC, D_MODEL), _f32),
            pltpu.VMEM((2, D_MODEL, 2 * FC), _bf16),
            pltpu.VMEM((2, FC, D_MODEL), _bf16),
            pltpu.SemaphoreType.DMA((3, 2)),
            pltpu.SemaphoreType.DMA((3, 2)),
        ],
        compiler_params=_params(("arbitrary",)),
        name="ffn_first",
    )(xm, xs, g, next_gain, w_up, w_down)


def _ffn_main_kernel(jobs, *refs):
    refs = list(refs)
    take = lambda n: [refs.pop(0) for _ in range(n)]
    xm_ref, xs_ref, g_ref, wgu_ref, wd_ref, gn_ref, om0_hbm, os0_hbm, hn0_hbm = take(9)
    job_in = take(len(jobs))
    om_ref, os_ref, hn_ref = take(3)
    job_out = take(len(jobs))
    h_ref, sem = refs
    i = pl.program_id(0)
    j = pl.program_id(1)

    def run_jobs():
        for job, x_ref, o_ref in zip(jobs, job_in, job_out):
            _run_cast_job(job, i * N_FF_CHUNKS + j, x_ref, o_ref)

    @pl.when(i < NF)
    def _():
        @pl.when(j == 0)
        def _():
            rows = lambda n: pl.ds(pl.multiple_of(i * n, n), n)
            copies = (pltpu.make_async_copy(om0_hbm.at[rows(TM)], om_ref, sem.at[0]),
                      pltpu.make_async_copy(os0_hbm.at[rows(TS)], os_ref, sem.at[1]),
                      pltpu.make_async_copy(hn0_hbm.at[rows(TILE)], hn_ref, sem.at[2]))
            for cp in copies:
                cp.start()
            for cp in copies:
                cp.wait()
        run_jobs()

    @pl.when(i >= NF)
    def _():
        @pl.when(j == 0)
        def _():
            g = g_ref[...]
            xm = xm_ref[...]
            xs = xs_ref[...]
            h_ref[0:TM, :] = _rms_rows(xm, g).astype(_bf16)
            om_ref[...] = xm
            h_ref[TM:, :] = _rms_rows(xs, g).astype(_bf16)
            os_ref[...] = xs

        run_jobs()
        r = _swiglu_chunk(h_ref[...], wgu_ref[...], wd_ref[...], TF)
        om_ref[...] += r[0:TM, :]
        os_ref[...] += r[TM:, :]

        @pl.when(j == N_FF_CHUNKS - 1)
        def _():
            gn = gn_ref[...]
            hn_ref[0:TM, :] = _rms_rows(om_ref[...], gn).astype(_bf16)
            hn_ref[TM:, :] = _rms_rows(os_ref[...], gn).astype(_bf16)


def _ffn_main(xm, xs, g, wgu, wd, next_gain, first_tile, jobs, job_inputs):
    assert all(job.n_out_slabs <= FFN_STEPS for job in jobs)
    job_specs = [_job_specs(job, lambda i, j: i * N_FF_CHUNKS + j) for job in jobs]
    row = lambda i, j: (i, 0)
    fixed = lambda i, j: (0, 0)
    chunk = lambda i, j: jnp.where(i < NF, 0, j)
    return pl.pallas_call(
        functools.partial(_ffn_main_kernel, tuple(jobs)),
        grid=(N_TILES, N_FF_CHUNKS),
        in_specs=[
            pl.BlockSpec((TM, D_MODEL), row),
            pl.BlockSpec((TS, D_MODEL), row),
            pl.BlockSpec((1, D_MODEL), fixed),
            pl.BlockSpec((D_MODEL, 2 * TF), lambda i, j: (0, chunk(i, j))),
            pl.BlockSpec((TF, D_MODEL), lambda i, j: (chunk(i, j), 0)),
            pl.BlockSpec((1, D_MODEL), fixed),
        ] + [pl.BlockSpec(memory_space=pl.ANY) for _ in first_tile] + [s[0] for s in job_specs],
        out_specs=[
            pl.BlockSpec((TM, D_MODEL), row),
            pl.BlockSpec((TS, D_MODEL), row),
            pl.BlockSpec((TILE, D_MODEL), row),
        ] + [s[1] for s in job_specs],
        out_shape=[
            jax.ShapeDtypeStruct((N_MAIN, D_MODEL), _f32),
            jax.ShapeDtypeStruct((N_SMALL, D_MODEL), _f32),
            jax.ShapeDtypeStruct((N_ALL, D_MODEL), _bf16),
        ] + [jax.ShapeDtypeStruct((job.n_out_slabs * job.slab, job.n_out_cols), _bf16)
             for job in jobs],
        scratch_shapes=[pltpu.VMEM((TILE, D_MODEL), _bf16), pltpu.SemaphoreType.DMA((3,))],
        compiler_params=_params(("arbitrary", "arbitrary")),
        name="ffn",
    )(xm, xs, g, wgu, wd, next_gain, *first_tile, *job_inputs)


def _ffn_stream_kernel(xm_ref, xs_ref, tm_ref, ts_ref, wo_ref, g_ref, wgu_hbm, wd_hbm,
                       om_ref, os_ref, h_ref, t_all, wgu_buf, wd_buf, sem):
    i = pl.program_id(0)

    def chunk_copies(chunk, slot):
        return (
            pltpu.make_async_copy(wgu_hbm.at[:, pl.ds(chunk * 2 * TF, 2 * TF)],
                                  wgu_buf.at[slot], sem.at[0, slot]),
            pltpu.make_async_copy(wd_hbm.at[pl.ds(chunk * TF, TF), :],
                                  wd_buf.at[slot], sem.at[1, slot]),
        )

    def start(chunk, slot):
        for c in chunk_copies(chunk, slot):
            c.start()

    @pl.when(i == 0)
    def _():
        start(0, 0)

    g = g_ref[...]
    t_all[0:TM, :] = tm_ref[...]
    t_all[TM:, :] = ts_ref[...]
    p = jnp.dot(t_all[...], wo_ref[...], preferred_element_type=_f32)
    xm = xm_ref[...] + p[0:TM, :]
    xs = xs_ref[...] + p[TM:, :]
    h_ref[0:TM, :] = _rms_rows(xm, g).astype(_bf16)
    om_ref[...] = xm
    h_ref[TM:, :] = _rms_rows(xs, g).astype(_bf16)
    os_ref[...] = xs

    h = h_ref[...]
    for j in range(N_FF_CHUNKS):
        slot = (i * N_FF_CHUNKS + j) % 2
        if j + 1 < N_FF_CHUNKS:
            start(j + 1, 1 - slot)
        else:
            @pl.when(i + 1 < N_TILES)
            def _():
                start(0, 1 - slot)
        for c in chunk_copies(j, slot):
            c.wait()
        r = _swiglu_chunk(h, wgu_buf[slot], wd_buf[slot], TF)
        om_ref[...] += r[0:TM, :]
        os_ref[...] += r[TM:, :]


def _ffn_stream(xm, xs, g, wgu, wd, proj):
    row = lambda i: (i, 0)
    fixed = lambda i: (0, 0)
    return pl.pallas_call(
        _ffn_stream_kernel,
        grid=(N_TILES,),
        in_specs=[
            pl.BlockSpec((TM, D_MODEL), row),
            pl.BlockSpec((TS, D_MODEL), row),
            pl.BlockSpec((TM, D_MODEL), row),
            pl.BlockSpec((TS, D_MODEL), row),
            pl.BlockSpec((D_MODEL, D_MODEL), fixed, pipeline_mode=pl.Buffered(1)),
            pl.BlockSpec((1, D_MODEL), fixed),
            pl.BlockSpec(memory_space=pl.ANY),
            pl.BlockSpec(memory_space=pl.ANY),
        ],
        out_specs=[pl.BlockSpec((TM, D_MODEL), row), pl.BlockSpec((TS, D_MODEL), row)],
        out_shape=[jax.ShapeDtypeStruct((N_MAIN, D_MODEL), _f32),
                   jax.ShapeDtypeStruct((N_SMALL, D_MODEL), _f32)],
        scratch_shapes=[
            pltpu.VMEM((TILE, D_MODEL), _bf16),
            pltpu.VMEM((TILE, D_MODEL), _bf16),
            pltpu.VMEM((2, D_MODEL, 2 * TF), _bf16),
            pltpu.VMEM((2, TF, D_MODEL), _bf16),
            pltpu.SemaphoreType.DMA((2, 2)),
        ],
        compiler_params=_params(("arbitrary",)),
        name="ffn_stream",
    )(xm, xs, *proj, g, wgu, wd)


N_BRANCH_IN = OFF_GC
CONV_HALO = 8
U_BASE = 112
U_STATE0 = U_BASE + SAMPLE_ROW0 - (CONV_WIDTH - 1) * DEC_BATCH
U_ROWS = U_BASE + N_SMALL
U_LAST0 = U_BASE + SAMPLE_ROW0 + (DEC_SEQ - (CONV_WIDTH - 1)) * DEC_BATCH
assert U_BASE >= (CONV_WIDTH - 1) * DEC_BATCH and U_BASE % TS == 0 and TS == N_META


def _inproj_kernel(h_ref, w_ref, cw_ref, st_ref, qg_ref, kg_ref,
                   cb_ref, q_ref, k_ref, v_ref, sm_ref, ncp_ref, ncs_ref,
                   ubuf, meta_halo, utab):
    i = pl.program_id(0)

    @pl.when(i == 0)
    def _():
        utab[...] = jnp.zeros(utab.shape, _f32)
        utab[U_STATE0:U_STATE0 + (CONV_WIDTH - 1) * DEC_BATCH, :] = st_ref[...]
        meta_halo[...] = jnp.zeros(meta_halo.shape, _f32)
        ubuf[0:CONV_HALO, :] = jnp.zeros((CONV_HALO, D_CONV), _f32)

    h = h_ref[...]
    col_dot = lambda off, width: jnp.dot(h, w_ref[:, off:off + width],
                                         preferred_element_type=_f32)
    z_q = col_dot(OFF_Q, Q_DIM)
    z_kv = col_dot(OFF_K, 2 * KV_DIM)
    z_xc = col_dot(OFF_XC, D_CONV)
    z_cg = col_dot(OFF_CG, D_CONV)
    z_bg = col_dot(OFF_BG, D_CONV)
    cw = cw_ref[...]
    w0, w1, w2 = cw[0:1, :], cw[1:2, :], cw[2:3, :]
    qg = qg_ref[...]
    kg = kg_ref[...]

    def qkv(rows, q_out, k_out, v_out):
        for t in range(Q_DIM // PAIR):
            q_out(t, _pair_rms(z_q[rows, t * PAIR:(t + 1) * PAIR], qg) * Q_SCALE)
        for t in range(KV_DIM // PAIR):
            k_out(t, _pair_rms(z_kv[rows, t * PAIR:(t + 1) * PAIR], kg))
            v_out(t, z_kv[rows, KV_DIM + t * PAIR:KV_DIM + (t + 1) * PAIR])

    u_s = z_cg[TM:, :] * z_xc[TM:, :]
    cur = pl.multiple_of(U_BASE + TS * i, TS)
    slot = pl.multiple_of(jnp.where(i == 0, U_ROWS, cur), TS)
    utab[pl.ds(slot, TS), :] = u_s
    y_s = (w0 * utab[pl.ds(cur - 2 * DEC_BATCH, TS), :]
           + w1 * utab[pl.ds(cur - DEC_BATCH, TS), :] + w2 * u_s)
    cb_ref[TM:, :] = (z_bg[TM:, :] * y_s).astype(_bf16)
    meta_halo[...] = jnp.where(i == 0, u_s[N_META - CONV_HALO:N_META, :], meta_halo[...])

    def sm_store(off):
        def store(t, val):
            sm_ref[:, off + t * PAIR:off + (t + 1) * PAIR] = val
        return store
    qkv(slice(TM, TILE), sm_store(0), sm_store(Q_DIM), sm_store(Q_DIM + KV_DIM))

    ncs_ref[...] = utab[U_LAST0:U_LAST0 + (CONV_WIDTH - 1) * DEC_BATCH, :]

    ubuf[0:CONV_HALO, :] = jnp.where(i % TILES_PER_SEQ == 0, meta_halo[...], ubuf[0:CONV_HALO, :])

    u = z_cg[0:TM, :] * z_xc[0:TM, :]
    ubuf[CONV_HALO:CONV_HALO + TM, :] = u
    u1 = ubuf[CONV_HALO - 1:CONV_HALO - 1 + TM, :]
    u2 = ubuf[CONV_HALO - 2:CONV_HALO - 2 + TM, :]
    cb_ref[0:TM, :] = (z_bg[0:TM, :] * (w0 * u2 + w1 * u1 + w2 * u)).astype(_bf16)
    tail = ubuf[TM:TM + CONV_HALO, :]
    ubuf[0:CONV_HALO, :] = tail
    ncp_ref[0] = tail[CONV_HALO - (CONV_WIDTH - 1):CONV_HALO, :]

    def lane_store(ref):
        def store(t, val):
            ref[:, t * PAIR:(t + 1) * PAIR] = val
        return store
    qkv(slice(0, TM), lane_store(q_ref), lane_store(k_ref), lane_store(v_ref))


def _inproj(h_all, w_in_b, conv_w, state_rows, q_gain, k_gain):
    qg = jnp.tile(q_gain.reshape(1, HEAD_DIM), (1, 2))
    kg = jnp.tile(k_gain.reshape(1, HEAD_DIM), (1, 2))
    row = lambda i: (i, 0)
    fixed = lambda i: (0, 0)
    return pl.pallas_call(
        _inproj_kernel,
        grid=(N_TILES,),
        in_specs=[
            pl.BlockSpec((TILE, D_MODEL), row),
            pl.BlockSpec((D_MODEL, N_BRANCH_IN), fixed, pipeline_mode=pl.Buffered(1)),
            pl.BlockSpec((CONV_WIDTH, D_CONV), fixed),
            pl.BlockSpec(((CONV_WIDTH - 1) * DEC_BATCH, D_CONV), fixed),
            pl.BlockSpec((1, PAIR), fixed),
            pl.BlockSpec((1, PAIR), fixed),
        ],
        out_specs=[
            pl.BlockSpec((TILE, D_CONV), row),
            pl.BlockSpec((TM, Q_DIM), row),
            pl.BlockSpec((TM, KV_DIM), row),
            pl.BlockSpec((TM, KV_DIM), row),
            pl.BlockSpec((TS, N_QKV), row),
            pl.BlockSpec((1, CONV_WIDTH - 1, D_CONV), lambda i: (i // TILES_PER_SEQ, 0, 0)),
            pl.BlockSpec(((CONV_WIDTH - 1) * DEC_BATCH, D_CONV), fixed),
        ],
        out_shape=[
            jax.ShapeDtypeStruct((N_ALL, D_CONV), _bf16),
            jax.ShapeDtypeStruct((N_MAIN, Q_DIM), _f32),
            jax.ShapeDtypeStruct((N_MAIN, KV_DIM), _f32),
            jax.ShapeDtypeStruct((N_MAIN, KV_DIM), _f32),
            jax.ShapeDtypeStruct((N_SMALL, N_QKV), _f32),
            jax.ShapeDtypeStruct((BATCH, CONV_WIDTH - 1, D_CONV), _f32),
            jax.ShapeDtypeStruct(((CONV_WIDTH - 1) * DEC_BATCH, D_CONV), _f32),
        ],
        scratch_shapes=[
            pltpu.VMEM((CONV_HALO + TM, D_CONV), _f32),
            pltpu.VMEM((CONV_HALO, D_CONV), _f32),
            pltpu.VMEM((U_ROWS + TS, D_CONV), _f32),
        ],
        compiler_params=_params(("arbitrary",)),
        name="inproj",
    )(h_all, w_in_b, conv_w, state_rows, qg, kg)


QPAIR = 2 * GROUP * HEAD_DIM
HEADS_PER_STEP = 2 * GROUP


def _attn_prompt_kernel(sink_ref, q_ref, k_ref, v_ref, km_ref, vm_ref,
                        o_ref, nk_ref, nv_ref, kt_buf, vbuf, bias_ref):
    gp = pl.program_id(1)

    for order in range(2):
        swap = (lambda x: x) if order == 0 else (lambda x: pltpu.roll(x, HEAD_DIM, axis=1))
        vbuf[order, 0:KPAD, :] = jnp.zeros((KPAD, PAIR), _bf16)
        vbuf[order, KPAD:QBLK, :] = swap(vm_ref[...]).astype(_bf16)
        vbuf[order, QBLK:KBUF_ROWS, :] = swap(v_ref[...]).astype(_bf16)
        first_keys = jnp.concatenate([jnp.zeros((KPAD, PAIR), _f32), km_ref[...]], axis=0)
        kt_buf[order, 0] = swap(first_keys).T.astype(_bf16)
        for b in range(N_QBLK):
            kt_buf[order, b + 1] = swap(k_ref[b * QBLK:(b + 1) * QBLK, :]).T.astype(_bf16)
    nk_ref[0] = k_ref[SEQ - WINDOW:SEQ, :]
    nv_ref[0] = v_ref[SEQ - WINDOW:SEQ, :]

    partner = lambda p: p ^ 1
    row = lax.broadcasted_iota(jnp.int32, (QBLK, 2 * QBLK), 0)
    col = lax.broadcasted_iota(jnp.int32, (QBLK, 2 * QBLK), 1)
    dist = QBLK + row - col
    in_window = (dist >= 0) & (dist <= WINDOW)
    distf = dist.astype(_f32)
    upper = lax.broadcasted_iota(jnp.int32, (2 * QBLK, 1), 0) < QBLK
    sink_cols = []
    for p in range(GROUP):
        sinks = []
        for g, hp in ((0, p), (1, partner(p))):
            head = gp * HEADS_PER_STEP + g * GROUP + hp
            slope = jnp.exp2(jnp.full((QBLK, 2 * QBLK), -8.0 / N_HEADS, _f32)
                             * (head + 1).astype(_f32))
            bias_ref[p, g * QBLK:(g + 1) * QBLK, :] = jnp.where(
                in_window, -slope * distf * LOG2E, NEG_INF)
            sinks.append(sink_ref[head] * LOG2E)
        sink_cols.append(jnp.where(upper, sinks[0], sinks[1]))

    lane = lax.broadcasted_iota(jnp.int32, (QBLK, PAIR), 1)
    low_half = lane < HEAD_DIM
    pad_col = lax.broadcasted_iota(jnp.int32, (2 * QBLK, 2 * QBLK), 1) < KPAD

    def block(i, first):
        r0 = i * QBLK if first else pl.multiple_of(i * QBLK, QBLK)
        qn = q_ref[pl.ds(r0 - QBLK, QBLK), :]
        outs = [[None] * GROUP for _ in range(2)]
        for p in range(GROUP):
            order = p % 2
            kt = jnp.concatenate([kt_buf[order, i - 1], kt_buf[order, i]], axis=1)
            vv = vbuf[order, pl.ds(r0 - QBLK, 2 * QBLK), :]
            tiles = []
            for g, hp in ((0, p), (1, partner(p))):
                hh = g * GROUP + hp
                in_low = hp % 2 == 0
                tile = qn[:, (hh // 2) * PAIR:(hh // 2 + 1) * PAIR]
                tiles.append(jnp.where(low_half if in_low else jnp.logical_not(low_half),
                                       tile, 0.0).astype(_bf16))
            qst = jnp.concatenate(tiles, axis=0)
            s = jnp.dot(qst, kt, preferred_element_type=_f32) + bias_ref[p]
            if first:
                s = jnp.where(pad_col, NEG_INF, s)
            sink = sink_cols[p]
            m = jnp.maximum(jnp.max(s, axis=-1, keepdims=True), sink)
            e = jnp.exp2(s - m)
            denom = jnp.sum(e, axis=-1, keepdims=True) + jnp.exp2(sink - m)
            o = jnp.dot(e.astype(_bf16), vv, preferred_element_type=_f32) / denom
            outs[0][p] = o[0:QBLK, :]
            outs[1][partner(p)] = o[QBLK:2 * QBLK, :]
        for g in range(2):
            for pr in range(GROUP // 2):
                t = g * (GROUP // 2) + pr
                o_ref[pl.ds(r0 - QBLK, QBLK), t * PAIR:(t + 1) * PAIR] = jnp.where(
                    low_half, outs[g][2 * pr], outs[g][2 * pr + 1]).astype(_bf16)

    block(1, True)

    def later_block(i, carry):
        block(i, False)
        return carry

    lax.fori_loop(2, N_QBLK + 1, later_block, 0)


def _attn_prompt(qm, km, vm, qkv_s, sinks):
    grid_spec = pltpu.PrefetchScalarGridSpec(
        num_scalar_prefetch=1,
        grid=(BATCH, N_KV_HEADS // 2),
        in_specs=[
            pl.BlockSpec((SEQ, QPAIR), lambda b, p, s: (b, p)),
            pl.BlockSpec((SEQ, PAIR), lambda b, p, s: (b, p)),
            pl.BlockSpec((SEQ, PAIR), lambda b, p, s: (b, p)),
            pl.BlockSpec((N_META, PAIR), lambda b, p, s: (0, Q_DIM // PAIR + p)),
            pl.BlockSpec((N_META, PAIR), lambda b, p, s: (0, (Q_DIM + KV_DIM) // PAIR + p)),
        ],
        out_specs=[
            pl.BlockSpec((SEQ, QPAIR), lambda b, p, s: (b, p)),
            pl.BlockSpec((1, WINDOW, PAIR), lambda b, p, s: (b, 0, p)),
            pl.BlockSpec((1, WINDOW, PAIR), lambda b, p, s: (b, 0, p)),
        ],
        scratch_shapes=[
            pltpu.VMEM((2, N_QBLK + 1, PAIR, QBLK), _bf16),
            pltpu.VMEM((2, KBUF_ROWS, PAIR), _bf16),
            pltpu.VMEM((GROUP, 2 * QBLK, 2 * QBLK), _f32),
        ],
    )
    return pl.pallas_call(
        _attn_prompt_kernel,
        grid_spec=grid_spec,
        out_shape=[
            jax.ShapeDtypeStruct((N_MAIN, Q_DIM), _bf16),
            jax.ShapeDtypeStruct((BATCH, WINDOW, KV_DIM), _f32),
            jax.ShapeDtypeStruct((BATCH, WINDOW, KV_DIM), _f32),
        ],
        compiler_params=_params(("arbitrary", "arbitrary")),
        name="attn_prompt",
    )(sinks, qm, km, vm, qkv_s, qkv_s)


SEQ_CHUNK = 8
N_QROWS = N_HEADS * DEC_SEQ
N_KEYS = WINDOW + DEC_SEQ
N_KEYS_PAD = ((N_KEYS + 7) // 8) * 8


def _attn_sample_kernel(q_ref, kn_ref, vn_ref, ck_ref, cv_ref, sink_ref,
                        o_ref, nk_ref, nv_ref, kk_buf, vv_buf):
    q4 = q_ref[...]
    shp = (SEQ_CHUNK, N_QROWS, KV_DIM)
    row_kv = lax.broadcasted_iota(jnp.int32, shp, 1) // (GROUP * DEC_SEQ)
    lane_kv = lax.broadcasted_iota(jnp.int32, shp, 2) // HEAD_DIM
    own = row_kv == lane_kv
    qp = jnp.where(own, q4, 0.0)

    kk_buf[:, 0:WINDOW, :] = ck_ref[...]
    kk_buf[:, WINDOW:N_KEYS, :] = kn_ref[...]
    kk_buf[:, N_KEYS:N_KEYS_PAD, :] = jnp.zeros((SEQ_CHUNK, N_KEYS_PAD - N_KEYS, KV_DIM), _f32)
    vv_buf[:, 0:WINDOW, :] = cv_ref[...]
    vv_buf[:, WINDOW:N_KEYS, :] = vn_ref[...]
    vv_buf[:, N_KEYS:N_KEYS_PAD, :] = jnp.zeros((SEQ_CHUNK, N_KEYS_PAD - N_KEYS, KV_DIM), _f32)
    nk_ref[...] = kk_buf[:, DEC_SEQ:N_KEYS, :]
    nv_ref[...] = vv_buf[:, DEC_SEQ:N_KEYS, :]

    kk = kk_buf[...]
    vv = vv_buf[...]
    s = jnp.einsum('bqd,bkd->bqk', qp, kk, preferred_element_type=_f32)

    row = lax.broadcasted_iota(jnp.int32, (N_QROWS, N_KEYS_PAD), 0)
    col = lax.broadcasted_iota(jnp.int32, (N_QROWS, N_KEYS_PAD), 1)
    step = row % DEC_SEQ
    head = (row // DEC_SEQ + 1).astype(_f32)
    dist = step + WINDOW - col
    valid = (dist >= 0) & (dist <= WINDOW) & (col < N_KEYS)
    slope = jnp.exp2(head * (-8.0 / N_HEADS))
    bias = jnp.where(valid, -slope * dist.astype(_f32) * LOG2E, NEG_INF)
    s = jnp.where(valid[None], s + bias[None], NEG_INF)
    sink = sink_ref[...][None] * LOG2E
    m = jnp.maximum(jnp.max(s, axis=-1, keepdims=True), sink)
    e = jnp.exp2(s - m)
    denom = jnp.sum(e, axis=-1, keepdims=True) + jnp.exp2(sink - m)
    o = jnp.einsum('bqk,bkd->bqd', e, vv, preferred_element_type=_f32) / denom
    o = jnp.where(own, o, 0.0)
    acc = o
    for g in range(1, N_KV_HEADS):
        acc = acc + pltpu.roll(o, g * HEAD_DIM, axis=2)
    o_ref[...] = acc[:, :, 0:HEAD_DIM]


def _attn_sample(q_s, k_new, v_new, cache_k, cache_v, sink_rows):
    c = SEQ_CHUNK
    return pl.pallas_call(
        _attn_sample_kernel,
        grid=(DEC_BATCH // c,),
        in_specs=[
            pl.BlockSpec((c, N_QROWS, KV_DIM), lambda i: (i, 0, 0)),
            pl.BlockSpec((c, DEC_SEQ, KV_DIM), lambda i: (i, 0, 0)),
            pl.BlockSpec((c, DEC_SEQ, KV_DIM), lambda i: (i, 0, 0)),
            pl.BlockSpec((c, WINDOW, KV_DIM), lambda i: (i, 0, 0)),
            pl.BlockSpec((c, WINDOW, KV_DIM), lambda i: (i, 0, 0)),
            pl.BlockSpec((N_QROWS, 1), lambda i: (0, 0)),
        ],
        out_specs=[
            pl.BlockSpec((c, N_QROWS, HEAD_DIM), lambda i: (i, 0, 0)),
            pl.BlockSpec((c, WINDOW, KV_DIM), lambda i: (i, 0, 0)),
            pl.BlockSpec((c, WINDOW, KV_DIM), lambda i: (i, 0, 0)),
        ],
        out_shape=[
            jax.ShapeDtypeStruct((DEC_BATCH, N_QROWS, HEAD_DIM), _f32),
            jax.ShapeDtypeStruct((DEC_BATCH, WINDOW, KV_DIM), _f32),
            jax.ShapeDtypeStruct((DEC_BATCH, WINDOW, KV_DIM), _f32),
        ],
        scratch_shapes=[
            pltpu.VMEM((c, N_KEYS_PAD, KV_DIM), _f32),
            pltpu.VMEM((c, N_KEYS_PAD, KV_DIM), _f32),
        ],
        compiler_params=_params(("arbitrary",)),
        name="attn_sample",
    )(q_s, k_new, v_new, cache_k, cache_v, sink_rows)


def _gate_kernel(*refs):
    h_ref, cb_ref, atm_ref, ats_ref = refs[:4]
    w_refs = refs[4:4 + 4 * GATE_PIECES]
    tm_ref, ts_ref, at_all = refs[4 + 4 * GATE_PIECES:]
    at_all[0:TM, :] = atm_ref[...]
    at_all[TM:, :] = ats_ref[...]
    h = h_ref[...]
    cb = cb_ref[...]
    at = at_all[...]
    for c in range(GATE_PIECES):
        wgc_ref, wga_ref, wc_ref, wa_ref = w_refs[4 * c:4 * c + 4]
        cols = slice(c * GATE_PIECE, (c + 1) * GATE_PIECE)
        gc = jnp.dot(h, wgc_ref[...], preferred_element_type=_f32)
        ga = jnp.dot(h, wga_ref[...], preferred_element_type=_f32)
        yc = jnp.dot(cb, wc_ref[...], preferred_element_type=_f32)
        ya = jnp.dot(at, wa_ref[...], preferred_element_type=_f32)
        t = (jax.nn.sigmoid(gc) * yc + jax.nn.sigmoid(ga) * ya).astype(_bf16)
        tm_ref[:, cols] = t[0:TM, :]
        ts_ref[:, cols] = t[TM:, :]


def _gate(h_all, cb_all, atm, ats, w_in_b, wc, wa):
    row = lambda n, i: (i, 0)
    once = pl.Buffered(1)
    w_specs, w_args = [], []
    for c in range(GATE_PIECES):
        piece = lambda n, i, c=c, base=0: (0, base + n * GATE_PIECES + c)
        w_specs += [
            pl.BlockSpec((D_MODEL, GATE_PIECE),
                         functools.partial(piece, base=OFF_GC // GATE_PIECE), pipeline_mode=once),
            pl.BlockSpec((D_MODEL, GATE_PIECE),
                         functools.partial(piece, base=OFF_GA // GATE_PIECE), pipeline_mode=once),
            pl.BlockSpec((D_CONV, GATE_PIECE), piece, pipeline_mode=once),
            pl.BlockSpec((Q_DIM, GATE_PIECE), piece, pipeline_mode=once),
        ]
        w_args += [w_in_b, w_in_b, wc, wa]
    return pl.pallas_call(
        _gate_kernel,
        grid=(N_OUT_CHUNKS, N_TILES),
        in_specs=[
            pl.BlockSpec((TILE, D_MODEL), row),
            pl.BlockSpec((TILE, D_CONV), row),
            pl.BlockSpec((TM, Q_DIM), row),
            pl.BlockSpec((TS, Q_DIM), row),
        ] + w_specs,
        out_specs=[
            pl.BlockSpec((TM, TN_OUT), lambda n, i: (i, n)),
            pl.BlockSpec((TS, TN_OUT), lambda n, i: (i, n)),
        ],
        out_shape=[
            jax.ShapeDtypeStruct((N_MAIN, D_MODEL), _bf16),
            jax.ShapeDtypeStruct((N_SMALL, D_MODEL), _bf16),
        ],
        scratch_shapes=[pltpu.VMEM((TILE, Q_DIM), _bf16)],
        compiler_params=_params(("arbitrary", "arbitrary")),
        name="gate",
    )(h_all, cb_all, atm, ats, *w_args)


def _sample_rows(a):
    return a.transpose(1, 0, 2).reshape(N_SAMPLE, a.shape[-1])


def _to_small(sample_rows, dtype):
    w = sample_rows.shape[-1]
    return jnp.concatenate([
        jnp.zeros((SAMPLE_ROW0, w), dtype), sample_rows.astype(dtype),
        jnp.zeros((N_SMALL - SAMPLE_ROW0 - N_SAMPLE, w), dtype)], axis=0)


def kernel(x_prompt, x_sample, state_conv, cache_k_win, cache_v_win, meta_tokens, ffn1_norm, ffn1_w_up, ffn1_w_down, mix_norm, w_in, q_norm, k_norm, conv_w, w_conv_out, attn_sinks, w_attn_out, w_o, ffn2_norm, ffn2_w_up, ffn2_w_down):
    l = 0
    xm = x_prompt.reshape(N_MAIN, D_MODEL)
    xs = jnp.concatenate([
        meta_tokens.astype(_f32), _sample_rows(x_sample),
        jnp.zeros((N_SMALL - N_META - N_SAMPLE, D_MODEL), _f32)], axis=0)

    g1 = ffn1_norm[l].reshape(1, D_MODEL)
    g_mix = mix_norm[l].reshape(1, D_MODEL)
    *first_tile, wgu1, wd1 = _ffn_first(xm, xs, g1, g_mix, ffn1_w_up[l], ffn1_w_down[l])
    jobs = (_w_up_job(16), _w_down_job(32), _cast_job(D_MODEL, D_IN, 16),
            _cast_job(D_CONV, D_MODEL, 16), _cast_job(Q_DIM, D_MODEL, 16),
            _cast_job(D_MODEL, D_MODEL, 16))
    x1m, x1s, h_all, wgu2, wd2, w_in_b, wc_b, wa_b, wo_b = _ffn_main(
        xm, xs, g1, wgu1, wd1, g_mix, first_tile, jobs,
        (ffn2_w_up[l], ffn2_w_down[l], w_in[l], w_conv_out[l], w_attn_out[l], w_o[l]))

    state_rows = state_conv[l].transpose(1, 0, 2).reshape((CONV_WIDTH - 1) * DEC_BATCH, D_CONV)
    cb_all, qm, km, vm, qkv_s, new_conv_p, new_conv_s = _inproj(
        h_all, w_in_b, conv_w[l], state_rows, q_norm[l], k_norm[l])

    atm, new_k_p, new_v_p = _attn_prompt(qm, km, vm, qkv_s, attn_sinks[l])
    samp = qkv_s[SAMPLE_ROW0:SAMPLE_ROW0 + N_SAMPLE]
    q_s = (samp[:, 0:Q_DIM].reshape(DEC_SEQ, DEC_BATCH, N_HEADS, HEAD_DIM)
           .transpose(1, 2, 0, 3).reshape(DEC_BATCH, N_QROWS, HEAD_DIM))
    q_s = jnp.tile(q_s, (1, 1, N_KV_HEADS))
    k_new = samp[:, Q_DIM:Q_DIM + KV_DIM].reshape(DEC_SEQ, DEC_BATCH, KV_DIM).transpose(1, 0, 2)
    v_new = samp[:, Q_DIM + KV_DIM:N_QKV].reshape(DEC_SEQ, DEC_BATCH, KV_DIM).transpose(1, 0, 2)
    sink_rows = jnp.repeat(attn_sinks[l].astype(_f32), DEC_SEQ).reshape(N_QROWS, 1)
    o_s, new_k_s, new_v_s = _attn_sample(
        q_s, k_new, v_new,
        cache_k_win[l].reshape(DEC_BATCH, WINDOW, KV_DIM),
        cache_v_win[l].reshape(DEC_BATCH, WINDOW, KV_DIM), sink_rows)
    at_rows = (o_s.reshape(DEC_BATCH, N_HEADS, DEC_SEQ, HEAD_DIM)
               .transpose(2, 0, 1, 3).reshape(N_SAMPLE, Q_DIM))
    ats = _to_small(at_rows, _bf16)

    t_main, t_small = _gate(h_all, cb_all, atm, ats, w_in_b, wc_b, wa_b)
    ym, ys = _ffn_stream(x1m, x1s, ffn2_norm[l].reshape(1, D_MODEL), wgu2, wd2,
                         proj=(t_main, t_small, wo_b))

    y_prompt = ym.reshape(BATCH, SEQ, D_MODEL)
    y_sample = (ys[SAMPLE_ROW0:SAMPLE_ROW0 + N_SAMPLE]
                .reshape(DEC_SEQ, DEC_BATCH, D_MODEL).transpose(1, 0, 2))
    kv_shape_p = (1, BATCH, WINDOW, N_KV_HEADS, HEAD_DIM)
    kv_shape_s = (1, DEC_BATCH, WINDOW, N_KV_HEADS, HEAD_DIM)
    return (y_prompt, y_sample,
            new_conv_p[None],
            new_k_p.reshape(kv_shape_p), new_v_p.reshape(kv_shape_p),
            new_conv_s.reshape(CONV_WIDTH - 1, DEC_BATCH, D_CONV).transpose(1, 0, 2)[None],
            new_k_s.reshape(kv_shape_s), new_v_s.reshape(kv_shape_s))
```

```python
import functools
from typing import NamedTuple

import jax
import jax.numpy as jnp
from jax import lax
from jax.experimental import pallas as pl
from jax.experimental.pallas import tpu as pltpu

D_MODEL = 2048
BATCH = 4
SEQ = 2048
DEC_BATCH = 32
DEC_SEQ = 4
PAST_LEN = 16384
N_META = 16
D_CONV = D_MODEL // 2
CONV_WIDTH = 3
HEAD_DIM = 64
N_HEADS = (D_MODEL // 2) // HEAD_DIM
N_KV_HEADS = N_HEADS // 4
GROUP = N_HEADS // N_KV_HEADS
Q_DIM = N_HEADS * HEAD_DIM
KV_DIM = N_KV_HEADS * HEAD_DIM
PAIR = 2 * HEAD_DIM
WINDOW = 128
D_FF = ((8 * D_MODEL // 3 + 127) // 128) * 128
D_IN = 3 * D_CONV + Q_DIM + 2 * KV_DIM + 2 * D_MODEL
EPS = 1e-6
NEG_INF = -1e30
LOG2E = 1.4426950408889634
Q_SCALE = HEAD_DIM ** -0.5 * LOG2E

OFF_XC = 0
OFF_BG = D_CONV
OFF_CG = 2 * D_CONV
OFF_Q = 3 * D_CONV
OFF_K = OFF_Q + Q_DIM
OFF_V = OFF_K + KV_DIM
OFF_GC = OFF_V + KV_DIM
OFF_GA = OFF_GC + D_MODEL
N_QKV = Q_DIM + 2 * KV_DIM

N_MAIN = BATCH * SEQ
N_SAMPLE = DEC_BATCH * DEC_SEQ
N_SMALL = 256
SAMPLE_ROW0 = N_META

N_TILES = 16
TM = N_MAIN // N_TILES
TS = N_SMALL // N_TILES
TILE = TM + TS
N_ALL = N_TILES * TILE
TILES_PER_SEQ = SEQ // TM

TF = 512
D_FF_PAD = ((D_FF + TF - 1) // TF) * TF
N_FF_CHUNKS = D_FF_PAD // TF
FFN_STEPS = N_TILES * N_FF_CHUNKS
TN_OUT = 1024
N_OUT_CHUNKS = D_MODEL // TN_OUT
GATE_PIECE = 512
GATE_PIECES = TN_OUT // GATE_PIECE

QBLK = 128
KPAD = QBLK - N_META
N_QBLK = SEQ // QBLK
KBUF_ROWS = QBLK + SEQ

VMEM_LIMIT = 56 * 1024 * 1024

_bf16 = jnp.bfloat16
_f32 = jnp.float32


def _params(sem):
    return pltpu.CompilerParams(dimension_semantics=sem, vmem_limit_bytes=VMEM_LIMIT)


def _rms_rows(x, g):
    ms = jnp.mean(x * x, axis=-1, keepdims=True)
    return x * lax.rsqrt(ms + EPS) * g


def _pair_rms(x, g_pair):
    low = lax.broadcasted_iota(jnp.int32, x.shape, 1) < HEAD_DIM
    x2 = x * x
    s_lo = jnp.sum(jnp.where(low, x2, 0.0), axis=-1, keepdims=True)
    s_hi = jnp.sum(jnp.where(low, 0.0, x2), axis=-1, keepdims=True)
    ms = jnp.where(low, s_lo, s_hi) * (1.0 / HEAD_DIM)
    return x * lax.rsqrt(ms + EPS) * g_pair


class _CastJob(NamedTuple):
    slab: int
    rows_in: int
    n_in_slabs: int
    n_out_slabs: int
    n_in_cols: int
    n_out_cols: int
    segments: tuple
    zero_ranges: tuple


JOB_GROUP = 2


def _cast_job(rows_in, cols_in, slab, segments=None, zero_ranges=(), rows_out=None, cols_out=None):
    rows_out = rows_in if rows_out is None else rows_out
    cols_out = cols_in if cols_out is None else cols_out
    segments = ((0, cols_in, 0),) if segments is None else segments
    assert rows_out % (JOB_GROUP * slab) == 0 and rows_in % (JOB_GROUP * slab) == 0
    return _CastJob(slab, rows_in, rows_in // slab, rows_out // slab, cols_in, cols_out,
                    tuple(segments), tuple(zero_ranges))


def _job_specs(job, step_of):
    rows = JOB_GROUP * job.slab
    in_idx = lambda *ids: (jnp.minimum(step_of(*ids), job.n_in_slabs - 1) // JOB_GROUP, 0)
    out_idx = lambda *ids: (jnp.minimum(step_of(*ids), job.n_out_slabs - 1) // JOB_GROUP, 0)
    return (pl.BlockSpec((rows, job.n_in_cols), in_idx),
            pl.BlockSpec((rows, job.n_out_cols), out_idx))


def _run_cast_job(job, step, x_ref, o_ref):
    out_slab = jnp.minimum(step, job.n_out_slabs - 1)
    in_slab = jnp.minimum(step, job.n_in_slabs - 1)
    in_rows = pl.ds(pl.multiple_of((in_slab % JOB_GROUP) * job.slab, job.slab), job.slab)
    out_rows = pl.ds(pl.multiple_of((out_slab % JOB_GROUP) * job.slab, job.slab), job.slab)
    padded = job.n_out_slabs > job.n_in_slabs
    for src, width, dst in job.segments:
        x = x_ref[in_rows, src:src + width]
        if padded:
            row = out_slab * job.slab + lax.broadcasted_iota(jnp.int32, x.shape, 0)
            x = jnp.where(row < job.rows_in, x, 0.0)
        o_ref[out_rows, dst:dst + width] = x.astype(_bf16)
    for lo, hi in job.zero_ranges:
        o_ref[out_rows, lo:hi] = jnp.zeros((job.slab, hi - lo), _bf16)


def _w_up_job(slab):
    segments, zero_ranges = [], []
    for c in range(N_FF_CHUNKS):
        width = min(TF, D_FF - c * TF)
        for half, src0 in enumerate((0, D_FF)):
            dst = (2 * c + half) * TF
            segments.append((src0 + c * TF, width, dst))
            if width < TF:
                zero_ranges.append((dst + width, dst + TF))
    return _cast_job(D_MODEL, 2 * D_FF, slab, segments=segments, zero_ranges=zero_ranges,
                     cols_out=2 * D_FF_PAD)


def _w_down_job(slab):
    return _cast_job(D_FF, D_MODEL, slab, rows_out=D_FF_PAD)


FC = 256
N_FC = D_FF_PAD // FC
assert TF % FC == 0
NF = 2


def _swiglu_chunk(h, w_gate_up, w_down, n):
    gu = jnp.dot(h, w_gate_up, preferred_element_type=_f32)
    gate, up = gu[:, 0:n], gu[:, n:2 * n]
    a = (gate * jax.nn.sigmoid(gate) * up * 0.5).astype(_bf16)
    return jnp.dot(a, w_down, preferred_element_type=_f32)


def _ffn_first_kernel(xm_ref, xs_ref, g_ref, gn_ref, wup_hbm, wdn_hbm,
                      om_ref, os_ref, hn_ref, wgu_out, wd_out,
                      h_ref, stage_g, stage_u, stage_d, cast_gu, cast_d, sem_in, sem_out):
    def width(c):
        return min(FC, D_FF - c * FC)

    def in_copies(c):
        s, w = c % 2, width(c)
        return (
            pltpu.make_async_copy(wup_hbm.at[:, pl.ds(c * FC, w)],
                                  stage_g.at[s, :, pl.ds(0, w)], sem_in.at[0, s]),
            pltpu.make_async_copy(wup_hbm.at[:, pl.ds(D_FF + c * FC, w)],
                                  stage_u.at[s, :, pl.ds(0, w)], sem_in.at[1, s]),
            pltpu.make_async_copy(wdn_hbm.at[pl.ds(c * FC, w), :],
                                  stage_d.at[s, pl.ds(0, w), :], sem_in.at[2, s]),
        )

    def out_copies(c):
        s = c % 2
        col = (c * FC // TF) * 2 * TF + (c * FC) % TF
        return (
            pltpu.make_async_copy(cast_gu.at[s, :, pl.ds(0, FC)],
                                  wgu_out.at[:, pl.ds(col, FC)], sem_out.at[0, s]),
            pltpu.make_async_copy(cast_gu.at[s, :, pl.ds(FC, FC)],
                                  wgu_out.at[:, pl.ds(col + TF, FC)], sem_out.at[1, s]),
            pltpu.make_async_copy(cast_d.at[s], wd_out.at[pl.ds(c * FC, FC), :], sem_out.at[2, s]),
        )

    def start(copies):
        for cp in copies:
            cp.start()

    def wait(copies):
        for cp in copies:
            cp.wait()

    start(in_copies(0))
    g = g_ref[...]
    om_ref[...] = xm_ref[...]
    os_ref[...] = xs_ref[...]
    for t in range(NF):
        h_ref[t * TILE:t * TILE + TM, :] = _rms_rows(
            xm_ref[t * TM:(t + 1) * TM, :], g).astype(_bf16)
        h_ref[t * TILE + TM:(t + 1) * TILE, :] = _rms_rows(
            xs_ref[t * TS:(t + 1) * TS, :], g).astype(_bf16)

    for c in range(N_FC):
        s, w = c % 2, width(c)
        if c + 1 < N_FC:
            start(in_copies(c + 1))
        wait(in_copies(c))
        if c >= 2:
            wait(out_copies(c - 2))
        if w == FC:
            cast_gu[s, :, 0:FC] = stage_g[s].astype(_bf16)
            cast_gu[s, :, FC:2 * FC] = stage_u[s].astype(_bf16)
            cast_d[s] = stage_d[s].astype(_bf16)
        else:
            lane = lax.broadcasted_iota(jnp.int32, (D_MODEL, FC), 1)
            sub = lax.broadcasted_iota(jnp.int32, (FC, D_MODEL), 0)
            cast_gu[s, :, 0:FC] = jnp.where(lane < w, stage_g[s], 0.0).astype(_bf16)
            cast_gu[s, :, FC:2 * FC] = jnp.where(lane < w, stage_u[s], 0.0).astype(_bf16)
            cast_d[s] = jnp.where(sub < w, stage_d[s], 0.0).astype(_bf16)
        start(out_copies(c))
        for t in range(NF):
            r = _swiglu_chunk(h_ref[t * TILE:(t + 1) * TILE, :], cast_gu[s], cast_d[s], FC)
            om_ref[t * TM:(t + 1) * TM, :] += r[0:TM, :]
            os_ref[t * TS:(t + 1) * TS, :] += r[TM:, :]
    wait(out_copies(N_FC - 2))
    wait(out_copies(N_FC - 1))

    gn = gn_ref[...]
    for t in range(NF):
        hn_ref[t * TILE:t * TILE + TM, :] = _rms_rows(
            om_ref[t * TM:(t + 1) * TM, :], gn).astype(_bf16)
        hn_ref[t * TILE + TM:(t + 1) * TILE, :] = _rms_rows(
            os_ref[t * TS:(t + 1) * TS, :], gn).astype(_bf16)


def _ffn_first(xm, xs, g, next_gain, w_up, w_down):
    first = lambda i: (0, 0)
    once = pl.Buffered(1)
    return pl.pallas_call(
        _ffn_first_kernel,
        grid=(1,),
        in_specs=[
            pl.BlockSpec((NF * TM, D_MODEL), first, pipeline_mode=once),
            pl.BlockSpec((NF * TS, D_MODEL), first, pipeline_mode=once),
            pl.BlockSpec((1, D_MODEL), first),
            pl.BlockSpec((1, D_MODEL), first),
            pl.BlockSpec(memory_space=pl.ANY),
            pl.BlockSpec(memory_space=pl.ANY),
        ],
        out_specs=[
            pl.BlockSpec((NF * TM, D_MODEL), first, pipeline_mode=once),
            pl.BlockSpec((NF * TS, D_MODEL), first, pipeline_mode=once),
            pl.BlockSpec((NF * TILE, D_MODEL), first, pipeline_mode=once),
            pl.BlockSpec(memory_space=pl.ANY),
            pl.BlockSpec(memory_space=pl.ANY),
        ],
        out_shape=[
            jax.ShapeDtypeStruct((NF * TM, D_MODEL), _f32),
            jax.ShapeDtypeStruct((NF * TS, D_MODEL), _f32),
            jax.ShapeDtypeStruct((NF * TILE, D_MODEL), _bf16),
            jax.ShapeDtypeStruct((D_MODEL, 2 * D_FF_PAD), _bf16),
            jax.ShapeDtypeStruct((D_FF_PAD, D_MODEL), _bf16),
        ],
        scratch_shapes=[
            pltpu.VMEM((NF * TILE, D_MODEL), _bf16),
            pltpu.VMEM((2, D_MODEL, FC), _f32),
            pltpu.VMEM((2, D_MODEL, FC), _f32),
            pltpu.VMEM((2, FC, D_MODEL), _f32),
            pltpu.VMEM((2, D_MODEL, 2 * FC), _bf16),
            pltpu.VMEM((2, FC, D_MODEL), _bf16),
            pltpu.SemaphoreType.DMA((3, 2)),
            pltpu.SemaphoreType.DMA((3, 2)),
        ],
        compiler_params=_params(("arbitrary",)),
        name="ffn_first",
    )(xm, xs, g, next_gain, w_up, w_down)


def _ffn_main_kernel(jobs, *refs):
    refs = list(refs)
    take = lambda n: [refs.pop(0) for _ in range(n)]
    xm_ref, xs_ref, g_ref, wgu_ref, wd_ref, gn_ref, om0_hbm, os0_hbm, hn0_hbm = take(9)
    job_in = take(len(jobs))
    om_ref, os_ref, hn_ref = take(3)
    job_out = take(len(jobs))
    h_ref, sem = refs
    i = pl.program_id(0)
    j = pl.program_id(1)

    def run_jobs():
        for job, x_ref, o_ref in zip(jobs, job_in, job_out):
            _run_cast_job(job, i * N_FF_CHUNKS + j, x_ref, o_ref)

    @pl.when(i < NF)
    def _():
        @pl.when(j == 0)
        def _():
            rows = lambda n: pl.ds(pl.multiple_of(i * n, n), n)
            copies = (pltpu.make_async_copy(om0_hbm.at[rows(TM)], om_ref, sem.at[0]),
                      pltpu.make_async_copy(os0_hbm.at[rows(TS)], os_ref, sem.at[1]),
                      pltpu.make_async_copy(hn0_hbm.at[rows(TILE)], hn_ref, sem.at[2]))
            for cp in copies:
                cp.start()
            for cp in copies:
                cp.wait()
        run_jobs()

    @pl.when(i >= NF)
    def _():
        @pl.when(j == 0)
        def _():
            g = g_ref[...]
            xm = xm_ref[...]
            xs = xs_ref[...]
            h_ref[0:TM, :] = _rms_rows(xm, g).astype(_bf16)
            om_ref[...] = xm
            h_ref[TM:, :] = _rms_rows(xs, g).astype(_bf16)
            os_ref[...] = xs

        run_jobs()
        r = _swiglu_chunk(h_ref[...], wgu_ref[...], wd_ref[...], TF)
        om_ref[...] += r[0:TM, :]
        os_ref[...] += r[TM:, :]

        @pl.when(j == N_FF_CHUNKS - 1)
        def _():
            gn = gn_ref[...]
            hn_ref[0:TM, :] = _rms_rows(om_ref[...], gn).astype(_bf16)
            hn_ref[TM:, :] = _rms_rows(os_ref[...], gn).astype(_bf16)


def _ffn_main(xm, xs, g, wgu, wd, next_gain, first_tile, jobs, job_inputs):
    assert all(job.n_out_slabs <= FFN_STEPS for job in jobs)
    job_specs = [_job_specs(job, lambda i, j: i * N_FF_CHUNKS + j) for job in jobs]
    row = lambda i, j: (i, 0)
    fixed = lambda i, j: (0, 0)
    chunk = lambda i, j: jnp.where(i < NF, 0, j)
    return pl.pallas_call(
        functools.partial(_ffn_main_kernel, tuple(jobs)),
        grid=(N_TILES, N_FF_CHUNKS),
        in_specs=[
            pl.BlockSpec((TM, D_MODEL), row),
            pl.BlockSpec((TS, D_MODEL), row),
            pl.BlockSpec((1, D_MODEL), fixed),
            pl.BlockSpec((D_MODEL, 2 * TF), lambda i, j: (0, chunk(i, j))),
            pl.BlockSpec((TF, D_MODEL), lambda i, j: (chunk(i, j), 0)),
            pl.BlockSpec((1, D_MODEL), fixed),
        ] + [pl.BlockSpec(memory_space=pl.ANY) for _ in first_tile] + [s[0] for s in job_specs],
        out_specs=[
            pl.BlockSpec((TM, D_MODEL), row),
            pl.BlockSpec((TS, D_MODEL), row),
            pl.BlockSpec((TILE, D_MODEL), row),
        ] + [s[1] for s in job_specs],
        out_shape=[
            jax.ShapeDtypeStruct((N_MAIN, D_MODEL), _f32),
            jax.ShapeDtypeStruct((N_SMALL, D_MODEL), _f32),
            jax.ShapeDtypeStruct((N_ALL, D_MODEL), _bf16),
        ] + [jax.ShapeDtypeStruct((job.n_out_slabs * job.slab, job.n_out_cols), _bf16)
             for job in jobs],
        scratch_shapes=[pltpu.VMEM((TILE, D_MODEL), _bf16), pltpu.SemaphoreType.DMA((3,))],
        compiler_params=_params(("arbitrary", "arbitrary")),
        name="ffn",
    )(xm, xs, g, wgu, wd, next_gain, *first_tile, *job_inputs)


def _ffn_stream_kernel(xm_ref, xs_ref, tm_ref, ts_ref, wo_ref, g_ref, wgu_hbm, wd_hbm,
                       om_ref, os_ref, h_ref, t_all, wgu_buf, wd_buf, sem):
    i = pl.program_id(0)

    def chunk_copies(chunk, slot):
        return (
            pltpu.make_async_copy(wgu_hbm.at[:, pl.ds(chunk * 2 * TF, 2 * TF)],
                                  wgu_buf.at[slot], sem.at[0, slot]),
            pltpu.make_async_copy(wd_hbm.at[pl.ds(chunk * TF, TF), :],
                                  wd_buf.at[slot], sem.at[1, slot]),
        )

    def start(chunk, slot):
        for c in chunk_copies(chunk, slot):
            c.start()

    @pl.when(i == 0)
    def _():
        start(0, 0)

    g = g_ref[...]
    t_all[0:TM, :] = tm_ref[...]
    t_all[TM:, :] = ts_ref[...]
    p = jnp.dot(t_all[...], wo_ref[...], preferred_element_type=_f32)
    xm = xm_ref[...] + p[0:TM, :]
    xs = xs_ref[...] + p[TM:, :]
    h_ref[0:TM, :] = _rms_rows(xm, g).astype(_bf16)
    om_ref[...] = xm
    h_ref[TM:, :] = _rms_rows(xs, g).astype(_bf16)
    os_ref[...] = xs

    h = h_ref[...]
    for j in range(N_FF_CHUNKS):
        slot = (i * N_FF_CHUNKS + j) % 2
        if j + 1 < N_FF_CHUNKS:
            start(j + 1, 1 - slot)
        else:
            @pl.when(i + 1 < N_TILES)
            def _():
                start(0, 1 - slot)
        for c in chunk_copies(j, slot):
            c.wait()
        r = _swiglu_chunk(h, wgu_buf[slot], wd_buf[slot], TF)
        om_ref[...] += r[0:TM, :]
        os_ref[...] += r[TM:, :]


def _ffn_stream(xm, xs, g, wgu, wd, proj):
    row = lambda i: (i, 0)
    fixed = lambda i: (0, 0)
    return pl.pallas_call(
        _ffn_stream_kernel,
        grid=(N_TILES,),
        in_specs=[
            pl.BlockSpec((TM, D_MODEL), row),
            pl.BlockSpec((TS, D_MODEL), row),
            pl.BlockSpec((TM, D_MODEL), row),
            pl.BlockSpec((TS, D_MODEL), row),
            pl.BlockSpec((D_MODEL, D_MODEL), fixed, pipeline_mode=pl.Buffered(1)),
            pl.BlockSpec((1, D_MODEL), fixed),
            pl.BlockSpec(memory_space=pl.ANY),
            pl.BlockSpec(memory_space=pl.ANY),
        ],
        out_specs=[pl.BlockSpec((TM, D_MODEL), row), pl.BlockSpec((TS, D_MODEL), row)],
        out_shape=[jax.ShapeDtypeStruct((N_MAIN, D_MODEL), _f32),
                   jax.ShapeDtypeStruct((N_SMALL, D_MODEL), _f32)],
        scratch_shapes=[
            pltpu.VMEM((TILE, D_MODEL), _bf16),
            pltpu.VMEM((TILE, D_MODEL), _bf16),
            pltpu.VMEM((2, D_MODEL, 2 * TF), _bf16),
            pltpu.VMEM((2, TF, D_MODEL), _bf16),
            pltpu.SemaphoreType.DMA((2, 2)),
        ],
        compiler_params=_params(("arbitrary",)),
        name="ffn_stream",
    )(xm, xs, *proj, g, wgu, wd)


N_BRANCH_IN = OFF_GC
CONV_HALO = 8
U_BASE = 112
U_STATE0 = U_BASE + SAMPLE_ROW0 - (CONV_WIDTH - 1) * DEC_BATCH
U_ROWS = U_BASE + N_SMALL
U_LAST0 = U_BASE + SAMPLE_ROW0 + (DEC_SEQ - (CONV_WIDTH - 1)) * DEC_BATCH
assert U_BASE >= (CONV_WIDTH - 1) * DEC_BATCH and U_BASE % TS == 0 and TS == N_META


def _inproj_kernel(h_ref, w_ref, cw_ref, st_ref, qg_ref, kg_ref,
                   cb_ref, q_ref, k_ref, v_ref, sm_ref, ncp_ref, ncs_ref,
                   ubuf, meta_halo, utab):
    i = pl.program_id(0)

    @pl.when(i == 0)
    def _():
        utab[...] = jnp.zeros(utab.shape, _f32)
        utab[U_STATE0:U_STATE0 + (CONV_WIDTH - 1) * DEC_BATCH, :] = st_ref[...]
        meta_halo[...] = jnp.zeros(meta_halo.shape, _f32)
        ubuf[0:CONV_HALO, :] = jnp.zeros((CONV_HALO, D_CONV), _f32)

    h = h_ref[...]
    col_dot = lambda off, width: jnp.dot(h, w_ref[:, off:off + width],
                                         preferred_element_type=_f32)
    z_q = col_dot(OFF_Q, Q_DIM)
    z_kv = col_dot(OFF_K, 2 * KV_DIM)
    z_xc = col_dot(OFF_XC, D_CONV)
    z_cg = col_dot(OFF_CG, D_CONV)
    z_bg = col_dot(OFF_BG, D_CONV)
    cw = cw_ref[...]
    w0, w1, w2 = cw[0:1, :], cw[1:2, :], cw[2:3, :]
    qg = qg_ref[...]
    kg = kg_ref[...]

    def qkv(rows, q_out, k_out, v_out):
        for t in range(Q_DIM // PAIR):
            q_out(t, _pair_rms(z_q[rows, t * PAIR:(t + 1) * PAIR], qg) * Q_SCALE)
        for t in range(KV_DIM // PAIR):
            k_out(t, _pair_rms(z_kv[rows, t * PAIR:(t + 1) * PAIR], kg))
            v_out(t, z_kv[rows, KV_DIM + t * PAIR:KV_DIM + (t + 1) * PAIR])

    u_s = z_cg[TM:, :] * z_xc[TM:, :]
    cur = pl.multiple_of(U_BASE + TS * i, TS)
    slot = pl.multiple_of(jnp.where(i == 0, U_ROWS, cur), TS)
    utab[pl.ds(slot, TS), :] = u_s
    y_s = (w0 * utab[pl.ds(cur - 2 * DEC_BATCH, TS), :]
           + w1 * utab[pl.ds(cur - DEC_BATCH, TS), :] + w2 * u_s)
    cb_ref[TM:, :] = (z_bg[TM:, :] * y_s).astype(_bf16)
    meta_halo[...] = jnp.where(i == 0, u_s[N_META - CONV_HALO:N_META, :], meta_halo[...])

    def sm_store(off):
        def store(t, val):
            sm_ref[:, off + t * PAIR:off + (t + 1) * PAIR] = val
        return store
    qkv(slice(TM, TILE), sm_store(0), sm_store(Q_DIM), sm_store(Q_DIM + KV_DIM))

    ncs_ref[...] = utab[U_LAST0:U_LAST0 + (CONV_WIDTH - 1) * DEC_BATCH, :]

    ubuf[0:CONV_HALO, :] = jnp.where(i % TILES_PER_SEQ == 0, meta_halo[...], ubuf[0:CONV_HALO, :])

    u = z_cg[0:TM, :] * z_xc[0:TM, :]
    ubuf[CONV_HALO:CONV_HALO + TM, :] = u
    u1 = ubuf[CONV_HALO - 1:CONV_HALO - 1 + TM, :]
    u2 = ubuf[CONV_HALO - 2:CONV_HALO - 2 + TM, :]
    cb_ref[0:TM, :] = (z_bg[0:TM, :] * (w0 * u2 + w1 * u1 + w2 * u)).astype(_bf16)
    tail = ubuf[TM:TM + CONV_HALO, :]
    ubuf[0:CONV_HALO, :] = tail
    ncp_ref[0] = tail[CONV_HALO - (CONV_WIDTH - 1):CONV_HALO, :]

    def lane_store(ref):
        def store(t, val):
            ref[:, t * PAIR:(t + 1) * PAIR] = val
        return store
    qkv(slice(0, TM), lane_store(q_ref), lane_store(k_ref), lane_store(v_ref))


def _inproj(h_all, w_in_b, conv_w, state_rows, q_gain, k_gain):
    qg = jnp.tile(q_gain.reshape(1, HEAD_DIM), (1, 2))
    kg = jnp.tile(k_gain.reshape(1, HEAD_DIM), (1, 2))
    row = lambda i: (i, 0)
    fixed = lambda i: (0, 0)
    return pl.pallas_call(
        _inproj_kernel,
        grid=(N_TILES,),
        in_specs=[
            pl.BlockSpec((TILE, D_MODEL), row),
            pl.BlockSpec((D_MODEL, N_BRANCH_IN), fixed, pipeline_mode=pl.Buffered(1)),
            pl.BlockSpec((CONV_WIDTH, D_CONV), fixed),
            pl.BlockSpec(((CONV_WIDTH - 1) * DEC_BATCH, D_CONV), fixed),
            pl.BlockSpec((1, PAIR), fixed),
            pl.BlockSpec((1, PAIR), fixed),
        ],
        out_specs=[
            pl.BlockSpec((TILE, D_CONV), row),
            pl.BlockSpec((TM, Q_DIM), row),
            pl.BlockSpec((TM, KV_DIM), row),
            pl.BlockSpec((TM, KV_DIM), row),
            pl.BlockSpec((TS, N_QKV), row),
            pl.BlockSpec((1, CONV_WIDTH - 1, D_CONV), lambda i: (i // TILES_PER_SEQ, 0, 0)),
            pl.BlockSpec(((CONV_WIDTH - 1) * DEC_BATCH, D_CONV), fixed),
        ],
        out_shape=[
            jax.ShapeDtypeStruct((N_ALL, D_CONV), _bf16),
            jax.ShapeDtypeStruct((N_MAIN, Q_DIM), _f32),
            jax.ShapeDtypeStruct((N_MAIN, KV_DIM), _f32),
            jax.ShapeDtypeStruct((N_MAIN, KV_DIM), _f32),
            jax.ShapeDtypeStruct((N_SMALL, N_QKV), _f32),
            jax.ShapeDtypeStruct((BATCH, CONV_WIDTH - 1, D_CONV), _f32),
            jax.ShapeDtypeStruct(((CONV_WIDTH - 1) * DEC_BATCH, D_CONV), _f32),
        ],
        scratch_shapes=[
            pltpu.VMEM((CONV_HALO + TM, D_CONV), _f32),
            pltpu.VMEM((CONV_HALO, D_CONV), _f32),
            pltpu.VMEM((U_ROWS + TS, D_CONV), _f32),
        ],
        compiler_params=_params(("arbitrary",)),
        name="inproj",
    )(h_all, w_in_b, conv_w, state_rows, qg, kg)


QPAIR = 2 * GROUP * HEAD_DIM
HEADS_PER_STEP = 2 * GROUP


def _attn_prompt_kernel(sink_ref, q_ref, k_ref, v_ref, km_ref, vm_ref,
                        o_ref, nk_ref, nv_ref, kt_buf, vbuf, bias_ref):
    gp = pl.program_id(1)

    for order in range(2):
        swap = (lambda x: x) if order == 0 else (lambda x: pltpu.roll(x, HEAD_DIM, axis=1))
        vbuf[order, 0:KPAD, :] = jnp.zeros((KPAD, PAIR), _bf16)
        vbuf[order, KPAD:QBLK, :] = swap(vm_ref[...]).astype(_bf16)
        vbuf[order, QBLK:KBUF_ROWS, :] = swap(v_ref[...]).astype(_bf16)
        first_keys = jnp.concatenate([jnp.zeros((KPAD, PAIR), _f32), km_ref[...]], axis=0)
        kt_buf[order, 0] = swap(first_keys).T.astype(_bf16)
        for b in range(N_QBLK):
            kt_buf[order, b + 1] = swap(k_ref[b * QBLK:(b + 1) * QBLK, :]).T.astype(_bf16)
    nk_ref[0] = k_ref[SEQ - WINDOW:SEQ, :]
    nv_ref[0] = v_ref[SEQ - WINDOW:SEQ, :]

    partner = lambda p: p ^ 1
    row = lax.broadcasted_iota(jnp.int32, (QBLK, 2 * QBLK), 0)
    col = lax.broadcasted_iota(jnp.int32, (QBLK, 2 * QBLK), 1)
    dist = QBLK + row - col
    in_window = (dist >= 0) & (dist <= WINDOW)
    distf = dist.astype(_f32)
    upper = lax.broadcasted_iota(jnp.int32, (2 * QBLK, 1), 0) < QBLK
    sink_cols = []
    for p in range(GROUP):
        sinks = []
        for g, hp in ((0, p), (1, partner(p))):
            head = gp * HEADS_PER_STEP + g * GROUP + hp
            slope = jnp.exp2(jnp.full((QBLK, 2 * QBLK), -8.0 / N_HEADS, _f32)
                             * (head + 1).astype(_f32))
            bias_ref[p, g * QBLK:(g + 1) * QBLK, :] = jnp.where(
                in_window, -slope * distf * LOG2E, NEG_INF)
            sinks.append(sink_ref[head] * LOG2E)
        sink_cols.append(jnp.where(upper, sinks[0], sinks[1]))

    lane = lax.broadcasted_iota(jnp.int32, (QBLK, PAIR), 1)
    low_half = lane < HEAD_DIM
    pad_col = lax.broadcasted_iota(jnp.int32, (2 * QBLK, 2 * QBLK), 1) < KPAD

    def block(i, first):
        r0 = i * QBLK if first else pl.multiple_of(i * QBLK, QBLK)
        qn = q_ref[pl.ds(r0 - QBLK, QBLK), :]
        outs = [[None] * GROUP for _ in range(2)]
        for p in range(GROUP):
            order = p % 2
            kt = jnp.concatenate([kt_buf[order, i - 1], kt_buf[order, i]], axis=1)
            vv = vbuf[order, pl.ds(r0 - QBLK, 2 * QBLK), :]
            tiles = []
            for g, hp in ((0, p), (1, partner(p))):
                hh = g * GROUP + hp
                in_low = hp % 2 == 0
                tile = qn[:, (hh // 2) * PAIR:(hh // 2 + 1) * PAIR]
                tiles.append(jnp.where(low_half if in_low else jnp.logical_not(low_half),
                                       tile, 0.0).astype(_bf16))
            qst = jnp.concatenate(tiles, axis=0)
            s = jnp.dot(qst, kt, preferred_element_type=_f32) + bias_ref[p]
            if first:
                s = jnp.where(pad_col, NEG_INF, s)
            sink = sink_cols[p]
            m = jnp.maximum(jnp.max(s, axis=-1, keepdims=True), sink)
            e = jnp.exp2(s - m)
            denom = jnp.sum(e, axis=-1, keepdims=True) + jnp.exp2(sink - m)
            o = jnp.dot(e.astype(_bf16), vv, preferred_element_type=_f32) / denom
            outs[0][p] = o[0:QBLK, :]
            outs[1][partner(p)] = o[QBLK:2 * QBLK, :]
        for g in range(2):
            for pr in range(GROUP // 2):
                t = g * (GROUP // 2) + pr
                o_ref[pl.ds(r0 - QBLK, QBLK), t * PAIR:(t + 1) * PAIR] = jnp.where(
                    low_half, outs[g][2 * pr], outs[g][2 * pr + 1]).astype(_bf16)

    block(1, True)

    def later_block(i, carry):
        block(i, False)
        return carry

    lax.fori_loop(2, N_QBLK + 1, later_block, 0)


def _attn_prompt(qm, km, vm, qkv_s, sinks):
    grid_spec = pltpu.PrefetchScalarGridSpec(
        num_scalar_prefetch=1,
        grid=(BATCH, N_KV_HEADS // 2),
        in_specs=[
            pl.BlockSpec((SEQ, QPAIR), lambda b, p, s: (b, p)),
            pl.BlockSpec((SEQ, PAIR), lambda b, p, s: (b, p)),
            pl.BlockSpec((SEQ, PAIR), lambda b, p, s: (b, p)),
            pl.BlockSpec((N_META, PAIR), lambda b, p, s: (0, Q_DIM // PAIR + p)),
            pl.BlockSpec((N_META, PAIR), lambda b, p, s: (0, (Q_DIM + KV_DIM) // PAIR + p)),
        ],
        out_specs=[
            pl.BlockSpec((SEQ, QPAIR), lambda b, p, s: (b, p)),
            pl.BlockSpec((1, WINDOW, PAIR), lambda b, p, s: (b, 0, p)),
            pl.BlockSpec((1, WINDOW, PAIR), lambda b, p, s: (b, 0, p)),
        ],
        scratch_shapes=[
            pltpu.VMEM((2, N_QBLK + 1, PAIR, QBLK), _bf16),
            pltpu.VMEM((2, KBUF_ROWS, PAIR), _bf16),
            pltpu.VMEM((GROUP, 2 * QBLK, 2 * QBLK), _f32),
        ],
    )
    return pl.pallas_call(
        _attn_prompt_kernel,
        grid_spec=grid_spec,
        out_shape=[
            jax.ShapeDtypeStruct((N_MAIN, Q_DIM), _bf16),
            jax.ShapeDtypeStruct((BATCH, WINDOW, KV_DIM), _f32),
            jax.ShapeDtypeStruct((BATCH, WINDOW, KV_DIM), _f32),
        ],
        compiler_params=_params(("arbitrary", "arbitrary")),
        name="attn_prompt",
    )(sinks, qm, km, vm, qkv_s, qkv_s)


SEQ_CHUNK = 8
N_QROWS = N_HEADS * DEC_SEQ
N_KEYS = WINDOW + DEC_SEQ
N_KEYS_PAD = ((N_KEYS + 7) // 8) * 8


def _attn_sample_kernel(q_ref, kn_ref, vn_ref, ck_ref, cv_ref, sink_ref,
                        o_ref, nk_ref, nv_ref, kk_buf, vv_buf):
    q4 = q_ref[...]
    shp = (SEQ_CHUNK, N_QROWS, KV_DIM)
    row_kv = lax.broadcasted_iota(jnp.int32, shp, 1) // (GROUP * DEC_SEQ)
    lane_kv = lax.broadcasted_iota(jnp.int32, shp, 2) // HEAD_DIM
    own = row_kv == lane_kv
    qp = jnp.where(own, q4, 0.0)

    kk_buf[:, 0:WINDOW, :] = ck_ref[...]
    kk_buf[:, WINDOW:N_KEYS, :] = kn_ref[...]
    kk_buf[:, N_KEYS:N_KEYS_PAD, :] = jnp.zeros((SEQ_CHUNK, N_KEYS_PAD - N_KEYS, KV_DIM), _f32)
    vv_buf[:, 0:WINDOW, :] = cv_ref[...]
    vv_buf[:, WINDOW:N_KEYS, :] = vn_ref[...]
    vv_buf[:, N_KEYS:N_KEYS_PAD, :] = jnp.zeros((SEQ_CHUNK, N_KEYS_PAD - N_KEYS, KV_DIM), _f32)
    nk_ref[...] = kk_buf[:, DEC_SEQ:N_KEYS, :]
    nv_ref[...] = vv_buf[:, DEC_SEQ:N_KEYS, :]

    kk = kk_buf[...]
    vv = vv_buf[...]
    s = jnp.einsum('bqd,bkd->bqk', qp, kk, preferred_element_type=_f32)

    row = lax.broadcasted_iota(jnp.int32, (N_QROWS, N_KEYS_PAD), 0)
    col = lax.broadcasted_iota(jnp.int32, (N_QROWS, N_KEYS_PAD), 1)
    step = row % DEC_SEQ
    head = (row // DEC_SEQ + 1).astype(_f32)
    dist = step + WINDOW - col
    valid = (dist >= 0) & (dist <= WINDOW) & (col < N_KEYS)
    slope = jnp.exp2(head * (-8.0 / N_HEADS))
    bias = jnp.where(valid, -slope * dist.astype(_f32) * LOG2E, NEG_INF)
    s = jnp.where(valid[None], s + bias[None], NEG_INF)
    sink = sink_ref[...][None] * LOG2E
    m = jnp.maximum(jnp.max(s, axis=-1, keepdims=True), sink)
    e = jnp.exp2(s - m)
    denom = jnp.sum(e, axis=-1, keepdims=True) + jnp.exp2(sink - m)
    o = jnp.einsum('bqk,bkd->bqd', e, vv, preferred_element_type=_f32) / denom
    o = jnp.where(own, o, 0.0)
    acc = o
    for g in range(1, N_KV_HEADS):
        acc = acc + pltpu.roll(o, g * HEAD_DIM, axis=2)
    o_ref[...] = acc[:, :, 0:HEAD_DIM]


def _attn_sample(q_s, k_new, v_new, cache_k, cache_v, sink_rows):
    c = SEQ_CHUNK
    return pl.pallas_call(
        _attn_sample_kernel,
        grid=(DEC_BATCH // c,),
        in_specs=[
            pl.BlockSpec((c, N_QROWS, KV_DIM), lambda i: (i, 0, 0)),
            pl.BlockSpec((c, DEC_SEQ, KV_DIM), lambda i: (i, 0, 0)),
            pl.BlockSpec((c, DEC_SEQ, KV_DIM), lambda i: (i, 0, 0)),
            pl.BlockSpec((c, WINDOW, KV_DIM), lambda i: (i, 0, 0)),
            pl.BlockSpec((c, WINDOW, KV_DIM), lambda i: (i, 0, 0)),
            pl.BlockSpec((N_QROWS, 1), lambda i: (0, 0)),
        ],
        out_specs=[
            pl.BlockSpec((c, N_QROWS, HEAD_DIM), lambda i: (i, 0, 0)),
            pl.BlockSpec((c, WINDOW, KV_DIM), lambda i: (i, 0, 0)),
            pl.BlockSpec((c, WINDOW, KV_DIM), lambda i: (i, 0, 0)),
        ],
        out_shape=[
            jax.ShapeDtypeStruct((DEC_BATCH, N_QROWS, HEAD_DIM), _f32),
            jax.ShapeDtypeStruct((DEC_BATCH, WINDOW, KV_DIM), _f32),
            jax.ShapeDtypeStruct((DEC_BATCH, WINDOW, KV_DIM), _f32),
        ],
        scratch_shapes=[
            pltpu.VMEM((c, N_KEYS_PAD, KV_DIM), _f32),
            pltpu.VMEM((c, N_KEYS_PAD, KV_DIM), _f32),
        ],
        compiler_params=_params(("arbitrary",)),
        name="attn_sample",
    )(q_s, k_new, v_new, cache_k, cache_v, sink_rows)


def _gate_kernel(*refs):
    h_ref, cb_ref, atm_ref, ats_ref = refs[:4]
    w_refs = refs[4:4 + 4 * GATE_PIECES]
    tm_ref, ts_ref, at_all = refs[4 + 4 * GATE_PIECES:]
    at_all[0:TM, :] = atm_ref[...]
    at_all[TM:, :] = ats_ref[...]
    h = h_ref[...]
    cb = cb_ref[...]
    at = at_all[...]
    for c in range(GATE_PIECES):
        wgc_ref, wga_ref, wc_ref, wa_ref = w_refs[4 * c:4 * c + 4]
        cols = slice(c * GATE_PIECE, (c + 1) * GATE_PIECE)
        gc = jnp.dot(h, wgc_ref[...], preferred_element_type=_f32)
        ga = jnp.dot(h, wga_ref[...], preferred_element_type=_f32)
        yc = jnp.dot(cb, wc_ref[...], preferred_element_type=_f32)
        ya = jnp.dot(at, wa_ref[...], preferred_element_type=_f32)
        t = (jax.nn.sigmoid(gc) * yc + jax.nn.sigmoid(ga) * ya).astype(_bf16)
        tm_ref[:, cols] = t[0:TM, :]
        ts_ref[:, cols] = t[TM:, :]


def _gate(h_all, cb_all, atm, ats, w_in_b, wc, wa):
    row = lambda n, i: (i, 0)
    once = pl.Buffered(1)
    w_specs, w_args = [], []
    for c in range(GATE_PIECES):
        piece = lambda n, i, c=c, base=0: (0, base + n * GATE_PIECES + c)
        w_specs += [
            pl.BlockSpec((D_MODEL, GATE_PIECE),
                         functools.partial(piece, base=OFF_GC // GATE_PIECE), pipeline_mode=once),
            pl.BlockSpec((D_MODEL, GATE_PIECE),
                         functools.partial(piece, base=OFF_GA // GATE_PIECE), pipeline_mode=once),
            pl.BlockSpec((D_CONV, GATE_PIECE), piece, pipeline_mode=once),
            pl.BlockSpec((Q_DIM, GATE_PIECE), piece, pipeline_mode=once),
        ]
        w_args += [w_in_b, w_in_b, wc, wa]
    return pl.pallas_call(
        _gate_kernel,
        grid=(N_OUT_CHUNKS, N_TILES),
        in_specs=[
            pl.BlockSpec((TILE, D_MODEL), row),
            pl.BlockSpec((TILE, D_CONV), row),
            pl.BlockSpec((TM, Q_DIM), row),
            pl.BlockSpec((TS, Q_DIM), row),
        ] + w_specs,
        out_specs=[
            pl.BlockSpec((TM, TN_OUT), lambda n, i: (i, n)),
            pl.BlockSpec((TS, TN_OUT), lambda n, i: (i, n)),
        ],
        out_shape=[
            jax.ShapeDtypeStruct((N_MAIN, D_MODEL), _bf16),
            jax.ShapeDtypeStruct((N_SMALL, D_MODEL), _bf16),
        ],
        scratch_shapes=[pltpu.VMEM((TILE, Q_DIM), _bf16)],
        compiler_params=_params(("arbitrary", "arbitrary")),
        name="gate",
    )(h_all, cb_all, atm, ats, *w_args)


def _sample_rows(a):
    return a.transpose(1, 0, 2).reshape(N_SAMPLE, a.shape[-1])


def _to_small(sample_rows, dtype):
    w = sample_rows.shape[-1]
    return jnp.concatenate([
        jnp.zeros((SAMPLE_ROW0, w), dtype), sample_rows.astype(dtype),
        jnp.zeros((N_SMALL - SAMPLE_ROW0 - N_SAMPLE, w), dtype)], axis=0)


def kernel(x_prompt, x_sample, state_conv, cache_k_win, cache_v_win, meta_tokens, ffn1_norm, ffn1_w_up, ffn1_w_down, mix_norm, w_in, q_norm, k_norm, conv_w, w_conv_out, attn_sinks, w_attn_out, w_o, ffn2_norm, ffn2_w_up, ffn2_w_down):
    l = 0
    xm = x_prompt.reshape(N_MAIN, D_MODEL)
    xs = jnp.concatenate([
        meta_tokens.astype(_f32), _sample_rows(x_sample),
        jnp.zeros((N_SMALL - N_META - N_SAMPLE, D_MODEL), _f32)], axis=0)

    g1 = ffn1_norm[l].reshape(1, D_MODEL)
    g_mix = mix_norm[l].reshape(1, D_MODEL)
    *first_tile, wgu1, wd1 = _ffn_first(xm, xs, g1, g_mix, ffn1_w_up[l], ffn1_w_down[l])
    jobs = (_w_up_job(16), _w_down_job(32), _cast_job(D_MODEL, D_IN, 16),
            _cast_job(D_CONV, D_MODEL, 16), _cast_job(Q_DIM, D_MODEL, 16),
            _cast_job(D_MODEL, D_MODEL, 16))
    x1m, x1s, h_all, wgu2, wd2, w_in_b, wc_b, wa_b, wo_b = _ffn_main(
        xm, xs, g1, wgu1, wd1, g_mix, first_tile, jobs,
        (ffn2_w_up[l], ffn2_w_down[l], w_in[l], w_conv_out[l], w_attn_out[l], w_o[l]))

    state_rows = state_conv[l].transpose(1, 0, 2).reshape((CONV_WIDTH - 1) * DEC_BATCH, D_CONV)
    cb_all, qm, km, vm, qkv_s, new_conv_p, new_conv_s = _inproj(
        h_all, w_in_b, conv_w[l], state_rows, q_norm[l], k_norm[l])

    atm, new_k_p, new_v_p = _attn_prompt(qm, km, vm, qkv_s, attn_sinks[l])
    samp = qkv_s[SAMPLE_ROW0:SAMPLE_ROW0 + N_SAMPLE]
    q_s = (samp[:, 0:Q_DIM].reshape(DEC_SEQ, DEC_BATCH, N_HEADS, HEAD_DIM)
           .transpose(1, 2, 0, 3).reshape(DEC_BATCH, N_QROWS, HEAD_DIM))
    q_s = jnp.tile(q_s, (1, 1, N_KV_HEADS))
    k_new = samp[:, Q_DIM:Q_DIM + KV_DIM].reshape(DEC_SEQ, DEC_BATCH, KV_DIM).transpose(1, 0, 2)
    v_new = samp[:, Q_DIM + KV_DIM:N_QKV].reshape(DEC_SEQ, DEC_BATCH, KV_DIM).transpose(1, 0, 2)
    sink_rows = jnp.repeat(attn_sinks[l].astype(_f32), DEC_SEQ).reshape(N_QROWS, 1)
    o_s, new_k_s, new_v_s = _attn_sample(
        q_s, k_new, v_new,
        cache_k_win[l].reshape(DEC_BATCH, WINDOW, KV_DIM),
        cache_v_win[l].reshape(DEC_BATCH, WINDOW, KV_DIM), sink_rows)
    at_rows = (o_s.reshape(DEC_BATCH, N_HEADS, DEC_SEQ, HEAD_DIM)
               .transpose(2, 0, 1, 3).reshape(N_SAMPLE, Q_DIM))
    ats = _to_small(at_rows, _bf16)

    t_main, t_small = _gate(h_all, cb_all, atm, ats, w_in_b, wc_b, wa_b)
    ym, ys = _ffn_stream(x1m, x1s, ffn2_norm[l].reshape(1, D_MODEL), wgu2, wd2,
                         proj=(t_main, t_small, wo_b))

    y_prompt = ym.reshape(BATCH, SEQ, D_MODEL)
    y_sample = (ys[SAMPLE_ROW0:SAMPLE_ROW0 + N_SAMPLE]
                .reshape(DEC_SEQ, DEC_BATCH, D_MODEL).transpose(1, 0, 2))
    kv_shape_p = (1, BATCH, WINDOW, N_KV_HEADS, HEAD_DIM)
    kv_shape_s = (1, DEC_BATCH, WINDOW, N_KV_HEADS, HEAD_DIM)
    return (y_prompt, y_sample,
            new_conv_p[None],
            new_k_p.reshape(kv_shape_p), new_v_p.reshape(kv_shape_p),
            new_conv_s.reshape(CONV_WIDTH - 1, DEC_BATCH, D_CONV).transpose(1, 0, 2)[None],
            new_k_s.reshape(kv_shape_s), new_v_s.reshape(kv_shape_s))
```
